```python
import math
import jax, jax.numpy as jnp
from jax import lax
import numpy as np

D_MODEL = 4096
BATCH = 4
SEQ = 2048
DEPTH = 2
DEC_BATCH = 8
DEC_SEQ = 64
PAST_LEN = 4096

CHUNK = 64
EPS = 1e-6
NEG = -1e30
D_FF = 2 * D_MODEL
MLA_HEADS = D_MODEL // 256
Q_LORA = D_MODEL // 4
KV_LORA = D_MODEL // 8
QK_NOPE = 128
QK_ROPE = 64
V_HEAD = 128
ROPE_THETA = 10000.0
MLA_SCALE = (QK_NOPE + QK_ROPE) ** -0.5
QBLOCK = 128
BAND_HEADS = D_MODEL // 512
BAND_DIM = 128
BAND_PREV = 8
BAND_ROWS = BAND_PREV * CHUNK
REL_CLIP = 128
BAND_SCALE = BAND_DIM ** -0.5
GDN_HEADS = D_MODEL // 512
GDN_DK = 128
GDN_DV = 128
CONV_W = 4
GDN_QK = GDN_HEADS * GDN_DK
GDN_V = GDN_HEADS * GDN_DV
C_CONV = 2 * GDN_QK + GDN_V
A_COLS = Q_LORA + KV_LORA + QK_ROPE
B_COLS = 3 * BAND_HEADS * BAND_DIM
C_COLS = C_CONV + GDN_V + 2 * GDN_HEADS
IN_COLS = A_COLS + B_COLS + C_COLS
MIX_WIDTH = MLA_HEADS * V_HEAD + BAND_HEADS * BAND_DIM + GDN_HEADS * GDN_DV

kernel_name = "hybrid_streaming_mla_band_gdn_step"


def rms_norm(x, g):
    xf = x.astype(jnp.float32)
    y = xf * lax.rsqrt(jnp.mean(xf * xf, axis=-1, keepdims=True) + EPS)
    return (y * g.astype(jnp.float32)).astype(x.dtype)


def l2norm(x):
    return x * lax.rsqrt(jnp.sum(x * x, axis=-1, keepdims=True) + EPS)


def swiglu(h, w1, w3, w2):
    return (jax.nn.silu(h @ w1) * (h @ w3)) @ w2


def rope_cos_sin(pos):
    inv = 1.0 / (ROPE_THETA ** (jnp.arange(0, QK_ROPE, 2, dtype=jnp.float32) / QK_ROPE))
    ang = pos.astype(jnp.float32)[:, None] * inv[None, :]
    return jnp.cos(ang), jnp.sin(ang)


def apply_rope(x, cos, sin):
    xf = x.astype(jnp.float32)
    x1, x2 = xf[..., :QK_ROPE // 2], xf[..., QK_ROPE // 2:]
    return jnp.concatenate([x1 * cos - x2 * sin, x2 * cos + x1 * sin], axis=-1).astype(x.dtype)


def mla_attend(q_lat, q_rope, c_kv, k_rope, q_pos, k_pos, w_uv):
    B, Sq = q_lat.shape[:2]
    k_chunk = k_pos // CHUNK

    def attend(blk):
        ql, qr, qp = blk
        s = jnp.einsum('bqhc,bkc->bhqk', ql, c_kv) + jnp.einsum('bqhr,bkr->bhqk', qr, k_rope)
        s = s.astype(jnp.float32) * MLA_SCALE
        mask = k_chunk[None, :] <= (qp // CHUNK)[:, None]
        s = jnp.where(mask[None, None], s, NEG)
        p = jax.nn.softmax(s, axis=-1).astype(c_kv.dtype)
        ctx = jnp.einsum('bhqk,bkc->bqhc', p, c_kv)
        return jnp.einsum('bqhc,chd->bqhd', ctx, w_uv)

    if Sq > QBLOCK and Sq % QBLOCK == 0:
        nb = Sq // QBLOCK

        def split(t):
            return jnp.moveaxis(t.reshape((B, nb, QBLOCK) + t.shape[2:]), 1, 0)
        out = lax.map(attend, (split(q_lat), split(q_rope), q_pos.reshape(nb, QBLOCK)))
        return jnp.moveaxis(out, 0, 1).reshape(B, Sq, MLA_HEADS, V_HEAD)
    return attend((q_lat, q_rope, q_pos))


def mla_mixer(cols, pos, q_norm, w_qb, kv_norm, w_uk, w_uv, past_lat, past_rope):
    B, S = cols.shape[:2]
    q_a = cols[..., :Q_LORA]
    c_kv = cols[..., Q_LORA:Q_LORA + KV_LORA]
    k_r = cols[..., Q_LORA + KV_LORA:]
    q = (rms_norm(q_a, q_norm) @ w_qb).reshape(B, S, MLA_HEADS, QK_NOPE + QK_ROPE)
    cos, sin = rope_cos_sin(pos)
    q_rope = apply_rope(q[..., QK_NOPE:], cos[:, None], sin[:, None])
    q_lat = jnp.einsum('bshd,chd->bshc', q[..., :QK_NOPE], w_uk)
    c_kv = rms_norm(c_kv, kv_norm)
    k_r = apply_rope(k_r, cos, sin)
    if past_lat is None:
        keys_c, keys_r, k_pos = c_kv, k_r, pos
    else:
        P = past_lat.shape[1]
        keys_c = jnp.concatenate([past_lat, c_kv], axis=1)
        keys_r = jnp.concatenate([past_rope, k_r], axis=1)
        k_pos = jnp.concatenate([pos[0] - P + jnp.arange(P, dtype=jnp.int32), pos])
    out = mla_attend(q_lat, q_rope, keys_c, keys_r, pos, k_pos, w_uv)
    return out.reshape(B, S, MLA_HEADS * V_HEAD), c_kv, k_r


def band_attend(q, k, v, q_pos, k_pos, rel_bias):
    s = jnp.einsum('bnqhd,bnkhd->bnhqk', q, k).astype(jnp.float32) * BAND_SCALE
    rel = jnp.clip(q_pos[:, :, None] - k_pos[:, None, :], -REL_CLIP, REL_CLIP) + REL_CLIP
    bias = jnp.moveaxis(rel_bias[:, rel], 0, 1).astype(jnp.float32)
    qc = (q_pos // CHUNK)[:, :, None]
    kc = (k_pos // CHUNK)[:, None, :]
    mask = (k_pos[:, None, :] >= 0) & (kc <= qc) & (kc >= qc - BAND_PREV)
    s = jnp.where(mask[None, :, None], s + bias[None], NEG)
    p = jax.nn.softmax(s, axis=-1).astype(v.dtype)
    return jnp.einsum('bnhqk,bnkhd->bnqhd', p, v)


def band_mixer(cols, pos, rel_bias, past_k, past_v):
    B, S = cols.shape[:2]
    hd = BAND_HEADS * BAND_DIM
    q = cols[..., :hd].reshape(B, S, BAND_HEADS, BAND_DIM)
    k = cols[..., hd:2 * hd].reshape(B, S, BAND_HEADS, BAND_DIM)
    v = cols[..., 2 * hd:].reshape(B, S, BAND_HEADS, BAND_DIM)
    if past_k is None:
        nc = S // CHUNK

        def band(t):
            tp = jnp.pad(t.reshape(B, nc, CHUNK, BAND_HEADS, BAND_DIM),
                         ((0, 0), (BAND_PREV, 0), (0, 0), (0, 0), (0, 0)))
            return jnp.concatenate([tp[:, j:j + nc] for j in range(BAND_PREV + 1)], axis=2)
        q_pos = pos.reshape(nc, CHUNK)
        k_pos = ((jnp.arange(nc, dtype=jnp.int32)[:, None] - BAND_PREV) * CHUNK
                 + jnp.arange((BAND_PREV + 1) * CHUNK, dtype=jnp.int32)[None, :])
        o = band_attend(q.reshape(B, nc, CHUNK, BAND_HEADS, BAND_DIM), band(k), band(v),
                        q_pos, k_pos, rel_bias)
        keep = min(BAND_ROWS, S)
        new_k, new_v = k[:, S - keep:], v[:, S - keep:]
    else:
        Lb = past_k.shape[1]
        kk = jnp.concatenate([past_k, k], axis=1)[:, None]
        vv = jnp.concatenate([past_v, v], axis=1)[:, None]
        k_pos = jnp.concatenate([pos[0] - Lb + jnp.arange(Lb, dtype=jnp.int32), pos])[None]
        o = band_attend(q[:, None], kk, vv, pos[None], k_pos, rel_bias)
        new_k, new_v = k, v
    return o.reshape(B, S, hd), new_k, new_v


def gated_delta(q, k, v, g, beta, s0):
    B, S, H, _ = q.shape
    L = CHUNK if S % CHUNK == 0 else S
    N = S // L

    def blocks(t):
        return jnp.moveaxis(t.reshape((B, N, L) + t.shape[2:]), 3, 2)
    q, k, v, g, beta = blocks(q), blocks(k), blocks(v), blocks(g), blocks(beta)
    gc = jnp.cumsum(g, axis=-1)
    idx = jnp.arange(L)
    incl = idx[:, None] >= idx[None, :]
    strict = idx[:, None] > idx[None, :]
    diff = gc[..., :, None] - gc[..., None, :]
    decay = jnp.where(incl, jnp.exp(jnp.where(incl, diff, 0.0)), 0.0)
    kk = jnp.einsum('bnhid,bnhjd->bnhij', k, k)
    a_mat = jnp.where(strict, beta[..., :, None] * kk * decay, 0.0)
    eye = jnp.eye(L, dtype=jnp.float32)
    rhs = jnp.concatenate([beta[..., None] * v, (beta * jnp.exp(gc))[..., None] * k], axis=-1)
    sol = lax.linalg.triangular_solve(a_mat + eye, rhs, left_side=True, lower=True, unit_diagonal=True)
    u_v, w_k = sol[..., :GDN_DV], sol[..., GDN_DV:]
    attn = jnp.einsum('bnhid,bnhjd->bnhij', q, k) * decay
    q_g = q * jnp.exp(gc)[..., None]
    k_t = k * jnp.exp(gc[..., -1:] - gc)[..., None]
    g_last = jnp.exp(gc[..., -1])

    def step(s, xs):
        u_v_c, w_k_c, attn_c, q_g_c, k_t_c, g_l = xs
        u = u_v_c - jnp.einsum('bhld,bhde->bhle', w_k_c, s)
        o = jnp.einsum('bhld,bhde->bhle', q_g_c, s) + jnp.einsum('bhij,bhje->bhie', attn_c, u)
        s = g_l[..., None, None] * s + jnp.einsum('bhld,bhle->bhde', k_t_c, u)
        return s, o

    xs = tuple(jnp.moveaxis(t, 1, 0) for t in (u_v, w_k, attn, q_g, k_t, g_last))
    s_fin, o = lax.scan(step, s0, xs)
    o = jnp.transpose(o, (1, 0, 3, 2, 4)).reshape(B, S, H, GDN_DV)
    return o, s_fin


def gdn_mixer(cols, conv_w, a_log, dt_bias, o_norm, conv_buf, s0):
    B, S = cols.shape[:2]
    u = cols[..., :C_CONV]
    z = cols[..., C_CONV:C_CONV + GDN_V].reshape(B, S, GDN_HEADS, GDN_DV)
    a = cols[..., C_CONV + GDN_V:C_CONV + GDN_V + GDN_HEADS].astype(jnp.float32)
    b = cols[..., C_CONV + GDN_V + GDN_HEADS:].astype(jnp.float32)
    if conv_buf is None:
        conv_buf = jnp.zeros((B, CONV_W - 1, C_CONV), u.dtype)
    if s0 is None:
        s0 = jnp.zeros((B, GDN_HEADS, GDN_DK, GDN_DV), jnp.float32)
    up = jnp.concatenate([conv_buf.astype(u.dtype), u], axis=1)
    uc = lax.conv_general_dilated(up, conv_w[:, None, :].astype(up.dtype), (1,), 'VALID',
                                  dimension_numbers=('NWC', 'WIO', 'NWC'),
                                  feature_group_count=C_CONV)
    uc = jax.nn.silu(uc).astype(jnp.float32)
    new_buf = up[:, -(CONV_W - 1):]
    q = l2norm(uc[..., :GDN_QK].reshape(B, S, GDN_HEADS, GDN_DK)) * (GDN_DK ** -0.5)
    k = l2norm(uc[..., GDN_QK:2 * GDN_QK].reshape(B, S, GDN_HEADS, GDN_DK))
    v = uc[..., 2 * GDN_QK:].reshape(B, S, GDN_HEADS, GDN_DV)
    beta = jax.nn.sigmoid(b)
    g = -jnp.exp(a_log.astype(jnp.float32)) * jax.nn.softplus(a + dt_bias.astype(jnp.float32))
    o, s_new = gated_delta(q, k, v, g, beta, s0.astype(jnp.float32))
    o = rms_norm(o, o_norm) * jax.nn.silu(z.astype(jnp.float32))
    return o.astype(cols.dtype).reshape(B, S, GDN_V), new_buf, s_new.astype(cols.dtype)


def run_trunk(x, pos, past, params, final_norm):
    (norm_ff1, ff1_w1, ff1_w3, ff1_w2, norm_mix, w_in, q_norm, w_qb, kv_norm, w_uk, w_uv,
     rel_bias, conv_w, a_log, dt_bias, gdn_norm, w_out, norm_ff2, ff2_w1, ff2_w3, ff2_w2) = params
    lat_l, rope_l, bk_l, bv_l, conv_l, delta_l = [], [], [], [], [], []
    for l in range(DEPTH):
        if past is None:
            p_lat = p_rope = p_bk = p_bv = p_conv = p_delta = None
        else:
            p_lat, p_rope, p_bk, p_bv, p_conv, p_delta = (t[l] for t in past)
        x = x + 0.5 * swiglu(rms_norm(x, norm_ff1[l]), ff1_w1[l], ff1_w3[l], ff1_w2[l])
        cols = rms_norm(x, norm_mix[l]) @ w_in[l]
        ya, lat, kr = mla_mixer(cols[..., :A_COLS], pos, q_norm[l], w_qb[l], kv_norm[l],
                                w_uk[l], w_uv[l], p_lat, p_rope)
        yb, bk, bv = band_mixer(cols[..., A_COLS:A_COLS + B_COLS], pos, rel_bias[l], p_bk, p_bv)
        yc, cbuf, dst = gdn_mixer(cols[..., A_COLS + B_COLS:], conv_w[l], a_log[l], dt_bias[l],
                                  gdn_norm[l], p_conv, p_delta)
        x = x + jnp.concatenate([ya, yb, yc], axis=-1) @ w_out[l]
        x = x + 0.5 * swiglu(rms_norm(x, norm_ff2[l]), ff2_w1[l], ff2_w3[l], ff2_w2[l])
        lat_l.append(lat); rope_l.append(kr); bk_l.append(bk); bv_l.append(bv)
        conv_l.append(cbuf); delta_l.append(dst)
    y = rms_norm(x, final_norm)
    return (y, jnp.stack(lat_l), jnp.stack(rope_l), jnp.stack(bk_l), jnp.stack(bv_l),
            jnp.stack(conv_l), jnp.stack(delta_l))


def setup_inputs(seed: int = 0) -> dict:
    key = jax.random.key(seed)
    keys = list(jax.random.split(key, 40))
    f32 = jnp.float32

    def nrm(shape, scale):
        return jax.random.normal(keys.pop(), shape, f32) * scale

    def gain(shape):
        return 1.0 + nrm(shape, 0.01)

    band_rows = min(BAND_ROWS, PAST_LEN)
    dt = jnp.exp(jax.random.uniform(keys.pop(), (DEPTH, GDN_HEADS), f32, math.log(1e-3), math.log(1e-1)))
    dt_bias = dt + jnp.log(-jnp.expm1(-dt))
    a_log = jnp.log(jax.random.uniform(keys.pop(), (DEPTH, GDN_HEADS), f32, 1.0, 16.0))
    return {
        "x_prompt": nrm((BATCH, SEQ, D_MODEL), 1.0),
        "x_sample": nrm((DEC_BATCH, DEC_SEQ, D_MODEL), 1.0),
        "cache_latent": nrm((DEPTH, DEC_BATCH, PAST_LEN, KV_LORA), 1.0),
        "cache_k_rope": nrm((DEPTH, DEC_BATCH, PAST_LEN, QK_ROPE), 1.0),
        "cache_band_k": nrm((DEPTH, DEC_BATCH, band_rows, BAND_HEADS, BAND_DIM), 1.0),
        "cache_band_v": nrm((DEPTH, DEC_BATCH, band_rows, BAND_HEADS, BAND_DIM), 1.0),
        "state_conv": nrm((DEPTH, DEC_BATCH, CONV_W - 1, C_CONV), 1.0),
        "state_delta": nrm((DEPTH, DEC_BATCH, GDN_HEADS, GDN_DK, GDN_DV), 0.1),
        "norm_ff1": gain((DEPTH, D_MODEL)),
        "ff1_w1": nrm((DEPTH, D_MODEL, D_FF), D_MODEL ** -0.5),
        "ff1_w3": nrm((DEPTH, D_MODEL, D_FF), D_MODEL ** -0.5),
        "ff1_w2": nrm((DEPTH, D_FF, D_MODEL), D_FF ** -0.5),
        "norm_mix": gain((DEPTH, D_MODEL)),
        "w_in": nrm((DEPTH, D_MODEL, IN_COLS), D_MODEL ** -0.5),
        "q_norm": gain((DEPTH, Q_LORA)),
        "w_qb": nrm((DEPTH, Q_LORA, MLA_HEADS * (QK_NOPE + QK_ROPE)), Q_LORA ** -0.5),
        "kv_norm": gain((DEPTH, KV_LORA)),
        "w_uk": nrm((DEPTH, KV_LORA, MLA_HEADS, QK_NOPE), KV_LORA ** -0.5),
        "w_uv": nrm((DEPTH, KV_LORA, MLA_HEADS, V_HEAD), KV_LORA ** -0.5),
        "rel_bias": nrm((DEPTH, BAND_HEADS, 2 * REL_CLIP + 1), 0.5),
        "conv_w": nrm((DEPTH, CONV_W, C_CONV), CONV_W ** -0.5),
        "a_log": a_log,
        "dt_bias": dt_bias,
        "gdn_norm": gain((DEPTH, GDN_DV)),
        "w_out": nrm((DEPTH, MIX_WIDTH, D_MODEL), MIX_WIDTH ** -0.5),
        "norm_ff2": gain((DEPTH, D_MODEL)),
        "ff2_w1": nrm((DEPTH, D_MODEL, D_FF), D_MODEL ** -0.5),
        "ff2_w3": nrm((DEPTH, D_MODEL, D_FF), D_MODEL ** -0.5),
        "ff2_w2": nrm((DEPTH, D_FF, D_MODEL), D_FF ** -0.5),
        "final_norm": gain((D_MODEL,)),
    }


def reference(x_prompt, x_sample, cache_latent, cache_k_rope, cache_band_k, cache_band_v,
              state_conv, state_delta, norm_ff1, ff1_w1, ff1_w3, ff1_w2, norm_mix, w_in,
              q_norm, w_qb, kv_norm, w_uk, w_uv, rel_bias, conv_w, a_log, dt_bias, gdn_norm,
              w_out, norm_ff2, ff2_w1, ff2_w3, ff2_w2, final_norm):
    params = (norm_ff1, ff1_w1, ff1_w3, ff1_w2, norm_mix, w_in, q_norm, w_qb, kv_norm, w_uk, w_uv,
              rel_bias, conv_w, a_log, dt_bias, gdn_norm, w_out, norm_ff2, ff2_w1, ff2_w3, ff2_w2)
    pos_p = jnp.arange(x_prompt.shape[1], dtype=jnp.int32)
    y_prompt, p_lat, p_rope, p_bk, p_bv, p_conv, p_delta = run_trunk(
        x_prompt, pos_p, None, params, final_norm)
    past_len = cache_latent.shape[2]
    pos_s = past_len + jnp.arange(x_sample.shape[1], dtype=jnp.int32)
    past = (cache_latent, cache_k_rope, cache_band_k, cache_band_v, state_conv, state_delta)
    y_sample, s_lat, s_rope, s_bk, s_bv, s_conv, s_delta = run_trunk(
        x_sample, pos_s, past, params, final_norm)
    return (y_prompt, y_sample, p_lat, p_rope, p_bk, p_bv, p_conv, p_delta,
            s_lat, s_rope, s_bk, s_bv, s_conv, s_delta)
```

```python
import functools
import math

import jax
import jax.numpy as jnp
from jax import lax
from jax.experimental import pallas as pl
from jax.experimental.pallas import tpu as pltpu

F32 = jnp.float32
BF16 = jnp.bfloat16
HIGHEST = lax.Precision.HIGHEST

CHUNK = 64
EPS = 1e-6
NEG = -1e30
Q_LORA = 1024
KV_LORA = 512
QK_NOPE = 128
QK_ROPE = 64
V_HEAD = 128
MLA_HEADS = 16
ROPE_THETA = 10000.0
MLA_SCALE = (QK_NOPE + QK_ROPE) ** -0.5
BAND_HEADS = 8
BAND_DIM = 128
BAND_PREV = 8
BAND_ROWS = BAND_PREV * CHUNK
BAND_KEYS = BAND_ROWS + CHUNK
REL_CLIP = 128
BAND_SCALE = BAND_DIM ** -0.5
GDN_HEADS = 8
GDN_DK = 128
GDN_DV = 128
CONV_W = 4
GDN_QK = GDN_HEADS * GDN_DK
GDN_V = GDN_HEADS * GDN_DV
C_CONV = 2 * GDN_QK + GDN_V
LANE = 128
SLOT = 1024
COL_U, COL_QA, COL_BQ, COL_BK, COL_BV, COL_Z, COL_TAIL = 0, 3, 4, 5, 6, 7, 8
IN_COLS_PAD = 9 * SLOT
TAIL_KR, TAIL_KSW, TAIL_A, TAIL_B = 512, 640, 768, 896

VMEM_LIMIT_BYTES = 56 * 1024 * 1024


def _params(*sem):
    return pltpu.CompilerParams(dimension_semantics=sem, vmem_limit_bytes=VMEM_LIMIT_BYTES)


def _tile(n, target, mult=16):
    best = None
    for t in range(mult, min(n, target) + 1, mult):
        if n % t == 0:
            best = t
    assert best is not None, (n, target)
    return best


def _dot(a, b, precision=None):
    return jnp.dot(a, b, preferred_element_type=F32, precision=precision)


def _dot_nt(a, b, precision=None):
    return lax.dot_general(a, b, (((1,), (1,)), ((), ())), preferred_element_type=F32,
                           precision=precision)


def _dot_tn(a, b, precision=None):
    return lax.dot_general(a, b, (((0,), (0,)), ((), ())), preferred_element_type=F32,
                           precision=precision)


def _sigmoid(x):
    return 1.0 / (1.0 + jnp.exp(-x))


def _rms(x, g):
    return x * lax.rsqrt(jnp.mean(x * x, axis=-1, keepdims=True) + EPS) * g


def _rmsnorm_kernel(x_ref, g_ref, o_ref):
    o_ref[...] = _rms(x_ref[...], g_ref[...]).astype(o_ref.dtype)


def rmsnorm(x, g, out_dtype, rows=None, row_start=0):
    m, d = x.shape
    rows = m if rows is None else rows
    tm = _tile(math.gcd(rows, row_start) if row_start else rows, 256, 8)
    off = row_start // tm
    return pl.pallas_call(
        _rmsnorm_kernel,
        grid=(rows // tm,),
        in_specs=[pl.BlockSpec((tm, d), lambda i: (i + off, 0)),
                  pl.BlockSpec((1, d), lambda i: (0, 0))],
        out_specs=pl.BlockSpec((tm, d), lambda i: (i, 0)),
        out_shape=jax.ShapeDtypeStruct((rows, d), out_dtype),
        compiler_params=_params("parallel"),
        name="rmsnorm",
    )(x, g.reshape(1, d))


def _mm_kernel(a_ref, w_ref, o_ref):
    o_ref[...] = _dot(a_ref[...], w_ref[...]).astype(o_ref.dtype)


def _mm_res_kernel(a_ref, w_ref, r_ref, o_ref, *, scale):
    o_ref[...] = r_ref[...] + scale * _dot(a_ref[...], w_ref[...])


def _mm_swiglu_kernel(a_ref, w1_ref, w3_ref, o_ref):
    a = a_ref[...]
    h1 = _dot(a, w1_ref[...])
    h3 = _dot(a, w3_ref[...])
    o_ref[...] = (h1 * _sigmoid(h1) * h3).astype(o_ref.dtype)


def matmul(a, w, out_dtype, tm_target, tn_target):
    m, k = a.shape
    n = w.shape[1]
    tm, tn = _tile(m, tm_target), _tile(n, tn_target, LANE)
    return pl.pallas_call(
        _mm_kernel,
        grid=(n // tn, m // tm),
        in_specs=[pl.BlockSpec((tm, k), lambda j, i: (i, 0)),
                  pl.BlockSpec((k, tn), lambda j, i: (0, j))],
        out_specs=pl.BlockSpec((tm, tn), lambda j, i: (i, j)),
        out_shape=jax.ShapeDtypeStruct((m, n), out_dtype),
        compiler_params=_params("parallel", "parallel"),
        name="matmul",
    )(a, w)


def matmul_residual(a, w, res, scale, tm_target, tn_target):
    m, k = a.shape
    n = w.shape[1]
    tm, tn = _tile(m, tm_target), _tile(n, tn_target, LANE)
    return pl.pallas_call(
        functools.partial(_mm_res_kernel, scale=scale),
        grid=(n // tn, m // tm),
        in_specs=[pl.BlockSpec((tm, k), lambda j, i: (i, 0)),
                  pl.BlockSpec((k, tn), lambda j, i: (0, j)),
                  pl.BlockSpec((tm, tn), lambda j, i: (i, j))],
        out_specs=pl.BlockSpec((tm, tn), lambda j, i: (i, j)),
        out_shape=jax.ShapeDtypeStruct((m, n), F32),
        compiler_params=_params("parallel", "parallel"),
        name="matmul_residual",
    )(a, w, res)


def matmul_swiglu(a, w1, w3, tm_target, tn_target):
    m, k = a.shape
    n = w1.shape[1]
    tm, tn = _tile(m, tm_target), _tile(n, tn_target, LANE)
    return pl.pallas_call(
        _mm_swiglu_kernel,
        grid=(n // tn, m // tm),
        in_specs=[pl.BlockSpec((tm, k), lambda j, i: (i, 0)),
                  pl.BlockSpec((k, tn), lambda j, i: (0, j)),
                  pl.BlockSpec((k, tn), lambda j, i: (0, j))],
        out_specs=pl.BlockSpec((tm, tn), lambda j, i: (i, j)),
        out_shape=jax.ShapeDtypeStruct((m, n), BF16),
        compiler_params=_params("parallel", "parallel"),
        name="matmul_swiglu",
    )(a, w1, w3)


def _mla_prep_kernel(qa_ref, tail_ref, qn_ref, kvn_ref, wq_ref, wuk_ref, cos_ref, sin_ref,
                     qlat_ref, qrope_ref, lat_ref, krope_ref, klat_bf_ref, krope_bf_ref):
    hq = _rms(qa_ref[...], qn_ref[...]).astype(BF16)
    q_all = _dot(hq, wq_ref[...])
    cos = cos_ref[...]
    sin = sin_ref[...]
    hw = MLA_HEADS * LANE
    for h in range(MLA_HEADS):
        qn = q_all[:, h * LANE:(h + 1) * LANE].astype(BF16)
        qlat_ref[h] = _dot(qn, wuk_ref[h]).astype(BF16)
        qr = (q_all[:, hw + h * LANE:hw + (h + 1) * LANE] * cos
              + q_all[:, 2 * hw + h * LANE:2 * hw + (h + 1) * LANE] * sin)
        qrope_ref[h] = qr[:, :QK_ROPE].astype(BF16)
    tail = tail_ref[...]
    lat = _rms(tail[:, :KV_LORA], kvn_ref[...])
    lat_ref[...] = lat
    klat_bf_ref[...] = lat.astype(BF16)
    kr = tail[:, TAIL_KR:TAIL_KR + LANE] * cos + tail[:, TAIL_KSW:TAIL_KSW + LANE] * sin
    krope_ref[...] = kr[:, :QK_ROPE]
    krope_bf_ref[...] = kr[:, :QK_ROPE].astype(BF16)


def mla_prep(cols, q_norm, kv_norm, wq_all, wuk_t, cos128, sin128):
    m = cols.shape[0]
    tm = _tile(m, 256)
    h = MLA_HEADS
    const2 = lambda i: (0, 0)
    return pl.pallas_call(
        _mla_prep_kernel,
        grid=(m // tm,),
        in_specs=[pl.BlockSpec((tm, SLOT), lambda i: (i, COL_QA)),
                  pl.BlockSpec((tm, SLOT), lambda i: (i, COL_TAIL)),
                  pl.BlockSpec((1, Q_LORA), const2),
                  pl.BlockSpec((1, KV_LORA), const2),
                  pl.BlockSpec(wq_all.shape, const2, pipeline_mode=pl.Buffered(1)),
                  pl.BlockSpec(wuk_t.shape, lambda i: (0, 0, 0), pipeline_mode=pl.Buffered(1)),
                  pl.BlockSpec((tm, LANE), lambda i: (i, 0)),
                  pl.BlockSpec((tm, LANE), lambda i: (i, 0))],
        out_specs=[pl.BlockSpec((h, tm, KV_LORA), lambda i: (0, i, 0)),
                   pl.BlockSpec((h, tm, QK_ROPE), lambda i: (0, i, 0)),
                   pl.BlockSpec((tm, KV_LORA), lambda i: (i, 0)),
                   pl.BlockSpec((tm, QK_ROPE), lambda i: (i, 0)),
                   pl.BlockSpec((tm, KV_LORA), lambda i: (i, 0)),
                   pl.BlockSpec((tm, QK_ROPE), lambda i: (i, 0))],
        out_shape=[jax.ShapeDtypeStruct((h, m, KV_LORA), BF16),
                   jax.ShapeDtypeStruct((h, m, QK_ROPE), BF16),
                   jax.ShapeDtypeStruct((m, KV_LORA), F32),
                   jax.ShapeDtypeStruct((m, QK_ROPE), F32),
                   jax.ShapeDtypeStruct((m, KV_LORA), BF16),
                   jax.ShapeDtypeStruct((m, QK_ROPE), BF16)],
        compiler_params=_params("parallel"),
        name="mla_prep",
    )(cols, cols, q_norm.reshape(1, -1), kv_norm.reshape(1, -1), wq_all, wuk_t, cos128, sin128)


def _softmax_init(m_ref, l_ref, acc_ref):
    m_ref[...] = jnp.full(m_ref.shape, NEG, F32)
    l_ref[...] = jnp.zeros(l_ref.shape, F32)
    acc_ref[...] = jnp.zeros(acc_ref.shape, F32)


def _softmax_update(s, v, m_ref, l_ref, acc_ref):
    m_old = m_ref[...]
    m_new = jnp.maximum(m_old, jnp.max(s, axis=-1, keepdims=True))
    alpha = jnp.exp(m_old - m_new)
    p = jnp.exp(s - m_new)
    l_ref[...] = alpha * l_ref[...] + jnp.sum(p, axis=-1, keepdims=True)
    acc_ref[...] = alpha * acc_ref[...] + _dot(p.astype(BF16), v)
    m_ref[...] = m_new


def _mla_finish(wuv_ref, o_ref, l_ref, acc_ref, tq):
    ctx = acc_ref[...] / l_ref[...]
    for h in range(MLA_HEADS):
        ch = ctx[h * tq:(h + 1) * tq].astype(BF16)
        o_ref[:, h * V_HEAD:(h + 1) * V_HEAD] = _dot(ch, wuv_ref[h]).astype(o_ref.dtype)


def _mla_prompt_kernel(ql_ref, qr_ref, kl_ref, kr_ref, wuv_ref, o_ref, m_ref, l_ref, acc_ref,
                       *, tq, tk):
    qi = pl.program_id(1)
    kj = pl.program_id(2)
    last = (qi * tq + tq - 1) // tk

    @pl.when(kj == 0)
    def _():
        _softmax_init(m_ref, l_ref, acc_ref)

    def scores():
        q_l = ql_ref[...].reshape(MLA_HEADS * tq, KV_LORA)
        q_r = qr_ref[...].reshape(MLA_HEADS * tq, QK_ROPE)
        return (_dot_nt(q_l, kl_ref[...]) + _dot_nt(q_r, kr_ref[...])) * MLA_SCALE

    @pl.when(kj < last)
    def _():
        _softmax_update(scores(), kl_ref[...], m_ref, l_ref, acc_ref)

    @pl.when(kj == last)
    def _():
        s = scores()
        row = lax.broadcasted_iota(jnp.int32, s.shape, 0)
        col = lax.broadcasted_iota(jnp.int32, s.shape, 1)
        q_chunk = (qi * tq + row % tq) // CHUNK
        k_chunk = (kj * tk + col) // CHUNK
        s = jnp.where(k_chunk <= q_chunk, s, NEG)
        _softmax_update(s, kl_ref[...], m_ref, l_ref, acc_ref)

    @pl.when(kj == pl.num_programs(2) - 1)
    def _():
        _mla_finish(wuv_ref, o_ref, l_ref, acc_ref, tq)


def mla_attention_prompt(q_lat, q_rope, k_lat, k_rope, wuv, batch, seq):
    tq = 128
    tk = _tile(seq, 512, CHUNK)
    nq, nk = seq // tq, seq // tk
    h = MLA_HEADS
    rows = h * tq

    def k_map(b, qi, kj):
        return (b * nk + jnp.minimum(kj, (qi * tq + tq - 1) // tk), 0)

    return pl.pallas_call(
        functools.partial(_mla_prompt_kernel, tq=tq, tk=tk),
        grid=(batch, nq, nk),
        in_specs=[pl.BlockSpec((h, tq, KV_LORA), lambda b, qi, kj: (0, b * nq + qi, 0)),
                  pl.BlockSpec((h, tq, QK_ROPE), lambda b, qi, kj: (0, b * nq + qi, 0)),
                  pl.BlockSpec((tk, KV_LORA), k_map),
                  pl.BlockSpec((tk, QK_ROPE), k_map),
                  pl.BlockSpec(wuv.shape, lambda b, qi, kj: (0, 0, 0))],
        out_specs=pl.BlockSpec((tq, h * V_HEAD), lambda b, qi, kj: (b * nq + qi, 0)),
        out_shape=jax.ShapeDtypeStruct((batch * seq, h * V_HEAD), BF16),
        scratch_shapes=[pltpu.VMEM((rows, 1), F32), pltpu.VMEM((rows, 1), F32),
                        pltpu.VMEM((rows, KV_LORA), F32)],
        compiler_params=_params("parallel", "parallel", "arbitrary"),
        name="mla_attention_prompt",
    )(q_lat, q_rope, k_lat, k_rope, wuv)


def _mla_sample_kernel(ql_ref, qr_ref, pl_ref, pr_ref, kl_ref, kr_ref, wuv_ref, o_ref,
                       m_ref, l_ref, acc_ref):
    kj = pl.program_id(1)
    n_past = pl.num_programs(1) - 1
    q_l = ql_ref[...].reshape(MLA_HEADS * CHUNK, KV_LORA)
    q_r = qr_ref[...].reshape(MLA_HEADS * CHUNK, QK_ROPE)

    @pl.when(kj == 0)
    def _():
        _softmax_init(m_ref, l_ref, acc_ref)

    @pl.when(kj < n_past)
    def _():
        k_l = pl_ref[0].astype(BF16)
        k_r = pr_ref[0].astype(BF16)
        s = (_dot_nt(q_l, k_l) + _dot_nt(q_r, k_r)) * MLA_SCALE
        _softmax_update(s, k_l, m_ref, l_ref, acc_ref)

    @pl.when(kj == n_past)
    def _():
        k_l = kl_ref[...]
        s = (_dot_nt(q_l, k_l) + _dot_nt(q_r, kr_ref[...])) * MLA_SCALE
        _softmax_update(s, k_l, m_ref, l_ref, acc_ref)
        _mla_finish(wuv_ref, o_ref, l_ref, acc_ref, CHUNK)


def mla_attention_sample(q_lat, q_rope, past_lat, past_rope, k_lat, k_rope, wuv, batch, tok_block0):
    past = past_lat.shape[1]
    tk = _tile(past, 512, CHUNK)
    n_past = past // tk
    h = MLA_HEADS
    rows = h * CHUNK
    return pl.pallas_call(
        _mla_sample_kernel,
        grid=(batch, n_past + 1),
        in_specs=[pl.BlockSpec((h, CHUNK, KV_LORA), lambda b, kj: (0, tok_block0 + b, 0)),
                  pl.BlockSpec((h, CHUNK, QK_ROPE), lambda b, kj: (0, tok_block0 + b, 0)),
                  pl.BlockSpec((1, tk, KV_LORA), lambda b, kj: (b, jnp.minimum(kj, n_past - 1), 0)),
                  pl.BlockSpec((1, tk, QK_ROPE), lambda b, kj: (b, jnp.minimum(kj, n_past - 1), 0)),
                  pl.BlockSpec((CHUNK, KV_LORA), lambda b, kj: (tok_block0 + b, 0)),
                  pl.BlockSpec((CHUNK, QK_ROPE), lambda b, kj: (tok_block0 + b, 0)),
                  pl.BlockSpec(wuv.shape, lambda b, kj: (0, 0, 0))],
        out_specs=pl.BlockSpec((CHUNK, h * V_HEAD), lambda b, kj: (b, 0)),
        out_shape=jax.ShapeDtypeStruct((batch * CHUNK, h * V_HEAD), BF16),
        scratch_shapes=[pltpu.VMEM((rows, 1), F32), pltpu.VMEM((rows, 1), F32),
                        pltpu.VMEM((rows, KV_LORA), F32)],
        compiler_params=_params("parallel", "arbitrary"),
        name="mla_attention_sample",
    )(q_lat, q_rope, past_lat, past_rope, k_lat, k_rope, wuv)


def _band_kernel(q_ref, k_ref, v_ref, bias_ref, o_ref, *, pad_rows):
    n = pl.program_id(1)
    start = pl.multiple_of(n * CHUNK, CHUNK)
    q = q_ref[...]
    for h in range(BAND_HEADS):
        lanes = slice(h * BAND_DIM, (h + 1) * BAND_DIM)
        qh = q[:, lanes].astype(BF16)
        kh = k_ref[0, pl.ds(start, BAND_KEYS), lanes]
        vh = v_ref[0, pl.ds(start, BAND_KEYS), lanes]
        s = _dot_nt(qh, kh) * BAND_SCALE + bias_ref[h]
        if pad_rows:
            col = lax.broadcasted_iota(jnp.int32, s.shape, 1)
            s = jnp.where(start + col >= pad_rows, s, NEG)
        p = jnp.exp(s - jnp.max(s, axis=-1, keepdims=True))
        o = _dot(p.astype(BF16), vh) / jnp.sum(p, axis=-1, keepdims=True)
        o_ref[:, lanes] = o.astype(o_ref.dtype)


def band_attention(cols, k_rows, v_rows, bias, batch, nchunks, tok_block0, pad_rows):
    width = BAND_HEADS * BAND_DIM
    rows = k_rows.shape[1]
    return pl.pallas_call(
        functools.partial(_band_kernel, pad_rows=pad_rows),
        grid=(batch, nchunks),
        in_specs=[pl.BlockSpec((CHUNK, width), lambda b, n: (tok_block0 + b * nchunks + n, COL_BQ)),
                  pl.BlockSpec((1, rows, width), lambda b, n: (b, 0, 0)),
                  pl.BlockSpec((1, rows, width), lambda b, n: (b, 0, 0)),
                  pl.BlockSpec(bias.shape, lambda b, n: (0, 0, 0))],
        out_specs=pl.BlockSpec((CHUNK, width), lambda b, n: (b * nchunks + n, 0)),
        out_shape=jax.ShapeDtypeStruct((batch * nchunks * CHUNK, width), BF16),
        compiler_params=_params("parallel", "parallel"),
        name="band_attention",
    )(cols, k_rows, v_rows, bias)


def _unit_lower_inverse(a, row, col):
    n = a.shape[0]
    t = (row == col).astype(F32)
    shift = 0
    while (1 << shift) < n:
        same_big = (row >> (shift + 1)) == (col >> (shift + 1))
        same_small = (row >> shift) == (col >> shift)
        off = jnp.where(same_big & jnp.logical_not(same_small), a, 0.0)
        t = t - _dot(_dot(t, off, HIGHEST), t, HIGHEST)
        shift += 1
    return t


def _gdn_pre_kernel(u_ref, uprev_ref, init_ref, tail_ref, cw_ref, aneg_ref, dtb_ref,
                    uv_ref, wk_ref, attn_ref, qg_ref, kt_ref, gl_ref, ext_ref,
                    *, chunks_per_seq, n_prompt_chunks):
    c = pl.program_id(0)
    is_start = jnp.logical_or(c >= n_prompt_chunks, c % chunks_per_seq == 0)
    ext_ref[0:8, :] = jnp.where(is_start, init_ref[0], uprev_ref[...])
    ext_ref[8:8 + CHUNK, :] = u_ref[...]
    cw = cw_ref[...]
    uc = (cw[3:4] * ext_ref[8:8 + CHUNK, :] + cw[2:3] * ext_ref[7:7 + CHUNK, :]
          + cw[1:2] * ext_ref[6:6 + CHUNK, :] + cw[0:1] * ext_ref[5:5 + CHUNK, :])
    uc = uc * _sigmoid(uc)

    tail = tail_ref[...]
    xa = tail[:, TAIL_A:TAIL_A + LANE] + dtb_ref[...]
    softplus = jnp.maximum(xa, 0.0) + jnp.log(1.0 + jnp.exp(-jnp.abs(xa)))
    g = aneg_ref[...] * softplus
    beta = _sigmoid(tail[:, TAIL_B:TAIL_B + LANE])

    row = lax.broadcasted_iota(jnp.int32, (CHUNK, CHUNK), 0)
    col = lax.broadcasted_iota(jnp.int32, (CHUNK, CHUNK), 1)
    incl = row >= col
    strict = row > col
    gc = _dot(incl.astype(F32), g, HIGHEST)
    er = lax.broadcasted_iota(jnp.int32, (8, LANE), 0)
    ec = lax.broadcasted_iota(jnp.int32, (8, LANE), 1)
    g_row = _dot_nt((er == ec).astype(F32), g, HIGHEST)
    gc_row = _dot(g_row, (row <= col).astype(F32), HIGHEST)
    egc = jnp.exp(gc)
    ekt = jnp.exp(gc[CHUNK - 1:CHUNK, :] - gc)
    gl_ref[0] = jnp.broadcast_to(jnp.exp(gc_row[:, CHUNK - 1:CHUNK]), (8, LANE))

    for h in range(GDN_HEADS):
        qh = uc[:, h * GDN_DK:(h + 1) * GDN_DK]
        kh = uc[:, GDN_QK + h * GDN_DK:GDN_QK + (h + 1) * GDN_DK]
        vh = uc[:, 2 * GDN_QK + h * GDN_DV:2 * GDN_QK + (h + 1) * GDN_DV]
        qh = qh * lax.rsqrt(jnp.sum(qh * qh, axis=-1, keepdims=True) + EPS) * (GDN_DK ** -0.5)
        kh = kh * lax.rsqrt(jnp.sum(kh * kh, axis=-1, keepdims=True) + EPS)
        diff = gc[:, h:h + 1] - gc_row[h:h + 1, :]
        decay = jnp.where(incl, jnp.exp(jnp.where(incl, diff, 0.0)), 0.0)
        kb = kh.astype(BF16)
        kk = _dot_nt(kb, kb)
        qk = _dot_nt(qh.astype(BF16), kb)
        bcol = beta[:, h:h + 1]
        a_mat = jnp.where(strict, bcol * kk * decay, 0.0)
        t_inv = _unit_lower_inverse(a_mat, row, col)
        rhs = jnp.concatenate([bcol * vh, (bcol * egc[:, h:h + 1]) * kh], axis=1)
        sol = _dot(t_inv, rhs, HIGHEST)
        uv_ref[h] = sol[:, :GDN_DV]
        wk_ref[h] = sol[:, GDN_DV:].astype(BF16)
        attn_ref[h] = (qk * decay).astype(BF16)
        qg_ref[h] = (qh * egc[:, h:h + 1]).astype(BF16)
        kt_ref[h] = (kh * ekt[:, h:h + 1]).astype(BF16)


def gdn_pre(cols, conv_init, conv_w, aneg, dtb, chunks_per_seq, n_prompt_chunks):
    m = cols.shape[0]
    nch = m // CHUNK
    h = GDN_HEADS

    def init_map(c):
        return (jnp.where(c < n_prompt_chunks, c // chunks_per_seq,
                          n_prompt_chunks // chunks_per_seq + c - n_prompt_chunks), 0, 0)

    head_spec = lambda d: pl.BlockSpec((h, CHUNK, d), lambda c: (0, c, 0))
    return pl.pallas_call(
        functools.partial(_gdn_pre_kernel, chunks_per_seq=chunks_per_seq,
                          n_prompt_chunks=n_prompt_chunks),
        grid=(nch,),
        in_specs=[pl.BlockSpec((CHUNK, C_CONV), lambda c: (c, COL_U)),
                  pl.BlockSpec((8, C_CONV), lambda c: (jnp.maximum(c * (CHUNK // 8) - 1, 0), COL_U)),
                  pl.BlockSpec((1, 8, C_CONV), init_map),
                  pl.BlockSpec((CHUNK, SLOT), lambda c: (c, COL_TAIL)),
                  pl.BlockSpec((CONV_W, C_CONV), lambda c: (0, 0)),
                  pl.BlockSpec((1, LANE), lambda c: (0, 0)),
                  pl.BlockSpec((1, LANE), lambda c: (0, 0))],
        out_specs=[head_spec(GDN_DV), head_spec(GDN_DK), head_spec(CHUNK), head_spec(GDN_DK),
                   head_spec(GDN_DK), pl.BlockSpec((1, 8, LANE), lambda c: (c, 0, 0))],
        out_shape=[jax.ShapeDtypeStruct((h, m, GDN_DV), F32),
                   jax.ShapeDtypeStruct((h, m, GDN_DK), BF16),
                   jax.ShapeDtypeStruct((h, m, CHUNK), BF16),
                   jax.ShapeDtypeStruct((h, m, GDN_DK), BF16),
                   jax.ShapeDtypeStruct((h, m, GDN_DK), BF16),
                   jax.ShapeDtypeStruct((nch, 8, LANE), F32)],
        scratch_shapes=[pltpu.VMEM((8 + CHUNK, C_CONV), F32)],
        compiler_params=_params("parallel"),
        name="gdn_pre",
    )(cols, cols, conv_init, cols, conv_w, aneg, dtb)


def _gdn_rec_kernel(uv_ref, wk_ref, attn_ref, qg_ref, kt_ref, gl_ref, s0_ref, z_ref, gn_ref,
                    y_ref, s_ref):
    @pl.when(pl.program_id(1) == 0)
    def _():
        s_ref[...] = s0_ref[...]

    z = z_ref[...]
    gn = gn_ref[...]
    for h in range(GDN_HEADS):
        s = s_ref[0, h]
        sb = s.astype(BF16)
        u = uv_ref[h] - _dot(wk_ref[h], sb)
        ub = u.astype(BF16)
        o = _dot(qg_ref[h], sb) + _dot(attn_ref[h], ub)
        s_ref[0, h] = gl_ref[0, h:h + 1, :] * s + _dot_tn(kt_ref[h], ub)
        zh = z[:, h * GDN_DV:(h + 1) * GDN_DV]
        y_ref[:, h * GDN_DV:(h + 1) * GDN_DV] = (_rms(o, gn) * (zh * _sigmoid(zh))).astype(y_ref.dtype)


def gdn_recurrence(pre, cols, s0, gdn_norm, batch, nchunks, tok_block0):
    uv, wk, attn, qg, kt, gl = pre
    h = GDN_HEADS
    tok = lambda b, n: tok_block0 + b * nchunks + n
    head_spec = lambda d: pl.BlockSpec((h, CHUNK, d), lambda b, n: (0, tok(b, n), 0))
    state_spec = pl.BlockSpec((1, h, GDN_DK, GDN_DV), lambda b, n: (b, 0, 0, 0))
    return pl.pallas_call(
        _gdn_rec_kernel,
        grid=(batch, nchunks),
        in_specs=[head_spec(GDN_DV), head_spec(GDN_DK), head_spec(CHUNK), head_spec(GDN_DK),
                  head_spec(GDN_DK),
                  pl.BlockSpec((1, 8, LANE), lambda b, n: (tok(b, n), 0, 0)),
                  state_spec,
                  pl.BlockSpec((CHUNK, SLOT), lambda b, n: (tok(b, n), COL_Z)),
                  pl.BlockSpec((1, GDN_DV), lambda b, n: (0, 0))],
        out_specs=[pl.BlockSpec((CHUNK, GDN_V), lambda b, n: (b * nchunks + n, 0)), state_spec],
        out_shape=[jax.ShapeDtypeStruct((batch * nchunks * CHUNK, GDN_V), BF16),
                   jax.ShapeDtypeStruct((batch, h, GDN_DK, GDN_DV), F32)],
        compiler_params=_params("parallel", "arbitrary"),
        name="gdn_recurrence",
    )(uv, wk, attn, qg, kt, gl, s0, cols, gdn_norm.reshape(1, -1))


def _pad_cols(w, width):
    return jnp.pad(w, ((0, 0), (0, width - w.shape[1])))


def _layout_w_in(w):
    a0, b0 = 0, Q_LORA + KV_LORA + QK_ROPE
    c0 = b0 + 3 * BAND_HEADS * BAND_DIM
    q_a = w[:, a0:a0 + Q_LORA]
    c_kv = w[:, a0 + Q_LORA:a0 + Q_LORA + KV_LORA]
    k_r = w[:, a0 + Q_LORA + KV_LORA:b0]
    band = w[:, b0:c0]
    u = w[:, c0:c0 + C_CONV]
    z = w[:, c0 + C_CONV:c0 + C_CONV + GDN_V]
    a = w[:, c0 + C_CONV + GDN_V:c0 + C_CONV + GDN_V + GDN_HEADS]
    b = w[:, c0 + C_CONV + GDN_V + GDN_HEADS:]
    half = QK_ROPE // 2
    k_sw = jnp.concatenate([k_r[:, half:], k_r[:, :half]], axis=1)
    tail = jnp.concatenate([c_kv, _pad_cols(k_r, LANE), _pad_cols(k_sw, LANE),
                            _pad_cols(a, LANE), _pad_cols(b, LANE)], axis=1)
    return jnp.concatenate([u, q_a, band, z, tail], axis=1).astype(BF16)


def _layout_w_qb(w):
    k = w.shape[0]
    w = w.reshape(k, MLA_HEADS, QK_NOPE + QK_ROPE)
    nope = w[:, :, :QK_NOPE]
    rope = w[:, :, QK_NOPE:]
    half = QK_ROPE // 2
    rope_sw = jnp.concatenate([rope[:, :, half:], rope[:, :, :half]], axis=2)
    pad = lambda t: jnp.pad(t, ((0, 0), (0, 0), (0, LANE - QK_ROPE)))
    out = jnp.concatenate([nope, pad(rope), pad(rope_sw)], axis=1)
    return out.reshape(k, 3 * MLA_HEADS * LANE).astype(BF16)


def _rope_tables(pos):
    inv = 1.0 / (ROPE_THETA ** (jnp.arange(0, QK_ROPE, 2, dtype=F32) / QK_ROPE))
    ang = pos.astype(F32)[:, None] * inv[None, :]
    cos, sin = jnp.cos(ang), jnp.sin(ang)
    zeros = jnp.zeros((pos.shape[0], LANE - QK_ROPE), F32)
    return (jnp.concatenate([cos, cos, zeros], axis=1), jnp.concatenate([-sin, sin, zeros], axis=1))


def _band_bias(rel_bias):
    q_pos = BAND_ROWS + jnp.arange(CHUNK, dtype=jnp.int32)
    k_pos = jnp.arange(BAND_KEYS, dtype=jnp.int32)
    rel = jnp.clip(q_pos[:, None] - k_pos[None, :], -REL_CLIP, REL_CLIP) + REL_CLIP
    return rel_bias[:, rel]


def kernel(x_prompt, x_sample, cache_latent, cache_k_rope, cache_band_k, cache_band_v, state_conv, state_delta, norm_ff1, ff1_w1, ff1_w3, ff1_w2, norm_mix, w_in, q_norm, w_qb, kv_norm, w_uk, w_uv, rel_bias, conv_w, a_log, dt_bias, gdn_norm, w_out, norm_ff2, ff2_w1, ff2_w3, ff2_w2, final_norm):
    batch, seq, d = x_prompt.shape
    dbatch, dseq, _ = x_sample.shape
    depth = w_in.shape[0]
    past = cache_latent.shape[2]
    assert dseq == CHUNK and seq % CHUNK == 0 and cache_band_k.shape[2] == BAND_ROWS
    mp, ms = batch * seq, dbatch * dseq
    m = mp + ms
    nchunks = seq // CHUNK
    blk_s = mp // CHUNK
    band_w = BAND_HEADS * BAND_DIM

    x = jnp.concatenate([x_prompt.reshape(mp, d), x_sample.reshape(ms, d)], axis=0)
    pos = jnp.concatenate([jnp.tile(jnp.arange(seq, dtype=jnp.int32), batch),
                           jnp.tile(past + jnp.arange(dseq, dtype=jnp.int32), dbatch)])
    cos128, sin128 = _rope_tables(pos)
    zero_state = jnp.zeros((batch, GDN_HEADS, GDN_DK, GDN_DV), F32)

    def ffn(x, norm, w1, w3, w2):
        h = rmsnorm(x, norm, BF16)
        hid = matmul_swiglu(h, w1.astype(BF16), w3.astype(BF16), 544, 1024)
        return matmul_residual(hid, w2.astype(BF16), x, 0.5, 272, 1024)

    outs = [[] for _ in range(12)]
    for l in range(depth):
        x = ffn(x, norm_ff1[l], ff1_w1[l], ff1_w3[l], ff1_w2[l])

        h = rmsnorm(x, norm_mix[l], BF16)
        cols = matmul(h, _layout_w_in(w_in[l]), F32, 544, 1024)

        wuk_t = jnp.transpose(w_uk[l], (1, 2, 0)).astype(BF16)
        wuv_t = jnp.transpose(w_uv[l], (1, 0, 2)).astype(BF16)
        q_lat, q_rope, lat, krope, klat_bf, krope_bf = mla_prep(
            cols, q_norm[l], kv_norm[l], _layout_w_qb(w_qb[l]), wuk_t, cos128, sin128)
        ya_p = mla_attention_prompt(q_lat, q_rope, klat_bf, krope_bf, wuv_t, batch, seq)
        ya_s = mla_attention_sample(q_lat, q_rope, cache_latent[l], cache_k_rope[l], klat_bf,
                                    krope_bf, wuv_t, dbatch, blk_s)

        bias = _band_bias(rel_bias[l])
        band_k = cols[:, COL_BK * SLOT:(COL_BK + 1) * SLOT]
        band_v = cols[:, COL_BV * SLOT:(COL_BV + 1) * SLOT]

        def prompt_rows(t):
            t = t[:mp].reshape(batch, seq, band_w).astype(BF16)
            return jnp.pad(t, ((0, 0), (BAND_ROWS, 0), (0, 0)))

        def sample_rows(t, cache):
            return jnp.concatenate([cache.reshape(dbatch, BAND_ROWS, band_w).astype(BF16),
                                    t[mp:].reshape(dbatch, dseq, band_w).astype(BF16)], axis=1)

        yb_p = band_attention(cols, prompt_rows(band_k), prompt_rows(band_v), bias,
                              batch, nchunks, 0, BAND_ROWS)
        yb_s = band_attention(cols, sample_rows(band_k, cache_band_k[l]),
                              sample_rows(band_v, cache_band_v[l]), bias, dbatch, 1, blk_s, 0)

        conv_init = jnp.concatenate(
            [jnp.zeros((batch, 8, C_CONV), F32),
             jnp.pad(state_conv[l], ((0, 0), (8 - (CONV_W - 1), 0), (0, 0)))], axis=0)
        aneg = _pad_cols(-jnp.exp(a_log[l].astype(F32)).reshape(1, -1), LANE)
        dtb = _pad_cols(dt_bias[l].astype(F32).reshape(1, -1), LANE)
        pre = gdn_pre(cols, conv_init, conv_w[l], aneg, dtb, nchunks, mp // CHUNK)
        yc_p, delta_p = gdn_recurrence(pre, cols, zero_state, gdn_norm[l], batch, nchunks, 0)
        yc_s, delta_s = gdn_recurrence(pre, cols, state_delta[l], gdn_norm[l], dbatch, 1, blk_s)

        mix = jnp.concatenate([jnp.concatenate([ya_p, yb_p, yc_p], axis=1),
                               jnp.concatenate([ya_s, yb_s, yc_s], axis=1)], axis=0)
        x = matmul_residual(mix, w_out[l].astype(BF16), x, 1.0, 544, 1024)
        x = ffn(x, norm_ff2[l], ff2_w1[l], ff2_w3[l], ff2_w2[l])

        u_cols = cols[:, :C_CONV]
        keep = min(BAND_ROWS, seq)
        layer_out = (
            lat[:mp].reshape(batch, seq, KV_LORA),
            krope[:mp].reshape(batch, seq, QK_ROPE),
            band_k[:mp].reshape(batch, seq, BAND_HEADS, BAND_DIM)[:, seq - keep:],
            band_v[:mp].reshape(batch, seq, BAND_HEADS, BAND_DIM)[:, seq - keep:],
            u_cols[:mp].reshape(batch, seq, C_CONV)[:, seq - (CONV_W - 1):],
            delta_p,
            lat[mp:].reshape(dbatch, dseq, KV_LORA),
            krope[mp:].reshape(dbatch, dseq, QK_ROPE),
            band_k[mp:].reshape(dbatch, dseq, BAND_HEADS, BAND_DIM),
            band_v[mp:].reshape(dbatch, dseq, BAND_HEADS, BAND_DIM),
            u_cols[mp:].reshape(dbatch, dseq, C_CONV)[:, dseq - (CONV_W - 1):],
            delta_s,
        )
        for acc, t in zip(outs, layer_out):
            acc.append(t)

    y_prompt = rmsnorm(x, final_norm, F32, rows=mp).reshape(batch, seq, d)
    y_sample = rmsnorm(x, final_norm, F32, rows=ms, row_start=mp).reshape(dbatch, dseq, d)
    return (y_prompt, y_sample) + tuple(jnp.stack(t) for t in outs)
```

```python
import functools
import math

import jax
import jax.numpy as jnp
from jax import lax
from jax.experimental import pallas as pl
from jax.experimental.pallas import tpu as pltpu

F32 = jnp.float32
BF16 = jnp.bfloat16
HIGHEST = lax.Precision.HIGHEST

CHUNK = 64
CHUNK_SHIFT = 6
EPS = 1e-6
NEG = -1e30
Q_LORA = 1024
KV_LORA = 512
QK_NOPE = 128
QK_ROPE = 64
V_HEAD = 128
MLA_HEADS = 16
ROPE_THETA = 10000.0
MLA_SCALE = (QK_NOPE + QK_ROPE) ** -0.5
BAND_HEADS = 8
BAND_DIM = 128
BAND_PREV = 8
BAND_ROWS = BAND_PREV * CHUNK
BAND_KEYS = BAND_ROWS + CHUNK
REL_CLIP = 128
BAND_SCALE = BAND_DIM ** -0.5
BIAS_SPAN = 2 * BAND_KEYS
GDN_HEADS = 8
GDN_GROUP = 4
GDN_DK = 128
GDN_DV = 128
CONV_W = 4
GDN_QK = GDN_HEADS * GDN_DK
GDN_V = GDN_HEADS * GDN_DV
C_CONV = 2 * GDN_QK + GDN_V
LANE = 128
SLOT = 1024
COL_U, COL_QA, COL_BQ, COL_BK, COL_BV, COL_Z, COL_TAIL = 0, 3, 4, 5, 6, 7, 8
IN_COLS_PAD = 9 * SLOT
TAIL_KR, TAIL_KSW, TAIL_A, TAIL_B = 512, 640, 768, 896

VMEM_LIMIT_BYTES = 56 * 1024 * 1024


def _params(*sem):
    return pltpu.CompilerParams(dimension_semantics=sem, vmem_limit_bytes=VMEM_LIMIT_BYTES)


def _tile(n, target, mult=16):
    best = None
    for t in range(mult, min(n, target) + 1, mult):
        if n % t == 0:
            best = t
    assert best is not None, (n, target)
    return best


def _dot(a, b, precision=None):
    return jnp.dot(a, b, preferred_element_type=F32, precision=precision)


def _dot_nt(a, b, precision=None):
    return lax.dot_general(a, b, (((1,), (1,)), ((), ())), preferred_element_type=F32,
                           precision=precision)


def _dot_tn(a, b, precision=None):
    return lax.dot_general(a, b, (((0,), (0,)), ((), ())), preferred_element_type=F32,
                           precision=precision)


def _sigmoid(x):
    return 1.0 / (1.0 + jnp.exp(-x))


def _rms(x, g):
    return x * lax.rsqrt(jnp.mean(x * x, axis=-1, keepdims=True) + EPS) * g


def _rmsnorm_kernel(x_ref, g_ref, o_ref):
    o_ref[...] = _rms(x_ref[...], g_ref[...]).astype(o_ref.dtype)


def rmsnorm(x, g, out_dtype, rows=None, row_start=0):
    m, d = x.shape
    rows = m if rows is None else rows
    tm = _tile(math.gcd(rows, row_start) if row_start else rows, 256, 8)
    off = row_start // tm
    return pl.pallas_call(
        _rmsnorm_kernel,
        grid=(rows // tm,),
        in_specs=[pl.BlockSpec((tm, d), lambda i: (i + off, 0)),
                  pl.BlockSpec((1, d), lambda i: (0, 0))],
        out_specs=pl.BlockSpec((tm, d), lambda i: (i, 0)),
        out_shape=jax.ShapeDtypeStruct((rows, d), out_dtype),
        compiler_params=_params("parallel"),
        name="rmsnorm",
    )(x, g.reshape(1, d))


def _mm_kernel(a_ref, w_ref, o_ref):
    o_ref[...] = _dot(a_ref[...], w_ref[...]).astype(o_ref.dtype)


def _cast_weight(w_ref, wb_ref):
    @pl.when(pl.program_id(1) == 0)
    def _():
        wb_ref[...] = w_ref[...].astype(BF16)


def _mm_res_kernel(a_ref, w_ref, r_ref, o_ref, wb_ref, *, scale):
    _cast_weight(w_ref, wb_ref)
    o_ref[...] = r_ref[...] + scale * _dot(a_ref[...], wb_ref[...])


def _mm_mix_kernel(ya_ref, yb_ref, yc_ref, w_ref, r_ref, o_ref, wb_ref):
    _cast_weight(w_ref, wb_ref)
    ka, kb = ya_ref.shape[1], yb_ref.shape[1]
    o_ref[...] = (r_ref[...] + _dot(ya_ref[...], wb_ref[0:ka, :])
                  + _dot(yb_ref[...], wb_ref[ka:ka + kb, :]) + _dot(yc_ref[...], wb_ref[ka + kb:, :]))


def _mm_swiglu_kernel(a_ref, w1_ref, w3_ref, o_ref, w1b_ref, w3b_ref):
    _cast_weight(w1_ref, w1b_ref)
    _cast_weight(w3_ref, w3b_ref)
    a = a_ref[...]
    h1 = _dot(a, w1b_ref[...])
    h3 = _dot(a, w3b_ref[...])
    o_ref[...] = (h1 * _sigmoid(h1) * h3).astype(o_ref.dtype)


def matmul(a, w, out_dtype, tm_target, tn_target):
    m, k = a.shape
    n = w.shape[1]
    tm, tn = _tile(m, tm_target), _tile(n, tn_target, LANE)
    return pl.pallas_call(
        _mm_kernel,
        grid=(n // tn, m // tm),
        in_specs=[pl.BlockSpec((tm, k), lambda j, i: (i, 0)),
                  pl.BlockSpec((k, tn), lambda j, i: (0, j))],
        out_specs=pl.BlockSpec((tm, tn), lambda j, i: (i, j)),
        out_shape=jax.ShapeDtypeStruct((m, n), out_dtype),
        compiler_params=_params("parallel", "parallel"),
        name="matmul",
    )(a, w)


def _layer_weight_spec(layer, k, tn):
    return pl.BlockSpec((None, k, tn), lambda j, i: (layer, 0, j))


def matmul_residual(a, w, layer, res, scale, tm_target, tn_target):
    m, k = a.shape
    n = w.shape[2]
    tm, tn = _tile(m, tm_target), _tile(n, tn_target, LANE)
    return pl.pallas_call(
        functools.partial(_mm_res_kernel, scale=scale),
        grid=(n // tn, m // tm),
        in_specs=[pl.BlockSpec((tm, k), lambda j, i: (i, 0)),
                  _layer_weight_spec(layer, k, tn),
                  pl.BlockSpec((tm, tn), lambda j, i: (i, j))],
        out_specs=pl.BlockSpec((tm, tn), lambda j, i: (i, j)),
        out_shape=jax.ShapeDtypeStruct((m, n), F32),
        scratch_shapes=[pltpu.VMEM((k, tn), BF16)],
        compiler_params=_params("parallel", "arbitrary"),
        name="matmul_residual",
    )(a, w, res)


def matmul_mix(ya, yb, yc, w, layer, res, tm_target, tn_target):
    m = ya.shape[0]
    k, n = w.shape[1], w.shape[2]
    tm, tn = _tile(m, tm_target), _tile(n, tn_target, LANE)
    row = lambda y: pl.BlockSpec((tm, y.shape[1]), lambda j, i: (i, 0))
    return pl.pallas_call(
        _mm_mix_kernel,
        grid=(n // tn, m // tm),
        in_specs=[row(ya), row(yb), row(yc), _layer_weight_spec(layer, k, tn),
                  pl.BlockSpec((tm, tn), lambda j, i: (i, j))],
        out_specs=pl.BlockSpec((tm, tn), lambda j, i: (i, j)),
        out_shape=jax.ShapeDtypeStruct((m, n), F32),
        scratch_shapes=[pltpu.VMEM((k, tn), BF16)],
        compiler_params=_params("parallel", "arbitrary"),
        name="matmul_mix",
    )(ya, yb, yc, w, res)


def matmul_swiglu(a, w1, w3, layer, tm_target, tn_target):
    m, k = a.shape
    n = w1.shape[2]
    tm, tn = _tile(m, tm_target), _tile(n, tn_target, LANE)
    return pl.pallas_call(
        _mm_swiglu_kernel,
        grid=(n // tn, m // tm),
        in_specs=[pl.BlockSpec((tm, k), lambda j, i: (i, 0)),
                  _layer_weight_spec(layer, k, tn),
                  _layer_weight_spec(layer, k, tn)],
        out_specs=pl.BlockSpec((tm, tn), lambda j, i: (i, j)),
        out_shape=jax.ShapeDtypeStruct((m, n), BF16),
        scratch_shapes=[pltpu.VMEM((k, tn), BF16), pltpu.VMEM((k, tn), BF16)],
        compiler_params=_params("parallel", "arbitrary"),
        name="matmul_swiglu",
    )(a, w1, w3)


def _mla_prep_kernel(qa_ref, tail_ref, qn_ref, kvn_ref, wq_ref, wuk_ref, cos_ref, sin_ref,
                     qlat_ref, qrope_ref, lat_ref, krope_ref, klat_bf_ref, krope_bf_ref):
    hq = _rms(qa_ref[...], qn_ref[...]).astype(BF16)
    q_all = _dot(hq, wq_ref[...])
    cos = cos_ref[...]
    sin = sin_ref[...]
    hw = MLA_HEADS * LANE
    for h in range(MLA_HEADS):
        qn = q_all[:, h * LANE:(h + 1) * LANE].astype(BF16)
        qlat_ref[h] = _dot(qn, wuk_ref[h]).astype(BF16)
        qr = (q_all[:, hw + h * LANE:hw + (h + 1) * LANE] * cos
              + q_all[:, 2 * hw + h * LANE:2 * hw + (h + 1) * LANE] * sin)
        qrope_ref[h] = qr[:, :QK_ROPE].astype(BF16)
    tail = tail_ref[...]
    lat = _rms(tail[:, :KV_LORA], kvn_ref[...])
    lat_ref[...] = lat
    klat_bf_ref[...] = lat.astype(BF16)
    kr = tail[:, TAIL_KR:TAIL_KR + LANE] * cos + tail[:, TAIL_KSW:TAIL_KSW + LANE] * sin
    krope_ref[...] = kr[:, :QK_ROPE]
    krope_bf_ref[...] = kr[:, :QK_ROPE].astype(BF16)


def mla_prep(cols, q_norm, kv_norm, wq_all, wuk_t, cos128, sin128):
    m = cols.shape[0]
    tm = _tile(m, 256)
    h = MLA_HEADS
    const2 = lambda i: (0, 0)
    return pl.pallas_call(
        _mla_prep_kernel,
        grid=(m // tm,),
        in_specs=[pl.BlockSpec((tm, SLOT), lambda i: (i, COL_QA)),
                  pl.BlockSpec((tm, SLOT), lambda i: (i, COL_TAIL)),
                  pl.BlockSpec((1, Q_LORA), const2),
                  pl.BlockSpec((1, KV_LORA), const2),
                  pl.BlockSpec(wq_all.shape, const2, pipeline_mode=pl.Buffered(1)),
                  pl.BlockSpec(wuk_t.shape, lambda i: (0, 0, 0), pipeline_mode=pl.Buffered(1)),
                  pl.BlockSpec((tm, LANE), lambda i: (i, 0)),
                  pl.BlockSpec((tm, LANE), lambda i: (i, 0))],
        out_specs=[pl.BlockSpec((h, tm, KV_LORA), lambda i: (0, i, 0)),
                   pl.BlockSpec((h, tm, QK_ROPE), lambda i: (0, i, 0)),
                   pl.BlockSpec((tm, KV_LORA), lambda i: (i, 0)),
                   pl.BlockSpec((tm, QK_ROPE), lambda i: (i, 0)),
                   pl.BlockSpec((tm, KV_LORA), lambda i: (i, 0)),
                   pl.BlockSpec((tm, QK_ROPE), lambda i: (i, 0))],
        out_shape=[jax.ShapeDtypeStruct((h, m, KV_LORA), BF16),
                   jax.ShapeDtypeStruct((h, m, QK_ROPE), BF16),
                   jax.ShapeDtypeStruct((m, KV_LORA), F32),
                   jax.ShapeDtypeStruct((m, QK_ROPE), F32),
                   jax.ShapeDtypeStruct((m, KV_LORA), BF16),
                   jax.ShapeDtypeStruct((m, QK_ROPE), BF16)],
        compiler_params=_params("parallel"),
        name="mla_prep",
    )(cols, cols, q_norm.reshape(1, -1), kv_norm.reshape(1, -1), wq_all, wuk_t, cos128, sin128)


def _softmax_init(m_ref, l_ref, acc_ref):
    m_ref[...] = jnp.full(m_ref.shape, NEG, F32)
    l_ref[...] = jnp.zeros(l_ref.shape, F32)
    acc_ref[...] = jnp.zeros(acc_ref.shape, F32)


def _softmax_update(s, v, m_ref, l_ref, acc_ref):
    m_old = m_ref[...]
    m_new = jnp.maximum(m_old, jnp.max(s, axis=-1, keepdims=True))
    alpha = jnp.exp(m_old - m_new)
    p = jnp.exp(s - m_new)
    l_ref[...] = alpha * l_ref[...] + jnp.sum(p, axis=-1, keepdims=True)
    acc_ref[...] = alpha * acc_ref[...] + _dot(p.astype(BF16), v)
    m_ref[...] = m_new


def _mla_finish(wuv_ref, o_ref, l_ref, acc_ref, tq):
    ctx = acc_ref[...] / l_ref[...]
    for h in range(MLA_HEADS):
        ch = ctx[h * tq:(h + 1) * tq].astype(BF16)
        o_ref[:, h * V_HEAD:(h + 1) * V_HEAD] = _dot(ch, wuv_ref[h]).astype(o_ref.dtype)


def _mla_prompt_kernel(ql_ref, qr_ref, kl_ref, kr_ref, wuv_ref, o_ref, m_ref, l_ref, acc_ref,
                       *, tq, tk):
    qi = pl.program_id(1)
    kj = pl.program_id(2)
    last = (qi * tq + tq - 1) // tk

    @pl.when(kj == 0)
    def _():
        _softmax_init(m_ref, l_ref, acc_ref)

    def scores():
        q_l = ql_ref[...].reshape(MLA_HEADS * tq, KV_LORA)
        q_r = qr_ref[...].reshape(MLA_HEADS * tq, QK_ROPE)
        return (_dot_nt(q_l, kl_ref[...]) + _dot_nt(q_r, kr_ref[...])) * MLA_SCALE

    @pl.when(kj < last)
    def _():
        _softmax_update(scores(), kl_ref[...], m_ref, l_ref, acc_ref)

    @pl.when(kj == last)
    def _():
        s = scores()
        row = lax.broadcasted_iota(jnp.int32, s.shape, 0)
        col = lax.broadcasted_iota(jnp.int32, s.shape, 1)
        q_chunk = (qi * tq + row % tq) // CHUNK
        k_chunk = (kj * tk + col) // CHUNK
        s = jnp.where(k_chunk <= q_chunk, s, NEG)
        _softmax_update(s, kl_ref[...], m_ref, l_ref, acc_ref)

    @pl.when(kj == pl.num_programs(2) - 1)
    def _():
        _mla_finish(wuv_ref, o_ref, l_ref, acc_ref, tq)


def mla_attention_prompt(q_lat, q_rope, k_lat, k_rope, wuv, batch, seq):
    tq = 128
    tk = _tile(seq, 512, CHUNK)
    nq, nk = seq // tq, seq // tk
    h = MLA_HEADS
    rows = h * tq

    def k_map(b, qi, kj):
        return (b * nk + jnp.minimum(kj, (qi * tq + tq - 1) // tk), 0)

    return pl.pallas_call(
        functools.partial(_mla_prompt_kernel, tq=tq, tk=tk),
        grid=(batch, nq, nk),
        in_specs=[pl.BlockSpec((h, tq, KV_LORA), lambda b, qi, kj: (0, b * nq + qi, 0)),
                  pl.BlockSpec((h, tq, QK_ROPE), lambda b, qi, kj: (0, b * nq + qi, 0)),
                  pl.BlockSpec((tk, KV_LORA), k_map),
                  pl.BlockSpec((tk, QK_ROPE), k_map),
                  pl.BlockSpec(wuv.shape, lambda b, qi, kj: (0, 0, 0))],
        out_specs=pl.BlockSpec((tq, h * V_HEAD), lambda b, qi, kj: (b * nq + qi, 0)),
        out_shape=jax.ShapeDtypeStruct((batch * seq, h * V_HEAD), BF16),
        scratch_shapes=[pltpu.VMEM((rows, 1), F32), pltpu.VMEM((rows, 1), F32),
                        pltpu.VMEM((rows, KV_LORA), F32)],
        compiler_params=_params("parallel", "parallel", "arbitrary"),
        name="mla_attention_prompt",
    )(q_lat, q_rope, k_lat, k_rope, wuv)


def _mla_sample_kernel(ql_ref, qr_ref, pl_ref, pr_ref, kl_ref, kr_ref, wuv_ref, o_ref,
                       m_ref, l_ref, acc_ref):
    kj = pl.program_id(1)
    n_past = pl.num_programs(1) - 1
    q_l = ql_ref[...].reshape(MLA_HEADS * CHUNK, KV_LORA)
    q_r = qr_ref[...].reshape(MLA_HEADS * CHUNK, QK_ROPE)

    @pl.when(kj == 0)
    def _():
        _softmax_init(m_ref, l_ref, acc_ref)

    @pl.when(kj < n_past)
    def _():
        k_l = pl_ref[0].astype(BF16)
        k_r = pr_ref[0].astype(BF16)
        s = (_dot_nt(q_l, k_l) + _dot_nt(q_r, k_r)) * MLA_SCALE
        _softmax_update(s, k_l, m_ref, l_ref, acc_ref)

    @pl.when(kj == n_past)
    def _():
        k_l = kl_ref[...]
        s = (_dot_nt(q_l, k_l) + _dot_nt(q_r, kr_ref[...])) * MLA_SCALE
        _softmax_update(s, k_l, m_ref, l_ref, acc_ref)
        _mla_finish(wuv_ref, o_ref, l_ref, acc_ref, CHUNK)


def mla_attention_sample(q_lat, q_rope, past_lat, past_rope, k_lat, k_rope, wuv, batch, tok_block0):
    past = past_lat.shape[1]
    tk = _tile(past, 512, CHUNK)
    n_past = past // tk
    h = MLA_HEADS
    rows = h * CHUNK
    return pl.pallas_call(
        _mla_sample_kernel,
        grid=(batch, n_past + 1),
        in_specs=[pl.BlockSpec((h, CHUNK, KV_LORA), lambda b, kj: (0, tok_block0 + b, 0)),
                  pl.BlockSpec((h, CHUNK, QK_ROPE), lambda b, kj: (0, tok_block0 + b, 0)),
                  pl.BlockSpec((1, tk, KV_LORA), lambda b, kj: (b, jnp.minimum(kj, n_past - 1), 0)),
                  pl.BlockSpec((1, tk, QK_ROPE), lambda b, kj: (b, jnp.minimum(kj, n_past - 1), 0)),
                  pl.BlockSpec((CHUNK, KV_LORA), lambda b, kj: (tok_block0 + b, 0)),
                  pl.BlockSpec((CHUNK, QK_ROPE), lambda b, kj: (tok_block0 + b, 0)),
                  pl.BlockSpec(wuv.shape, lambda b, kj: (0, 0, 0))],
        out_specs=pl.BlockSpec((CHUNK, h * V_HEAD), lambda b, kj: (b, 0)),
        out_shape=jax.ShapeDtypeStruct((batch * CHUNK, h * V_HEAD), BF16),
        scratch_shapes=[pltpu.VMEM((rows, 1), F32), pltpu.VMEM((rows, 1), F32),
                        pltpu.VMEM((rows, KV_LORA), F32)],
        compiler_params=_params("parallel", "arbitrary"),
        name="mla_attention_sample",
    )(q_lat, q_rope, past_lat, past_rope, k_lat, k_rope, wuv)


def _band_kernel(q_ref, k_ref, v_ref, g_ref, o_ref, bias_ref, *, pad_rows):
    n = pl.program_id(1)

    @pl.when(jnp.logical_and(pl.program_id(0) == 0, n == 0))
    def _():
        for h in range(BAND_HEADS):
            rows = jnp.broadcast_to(g_ref[h:h + 1, :], (CHUNK, BIAS_SPAN))
            rolled = pltpu.roll(rows, BIAS_SPAN - (CHUNK - 1), 1, stride=1, stride_axis=0)
            bias_ref[h] = rolled[:, :BAND_KEYS]

    start = pl.multiple_of(n * CHUNK, CHUNK)
    q = q_ref[...]
    for h in range(BAND_HEADS):
        lanes = slice(h * BAND_DIM, (h + 1) * BAND_DIM)
        qh = q[:, lanes].astype(BF16)
        kh = k_ref[0, pl.ds(start, BAND_KEYS), lanes]
        vh = v_ref[0, pl.ds(start, BAND_KEYS), lanes]
        s = _dot_nt(qh, kh) * BAND_SCALE + bias_ref[h]
        if pad_rows:
            col = lax.broadcasted_iota(jnp.int32, s.shape, 1)
            s = jnp.where(start + col >= pad_rows, s, NEG)
        p = jnp.exp(s - jnp.max(s, axis=-1, keepdims=True))
        o = _dot(p.astype(BF16), vh) / jnp.sum(p, axis=-1, keepdims=True)
        o_ref[:, lanes] = o.astype(o_ref.dtype)


def band_attention(cols, k_rows, v_rows, bias_row, batch, nchunks, tok_block0, pad_rows):
    width = BAND_HEADS * BAND_DIM
    rows = k_rows.shape[1]
    return pl.pallas_call(
        functools.partial(_band_kernel, pad_rows=pad_rows),
        grid=(batch, nchunks),
        in_specs=[pl.BlockSpec((CHUNK, width), lambda b, n: (tok_block0 + b * nchunks + n, COL_BQ)),
                  pl.BlockSpec((1, rows, width), lambda b, n: (b, 0, 0)),
                  pl.BlockSpec((1, rows, width), lambda b, n: (b, 0, 0)),
                  pl.BlockSpec(bias_row.shape, lambda b, n: (0, 0))],
        out_specs=pl.BlockSpec((CHUNK, width), lambda b, n: (b * nchunks + n, 0)),
        out_shape=jax.ShapeDtypeStruct((batch * nchunks * CHUNK, width), BF16),
        scratch_shapes=[pltpu.VMEM((BAND_HEADS, CHUNK, BAND_KEYS), F32)],
        compiler_params=_params("arbitrary", "arbitrary"),
        name="band_attention",
    )(cols, k_rows, v_rows, bias_row)


def _expand_matrix(rows, first, width, per_shift):
    r = lax.broadcasted_iota(jnp.int32, (rows, width), 0)
    c = lax.broadcasted_iota(jnp.int32, (rows, width), 1)
    return (r == first + (c >> per_shift)).astype(F32)


def _block_diag(w, group):
    gl = w.shape[1]
    r = lax.broadcasted_iota(jnp.int32, (gl, gl), 0)
    c = lax.broadcasted_iota(jnp.int32, (gl, gl), 1)
    return jnp.where((r >> CHUNK_SHIFT) == (c >> CHUNK_SHIFT), jnp.concatenate([w] * group, axis=0), 0.0)


def _unit_lower_inverse(a, row, col, group):
    t = (row == col).astype(F32) - jnp.where((row >> 1) == (col >> 1), a, 0.0)
    for shift in range(1, CHUNK_SHIFT):
        lower_left = ((row >> (shift + 1)) == (col >> (shift + 1))) & ((row >> shift) != (col >> shift))
        off = jnp.where(lower_left, a, 0.0)
        x = _dot(t, _block_diag(off, group), HIGHEST)
        t = t - _dot(x, _block_diag(t, group), HIGHEST)
    return t


def _l2norm_heads(x, heads, width):
    parts = []
    for h in range(heads):
        xh = x[:, h * width:(h + 1) * width]
        parts.append(xh * lax.rsqrt(jnp.sum(xh * xh, axis=-1, keepdims=True) + EPS))
    return jnp.concatenate(parts, axis=1)


def _gdn_pre_kernel(u_ref, uprev_ref, init_ref, tail_ref, cw_ref, aneg_ref, dtb_ref,
                    uv_ref, wk_ref, attn_ref, qg_ref, kt_ref, gl_ref, ext_ref,
                    *, chunks_per_seq, n_prompt_chunks):
    group = GDN_GROUP
    gw = group * CHUNK
    c = pl.program_id(0)
    is_start = jnp.logical_or(c >= n_prompt_chunks, c % chunks_per_seq == 0)
    ext_ref[0:8, :] = jnp.where(is_start, init_ref[0], uprev_ref[...])
    ext_ref[8:8 + CHUNK, :] = u_ref[...]
    cw = cw_ref[...]
    uc = (cw[3:4] * ext_ref[8:8 + CHUNK, :] + cw[2:3] * ext_ref[7:7 + CHUNK, :]
          + cw[1:2] * ext_ref[6:6 + CHUNK, :] + cw[0:1] * ext_ref[5:5 + CHUNK, :])
    uc = uc * _sigmoid(uc)

    tail = tail_ref[...]
    xa = tail[:, TAIL_A:TAIL_A + LANE] + dtb_ref[...]
    softplus = jnp.maximum(xa, 0.0) + jnp.log(1.0 + jnp.exp(-jnp.abs(xa)))
    g = aneg_ref[...] * softplus
    beta = _sigmoid(tail[:, TAIL_B:TAIL_B + LANE])

    r64 = lax.broadcasted_iota(jnp.int32, (CHUNK, CHUNK), 0)
    c64 = lax.broadcasted_iota(jnp.int32, (CHUNK, CHUNK), 1)
    gc = _dot((r64 >= c64).astype(F32), g, HIGHEST)
    egc = jnp.exp(gc)
    ekt = jnp.exp(gc[CHUNK - 1:CHUNK, :] - gc)
    e128 = _expand_matrix(LANE, 0, GDN_HEADS * GDN_DK, 7)
    beta_x = _dot(beta, e128, HIGHEST)
    egc_x = _dot(egc, e128, HIGHEST)
    ekt_x = _dot(ekt, e128, HIGHEST)
    gl_ref[0] = egc_x[CHUNK - 1:CHUNK, :]

    qn = _l2norm_heads(uc[:, :GDN_QK], GDN_HEADS, GDN_DK) * (GDN_DK ** -0.5)
    kn = _l2norm_heads(uc[:, GDN_QK:2 * GDN_QK], GDN_HEADS, GDN_DK)
    v_all = uc[:, 2 * GDN_QK:]
    qg_ref[...] = (qn * egc_x).astype(BF16)
    kt_ref[...] = (kn * ekt_x).astype(BF16)
    rhs_v = beta_x * v_all
    rhs_k = beta_x * egc_x * kn

    row = lax.broadcasted_iota(jnp.int32, (CHUNK, gw), 0)
    col = lax.broadcasted_iota(jnp.int32, (CHUNK, gw), 1) & (CHUNK - 1)
    incl = row >= col
    head_r = lax.broadcasted_iota(jnp.int32, (gw, group * GDN_DK), 0) >> CHUNK_SHIFT
    head_c = lax.broadcasted_iota(jnp.int32, (gw, group * GDN_DK), 1) >> 7
    for grp in range(GDN_HEADS // group):
        lanes = slice(grp * group * GDN_DK, (grp + 1) * group * GDN_DK)
        e64 = _expand_matrix(LANE, grp * group, gw, CHUNK_SHIFT)
        gcw = _dot(gc, e64, HIGHEST)
        bw = _dot(beta, e64, HIGHEST)
        gc_t = jnp.sum(jnp.where(row == col, gcw, 0.0), axis=0, keepdims=True)
        decay = jnp.where(incl, jnp.exp(jnp.where(incl, gcw - gc_t, 0.0)), 0.0)
        kg = kn[:, lanes]
        k_rows = jnp.where(head_r == head_c, jnp.concatenate([kg] * group, axis=0), 0.0).astype(BF16)
        kk = _dot_nt(kg.astype(BF16), k_rows)
        qk = _dot_nt(qn[:, lanes].astype(BF16), k_rows)
        a_mat = jnp.where(row > col, bw * kk * decay, 0.0)
        t_inv = _unit_lower_inverse(a_mat, row, col, group)
        attn_ref[:, grp * gw:(grp + 1) * gw] = (qk * decay).astype(BF16)
        for hh in range(group):
            h = grp * group + hh
            hl = slice(h * GDN_DK, (h + 1) * GDN_DK)
            rhs = jnp.concatenate([rhs_v[:, hl], rhs_k[:, hl]], axis=1)
            sol = _dot(t_inv[:, hh * CHUNK:(hh + 1) * CHUNK], rhs, HIGHEST)
            uv_ref[:, hl] = sol[:, :GDN_DV]
            wk_ref[:, hl] = sol[:, GDN_DV:].astype(BF16)


def gdn_pre(cols, conv_init, conv_w, aneg, dtb, chunks_per_seq, n_prompt_chunks):
    m = cols.shape[0]
    nch = m // CHUNK
    h = GDN_HEADS

    def init_map(c):
        return (jnp.where(c < n_prompt_chunks, c // chunks_per_seq,
                          n_prompt_chunks // chunks_per_seq + c - n_prompt_chunks), 0, 0)

    tok_spec = lambda d: pl.BlockSpec((CHUNK, d), lambda c: (c, 0))
    return pl.pallas_call(
        functools.partial(_gdn_pre_kernel, chunks_per_seq=chunks_per_seq,
                          n_prompt_chunks=n_prompt_chunks),
        grid=(nch,),
        in_specs=[pl.BlockSpec((CHUNK, C_CONV), lambda c: (c, COL_U)),
                  pl.BlockSpec((8, C_CONV), lambda c: (jnp.maximum(c * (CHUNK // 8) - 1, 0), COL_U)),
                  pl.BlockSpec((1, 8, C_CONV), init_map),
                  pl.BlockSpec((CHUNK, SLOT), lambda c: (c, COL_TAIL)),
                  pl.BlockSpec((CONV_W, C_CONV), lambda c: (0, 0)),
                  pl.BlockSpec((1, LANE), lambda c: (0, 0)),
                  pl.BlockSpec((1, LANE), lambda c: (0, 0))],
        out_specs=[tok_spec(GDN_V), tok_spec(GDN_QK), tok_spec(h * CHUNK), tok_spec(GDN_QK),
                   tok_spec(GDN_QK), pl.BlockSpec((1, 1, GDN_QK), lambda c: (c, 0, 0))],
        out_shape=[jax.ShapeDtypeStruct((m, GDN_V), F32),
                   jax.ShapeDtypeStruct((m, GDN_QK), BF16),
                   jax.ShapeDtypeStruct((m, h * CHUNK), BF16),
                   jax.ShapeDtypeStruct((m, GDN_QK), BF16),
                   jax.ShapeDtypeStruct((m, GDN_QK), BF16),
                   jax.ShapeDtypeStruct((nch, 1, GDN_QK), F32)],
        scratch_shapes=[pltpu.VMEM((8 + CHUNK, C_CONV), F32)],
        compiler_params=_params("parallel"),
        name="gdn_pre",
    )(cols, cols, conv_init, cols, conv_w, aneg, dtb)


def _gdn_rec_kernel(uv_ref, wk_ref, attn_ref, qg_ref, kt_ref, gl_ref, s0_ref, z_ref, gn_ref,
                    y_ref, s_ref):
    @pl.when(pl.program_id(1) == 0)
    def _():
        s_ref[...] = s0_ref[...]

    z = z_ref[...]
    gn = gn_ref[...]
    for h in range(GDN_HEADS):
        hl = slice(h * GDN_DK, (h + 1) * GDN_DK)
        s = s_ref[0, h]
        sb = s.astype(BF16)
        u = uv_ref[:, hl] - _dot(wk_ref[:, hl], sb)
        ub = u.astype(BF16)
        o = _dot(qg_ref[:, hl], sb) + _dot(attn_ref[:, h * CHUNK:(h + 1) * CHUNK], ub)
        s_ref[0, h] = gl_ref[0, :, hl] * s + _dot_tn(kt_ref[:, hl], ub)
        y_ref[:, hl] = (_rms(o, gn) * (z[:, hl] * _sigmoid(z[:, hl]))).astype(y_ref.dtype)


def gdn_recurrence(pre, cols, s0, gdn_norm, batch, nchunks, tok_block0):
    uv, wk, attn, qg, kt, gl = pre
    h = GDN_HEADS
    tok = lambda b, n: tok_block0 + b * nchunks + n
    tok_spec = lambda d: pl.BlockSpec((CHUNK, d), lambda b, n: (tok(b, n), 0))
    state_spec = pl.BlockSpec((1, h, GDN_DK, GDN_DV), lambda b, n: (b, 0, 0, 0))
    return pl.pallas_call(
        _gdn_rec_kernel,
        grid=(batch, nchunks),
        in_specs=[tok_spec(GDN_V), tok_spec(GDN_QK), tok_spec(h * CHUNK), tok_spec(GDN_QK),
                  tok_spec(GDN_QK),
                  pl.BlockSpec((1, 1, GDN_QK), lambda b, n: (tok(b, n), 0, 0)),
                  state_spec,
                  pl.BlockSpec((CHUNK, SLOT), lambda b, n: (tok(b, n), COL_Z)),
                  pl.BlockSpec((1, GDN_DV), lambda b, n: (0, 0))],
        out_specs=[pl.BlockSpec((CHUNK, GDN_V), lambda b, n: (b * nchunks + n, 0)), state_spec],
        out_shape=[jax.ShapeDtypeStruct((batch * nchunks * CHUNK, GDN_V), BF16),
                   jax.ShapeDtypeStruct((batch, h, GDN_DK, GDN_DV), F32)],
        compiler_params=_params("parallel", "arbitrary"),
        name="gdn_recurrence",
    )(uv, wk, attn, qg, kt, gl, s0, cols, gdn_norm.reshape(1, -1))


def _pad_cols(w, width):
    return jnp.pad(w, ((0, 0), (0, width - w.shape[1])))


def _layout_w_in(w):
    a0, b0 = 0, Q_LORA + KV_LORA + QK_ROPE
    c0 = b0 + 3 * BAND_HEADS * BAND_DIM
    q_a = w[:, a0:a0 + Q_LORA]
    c_kv = w[:, a0 + Q_LORA:a0 + Q_LORA + KV_LORA]
    k_r = w[:, a0 + Q_LORA + KV_LORA:b0]
    band = w[:, b0:c0]
    u = w[:, c0:c0 + C_CONV]
    z = w[:, c0 + C_CONV:c0 + C_CONV + GDN_V]
    a = w[:, c0 + C_CONV + GDN_V:c0 + C_CONV + GDN_V + GDN_HEADS]
    b = w[:, c0 + C_CONV + GDN_V + GDN_HEADS:]
    half = QK_ROPE // 2
    k_sw = jnp.concatenate([k_r[:, half:], k_r[:, :half]], axis=1)
    tail = jnp.concatenate([c_kv, _pad_cols(k_r, LANE), _pad_cols(k_sw, LANE),
                            _pad_cols(a, LANE), _pad_cols(b, LANE)], axis=1)
    return jnp.concatenate([u, q_a, band, z, tail], axis=1).astype(BF16)


def _layout_w_qb(w):
    k = w.shape[0]
    w = w.reshape(k, MLA_HEADS, QK_NOPE + QK_ROPE)
    nope = w[:, :, :QK_NOPE]
    rope = w[:, :, QK_NOPE:]
    half = QK_ROPE // 2
    rope_sw = jnp.concatenate([rope[:, :, half:], rope[:, :, :half]], axis=2)
    pad = lambda t: jnp.pad(t, ((0, 0), (0, 0), (0, LANE - QK_ROPE)))
    out = jnp.concatenate([nope, pad(rope), pad(rope_sw)], axis=1)
    return out.reshape(k, 3 * MLA_HEADS * LANE).astype(BF16)


def _rope_tables(pos):
    inv = 1.0 / (ROPE_THETA ** (jnp.arange(0, QK_ROPE, 2, dtype=F32) / QK_ROPE))
    ang = pos.astype(F32)[:, None] * inv[None, :]
    cos, sin = jnp.cos(ang), jnp.sin(ang)
    zeros = jnp.zeros((pos.shape[0], LANE - QK_ROPE), F32)
    return (jnp.concatenate([cos, cos, zeros], axis=1), jnp.concatenate([-sin, sin, zeros], axis=1))


def _band_bias_row(rel_bias):
    heads = rel_bias.shape[0]
    left = BAND_KEYS - 1 - REL_CLIP
    right = BIAS_SPAN - left - (2 * REL_CLIP + 1)
    return jnp.concatenate([jnp.broadcast_to(rel_bias[:, -1:], (heads, left)), rel_bias[:, ::-1],
                            jnp.broadcast_to(rel_bias[:, :1], (heads, right))], axis=1)


def kernel(x_prompt, x_sample, cache_latent, cache_k_rope, cache_band_k, cache_band_v, state_conv, state_delta, norm_ff1, ff1_w1, ff1_w3, ff1_w2, norm_mix, w_in, q_norm, w_qb, kv_norm, w_uk, w_uv, rel_bias, conv_w, a_log, dt_bias, gdn_norm, w_out, norm_ff2, ff2_w1, ff2_w3, ff2_w2, final_norm):
    batch, seq, d = x_prompt.shape
    dbatch, dseq, _ = x_sample.shape
    depth = w_in.shape[0]
    past = cache_latent.shape[2]
    assert dseq == CHUNK and seq % CHUNK == 0 and cache_band_k.shape[2] == BAND_ROWS
    mp, ms = batch * seq, dbatch * dseq
    m = mp + ms
    nchunks = seq // CHUNK
    blk_s = mp // CHUNK
    band_w = BAND_HEADS * BAND_DIM

    x = jnp.concatenate([x_prompt.reshape(mp, d), x_sample.reshape(ms, d)], axis=0)
    pos = jnp.concatenate([jnp.tile(jnp.arange(seq, dtype=jnp.int32), batch),
                           jnp.tile(past + jnp.arange(dseq, dtype=jnp.int32), dbatch)])
    cos128, sin128 = _rope_tables(pos)
    zero_state = jnp.zeros((batch, GDN_HEADS, GDN_DK, GDN_DV), F32)

    def ffn(x, norm, w1, w3, w2, l):
        h = rmsnorm(x, norm[l], BF16)
        hid = matmul_swiglu(h, w1, w3, l, 544, 512)
        return matmul_residual(hid, w2, l, x, 0.5, 272, 512)

    outs = [[] for _ in range(12)]
    for l in range(depth):
        x = ffn(x, norm_ff1, ff1_w1, ff1_w3, ff1_w2, l)

        h = rmsnorm(x, norm_mix[l], BF16)
        cols = matmul(h, _layout_w_in(w_in[l]), F32, 544, 1024)

        wuk_t = jnp.transpose(w_uk[l], (1, 2, 0)).astype(BF16)
        wuv_t = jnp.transpose(w_uv[l], (1, 0, 2)).astype(BF16)
        q_lat, q_rope, lat, krope, klat_bf, krope_bf = mla_prep(
            cols, q_norm[l], kv_norm[l], _layout_w_qb(w_qb[l]), wuk_t, cos128, sin128)
        ya_p = mla_attention_prompt(q_lat, q_rope, klat_bf, krope_bf, wuv_t, batch, seq)
        ya_s = mla_attention_sample(q_lat, q_rope, cache_latent[l], cache_k_rope[l], klat_bf,
                                    krope_bf, wuv_t, dbatch, blk_s)

        bias = _band_bias_row(rel_bias[l])
        band_k = cols[:, COL_BK * SLOT:(COL_BK + 1) * SLOT]
        band_v = cols[:, COL_BV * SLOT:(COL_BV + 1) * SLOT]

        def prompt_rows(t):
            t = t[:mp].reshape(batch, seq, band_w).astype(BF16)
            return jnp.pad(t, ((0, 0), (BAND_ROWS, 0), (0, 0)))

        def sample_rows(t, cache):
            return jnp.concatenate([cache.reshape(dbatch, BAND_ROWS, band_w).astype(BF16),
                                    t[mp:].reshape(dbatch, dseq, band_w).astype(BF16)], axis=1)

        yb_p = band_attention(cols, prompt_rows(band_k), prompt_rows(band_v), bias,
                              batch, nchunks, 0, BAND_ROWS)
        yb_s = band_attention(cols, sample_rows(band_k, cache_band_k[l]),
                              sample_rows(band_v, cache_band_v[l]), bias, dbatch, 1, blk_s, 0)

        conv_init = jnp.concatenate(
            [jnp.zeros((batch, 8, C_CONV), F32),
             jnp.pad(state_conv[l], ((0, 0), (8 - (CONV_W - 1), 0), (0, 0)))], axis=0)
        aneg = _pad_cols(-jnp.exp(a_log[l].astype(F32)).reshape(1, -1), LANE)
        dtb = _pad_cols(dt_bias[l].astype(F32).reshape(1, -1), LANE)
        pre = gdn_pre(cols, conv_init, conv_w[l], aneg, dtb, nchunks, mp // CHUNK)
        yc_p, delta_p = gdn_recurrence(pre, cols, zero_state, gdn_norm[l], batch, nchunks, 0)
        yc_s, delta_s = gdn_recurrence(pre, cols, state_delta[l], gdn_norm[l], dbatch, 1, blk_s)

        x = matmul_mix(jnp.concatenate([ya_p, ya_s], axis=0), jnp.concatenate([yb_p, yb_s], axis=0),
                       jnp.concatenate([yc_p, yc_s], axis=0), w_out, l, x, 544, 512)
        x = ffn(x, norm_ff2, ff2_w1, ff2_w3, ff2_w2, l)

        u_cols = cols[:, :C_CONV]
        keep = min(BAND_ROWS, seq)
        layer_out = (
            lat[:mp].reshape(batch, seq, KV_LORA),
            krope[:mp].reshape(batch, seq, QK_ROPE),
            band_k[:mp].reshape(batch, seq, BAND_HEADS, BAND_DIM)[:, seq - keep:],
            band_v[:mp].reshape(batch, seq, BAND_HEADS, BAND_DIM)[:, seq - keep:],
            u_cols[:mp].reshape(batch, seq, C_CONV)[:, seq - (CONV_W - 1):],
            delta_p,
            lat[mp:].reshape(dbatch, dseq, KV_LORA),
            krope[mp:].reshape(dbatch, dseq, QK_ROPE),
            band_k[mp:].reshape(dbatch, dseq, BAND_HEADS, BAND_DIM),
            band_v[mp:].reshape(dbatch, dseq, BAND_HEADS, BAND_DIM),
            u_cols[mp:].reshape(dbatch, dseq, C_CONV)[:, dseq - (CONV_W - 1):],
            delta_s,
        )
        for acc, t in zip(outs, layer_out):
            acc.append(t)

    y_prompt = rmsnorm(x, final_norm, F32, rows=mp).reshape(batch, seq, d)
    y_sample = rmsnorm(x, final_norm, F32, rows=ms, row_start=mp).reshape(dbatch, dseq, d)
    return (y_prompt, y_sample) + tuple(jnp.stack(t) for t in outs)
```

```python
import functools
import math

import jax
import jax.numpy as jnp
from jax import lax
from jax.experimental import pallas as pl
from jax.experimental.pallas import tpu as pltpu

F32 = jnp.float32
BF16 = jnp.bfloat16
HIGHEST = lax.Precision.HIGHEST

CHUNK = 64
CHUNK_SHIFT = 6
EPS = 1e-6
NEG = -1e30
Q_LORA = 1024
KV_LORA = 512
QK_NOPE = 128
QK_ROPE = 64
V_HEAD = 128
MLA_HEADS = 16
ROPE_THETA = 10000.0
MLA_SCALE = (QK_NOPE + QK_ROPE) ** -0.5
MLA_HEAD_GROUPS = 2
BAND_HEADS = 8
BAND_DIM = 128
BAND_PREV = 8
BAND_ROWS = BAND_PREV * CHUNK
BAND_KEYS = BAND_ROWS + CHUNK
REL_CLIP = 128
BAND_SCALE = BAND_DIM ** -0.5
GDN_HEADS = 8
GDN_GROUP = 8
GDN_DK = 128
GDN_DV = 128
CONV_W = 4
GDN_QK = GDN_HEADS * GDN_DK
GDN_V = GDN_HEADS * GDN_DV
C_CONV = 2 * GDN_QK + GDN_V
LANE = 128
SLOT = 1024
COL_U, COL_QA, COL_BQ, COL_BK, COL_BV, COL_Z, COL_TAIL = 0, 3, 4, 5, 6, 7, 8
IN_COLS_PAD = 9 * SLOT
TAIL_KR, TAIL_KSW, TAIL_A, TAIL_B = 512, 640, 768, 896

VMEM_LIMIT_BYTES = 56 * 1024 * 1024


def _params(*sem):
    return pltpu.CompilerParams(dimension_semantics=sem, vmem_limit_bytes=VMEM_LIMIT_BYTES)


def _tile(n, target, mult=16):
    best = None
    for t in range(mult, min(n, target) + 1, mult):
        if n % t == 0:
            best = t
    assert best is not None, (n, target)
    return best


def _dot(a, b, precision=None):
    return jnp.dot(a, b, preferred_element_type=F32, precision=precision)


def _dot_nt(a, b, precision=None):
    return lax.dot_general(a, b, (((1,), (1,)), ((), ())), preferred_element_type=F32,
                           precision=precision)


def _dot_tn(a, b, precision=None):
    return lax.dot_general(a, b, (((0,), (0,)), ((), ())), preferred_element_type=F32,
                           precision=precision)


def _split_bf16(x):
    hi = x.astype(BF16)
    return hi, (x - hi.astype(F32)).astype(BF16)


def _dot_3pass(a, b, expand=lambda t: t):
    a_hi, a_lo = _split_bf16(a)
    b_hi, b_lo = _split_bf16(b)
    m = a.shape[0]
    top = _dot(jnp.concatenate([a_hi, a_lo], axis=0), expand(b_hi))
    return top[:m] + top[m:] + _dot(a_hi, expand(b_lo))


def _sigmoid(x):
    return 1.0 / (1.0 + jnp.exp(-x))


def _rms(x, g):
    return x * lax.rsqrt(jnp.mean(x * x, axis=-1, keepdims=True) + EPS) * g


def _rmsnorm_kernel(x_ref, g_ref, o_ref):
    o_ref[...] = _rms(x_ref[...], g_ref[...]).astype(o_ref.dtype)


def rmsnorm(x, g, out_dtype, rows=None, row_start=0):
    m, d = x.shape
    rows = m if rows is None else rows
    tm = _tile(math.gcd(rows, row_start) if row_start else rows, 256, 8)
    off = row_start // tm
    return pl.pallas_call(
        _rmsnorm_kernel,
        grid=(rows // tm,),
        in_specs=[pl.BlockSpec((tm, d), lambda i: (i + off, 0)),
                  pl.BlockSpec((1, d), lambda i: (0, 0))],
        out_specs=pl.BlockSpec((tm, d), lambda i: (i, 0)),
        out_shape=jax.ShapeDtypeStruct((rows, d), out_dtype),
        compiler_params=_params("parallel"),
        name="rmsnorm",
    )(x, g.reshape(1, d))


def _mm_kernel(a_ref, w_ref, o_ref):
    o_ref[...] = _dot(a_ref[...], w_ref[...]).astype(o_ref.dtype)


def _cast_weight(w_ref, wb_ref):
    @pl.when(pl.program_id(1) == 0)
    def _():
        wb_ref[...] = w_ref[...].astype(BF16)


def _mm_res_kernel(a_ref, w_ref, r_ref, o_ref, wb_ref, *, scale):
    _cast_weight(w_ref, wb_ref)
    o_ref[...] = r_ref[...] + scale * _dot(a_ref[...], wb_ref[...])


def _mm_mix_kernel(ya_ref, yb_ref, yc_ref, w_ref, r_ref, o_ref, wb_ref):
    _cast_weight(w_ref, wb_ref)
    ka, kb = ya_ref.shape[1], yb_ref.shape[1]
    o_ref[...] = (r_ref[...] + _dot(ya_ref[...], wb_ref[0:ka, :])
                  + _dot(yb_ref[...], wb_ref[ka:ka + kb, :]) + _dot(yc_ref[...], wb_ref[ka + kb:, :]))


def _mm_swiglu_kernel(a_ref, w1_ref, w3_ref, o_ref, w1b_ref, w3b_ref):
    _cast_weight(w1_ref, w1b_ref)
    _cast_weight(w3_ref, w3b_ref)
    a = a_ref[...]
    h1 = _dot(a, w1b_ref[...])
    h3 = _dot(a, w3b_ref[...])
    o_ref[...] = (h1 * _sigmoid(h1) * h3).astype(o_ref.dtype)


def matmul(a, w, out_dtype, tm_target, tn_target):
    m, k = a.shape
    n = w.shape[1]
    tm, tn = _tile(m, tm_target), _tile(n, tn_target, LANE)
    return pl.pallas_call(
        _mm_kernel,
        grid=(n // tn, m // tm),
        in_specs=[pl.BlockSpec((tm, k), lambda j, i: (i, 0)),
                  pl.BlockSpec((k, tn), lambda j, i: (0, j))],
        out_specs=pl.BlockSpec((tm, tn), lambda j, i: (i, j)),
        out_shape=jax.ShapeDtypeStruct((m, n), out_dtype),
        compiler_params=_params("parallel", "parallel"),
        name="matmul",
    )(a, w)


def _layer_weight_spec(layer, k, tn):
    return pl.BlockSpec((None, k, tn), lambda j, i: (layer, 0, j))


def matmul_residual(a, w, layer, res, scale, tm_target, tn_target):
    m, k = a.shape
    n = w.shape[2]
    tm, tn = _tile(m, tm_target), _tile(n, tn_target, LANE)
    return pl.pallas_call(
        functools.partial(_mm_res_kernel, scale=scale),
        grid=(n // tn, m // tm),
        in_specs=[pl.BlockSpec((tm, k), lambda j, i: (i, 0)),
                  _layer_weight_spec(layer, k, tn),
                  pl.BlockSpec((tm, tn), lambda j, i: (i, j))],
        out_specs=pl.BlockSpec((tm, tn), lambda j, i: (i, j)),
        out_shape=jax.ShapeDtypeStruct((m, n), F32),
        scratch_shapes=[pltpu.VMEM((k, tn), BF16)],
        compiler_params=_params("parallel", "arbitrary"),
        name="matmul_residual",
    )(a, w, res)


def matmul_mix(ya, yb, yc, w, layer, res, tm_target, tn_target):
    m = ya.shape[0]
    k, n = w.shape[1], w.shape[2]
    tm, tn = _tile(m, tm_target), _tile(n, tn_target, LANE)
    row = lambda y: pl.BlockSpec((tm, y.shape[1]), lambda j, i: (i, 0))
    return pl.pallas_call(
        _mm_mix_kernel,
        grid=(n // tn, m // tm),
        in_specs=[row(ya), row(yb), row(yc), _layer_weight_spec(layer, k, tn),
                  pl.BlockSpec((tm, tn), lambda j, i: (i, j))],
        out_specs=pl.BlockSpec((tm, tn), lambda j, i: (i, j)),
        out_shape=jax.ShapeDtypeStruct((m, n), F32),
        scratch_shapes=[pltpu.VMEM((k, tn), BF16)],
        compiler_params=_params("parallel", "arbitrary"),
        name="matmul_mix",
    )(ya, yb, yc, w, res)


def matmul_swiglu(a, w1, w3, layer, tm_target, tn_target):
    m, k = a.shape
    n = w1.shape[2]
    tm, tn = _tile(m, tm_target), _tile(n, tn_target, LANE)
    return pl.pallas_call(
        _mm_swiglu_kernel,
        grid=(n // tn, m // tm),
        in_specs=[pl.BlockSpec((tm, k), lambda j, i: (i, 0)),
                  _layer_weight_spec(layer, k, tn),
                  _layer_weight_spec(layer, k, tn)],
        out_specs=pl.BlockSpec((tm, tn), lambda j, i: (i, j)),
        out_shape=jax.ShapeDtypeStruct((m, n), BF16),
        scratch_shapes=[pltpu.VMEM((k, tn), BF16), pltpu.VMEM((k, tn), BF16)],
        compiler_params=_params("parallel", "arbitrary"),
        name="matmul_swiglu",
    )(a, w1, w3)


def _mla_prep_kernel(qa_ref, tail_ref, qn_ref, kvn_ref, wq_ref, wuk_ref, cos_ref, sin_ref,
                     qlat_ref, qrope_ref, lat_ref, krope_ref, klat_bf_ref, krope_bf_ref):
    hq = _rms(qa_ref[...], qn_ref[...]).astype(BF16)
    q_all = _dot(hq, wq_ref[...])
    cos = cos_ref[...]
    sin = sin_ref[...]
    hw = MLA_HEADS * LANE
    for h in range(MLA_HEADS):
        qn = q_all[:, h * LANE:(h + 1) * LANE].astype(BF16)
        qlat_ref[h] = (_dot(qn, wuk_ref[h]) * MLA_SCALE).astype(BF16)
        qr = (q_all[:, hw + h * LANE:hw + (h + 1) * LANE] * cos
              + q_all[:, 2 * hw + h * LANE:2 * hw + (h + 1) * LANE] * sin)
        qrope_ref[h] = (qr[:, :QK_ROPE] * MLA_SCALE).astype(BF16)
    tail = tail_ref[...]
    lat = _rms(tail[:, :KV_LORA], kvn_ref[...])
    lat_ref[...] = lat
    klat_bf_ref[...] = lat.astype(BF16)
    kr = tail[:, TAIL_KR:TAIL_KR + LANE] * cos + tail[:, TAIL_KSW:TAIL_KSW + LANE] * sin
    krope_ref[...] = kr[:, :QK_ROPE]
    krope_bf_ref[...] = kr[:, :QK_ROPE].astype(BF16)


def mla_prep(cols, q_norm, kv_norm, wq_all, wuk_t, cos128, sin128):
    m = cols.shape[0]
    tm = _tile(m, 256)
    h = MLA_HEADS
    const2 = lambda i: (0, 0)
    return pl.pallas_call(
        _mla_prep_kernel,
        grid=(m // tm,),
        in_specs=[pl.BlockSpec((tm, SLOT), lambda i: (i, COL_QA)),
                  pl.BlockSpec((tm, SLOT), lambda i: (i, COL_TAIL)),
                  pl.BlockSpec((1, Q_LORA), const2),
                  pl.BlockSpec((1, KV_LORA), const2),
                  pl.BlockSpec(wq_all.shape, const2, pipeline_mode=pl.Buffered(1)),
                  pl.BlockSpec(wuk_t.shape, lambda i: (0, 0, 0), pipeline_mode=pl.Buffered(1)),
                  pl.BlockSpec((tm, LANE), lambda i: (i, 0)),
                  pl.BlockSpec((tm, LANE), lambda i: (i, 0))],
        out_specs=[pl.BlockSpec((h, tm, KV_LORA), lambda i: (0, i, 0)),
                   pl.BlockSpec((h, tm, QK_ROPE), lambda i: (0, i, 0)),
                   pl.BlockSpec((tm, KV_LORA), lambda i: (i, 0)),
                   pl.BlockSpec((tm, QK_ROPE), lambda i: (i, 0)),
                   pl.BlockSpec((tm, KV_LORA), lambda i: (i, 0)),
                   pl.BlockSpec((tm, QK_ROPE), lambda i: (i, 0))],
        out_shape=[jax.ShapeDtypeStruct((h, m, KV_LORA), BF16),
                   jax.ShapeDtypeStruct((h, m, QK_ROPE), BF16),
                   jax.ShapeDtypeStruct((m, KV_LORA), F32),
                   jax.ShapeDtypeStruct((m, QK_ROPE), F32),
                   jax.ShapeDtypeStruct((m, KV_LORA), BF16),
                   jax.ShapeDtypeStruct((m, QK_ROPE), BF16)],
        compiler_params=_params("parallel"),
        name="mla_prep",
    )(cols, cols, q_norm.reshape(1, -1), kv_norm.reshape(1, -1), wq_all, wuk_t, cos128, sin128)


def _softmax_init(m_ref, l_ref, acc_ref):
    m_ref[...] = jnp.full(m_ref.shape, NEG, F32)
    l_ref[...] = jnp.zeros(l_ref.shape, F32)
    acc_ref[...] = jnp.zeros(acc_ref.shape, F32)


def _mla_step(ql_ref, qr_ref, k_l, k_r, m_ref, l_ref, acc_ref, tq, visible=None):
    hg = MLA_HEADS // MLA_HEAD_GROUPS
    rows = hg * tq
    for g in range(MLA_HEAD_GROUPS):
        q_l = ql_ref[g * hg:(g + 1) * hg].reshape(rows, KV_LORA)
        q_r = qr_ref[g * hg:(g + 1) * hg].reshape(rows, QK_ROPE)
        s = _dot_nt(q_l, k_l) + _dot_nt(q_r, k_r)
        if visible is not None:
            s = jnp.where(visible, s, NEG)
        rs = slice(g * rows, (g + 1) * rows)
        m_old = m_ref[rs]
        m_new = jnp.maximum(m_old, jnp.max(s, axis=-1, keepdims=True))
        alpha = jnp.exp(m_old - m_new)
        p = jnp.exp(s - m_new)
        l_ref[rs] = alpha * l_ref[rs] + jnp.sum(p, axis=-1, keepdims=True)
        acc_ref[rs] = alpha * acc_ref[rs] + _dot(p.astype(BF16), k_l)
        m_ref[rs] = m_new


def _mla_finish(wuv_ref, o_ref, l_ref, acc_ref, tq):
    ctx = acc_ref[...] / l_ref[...]
    for h in range(MLA_HEADS):
        ch = ctx[h * tq:(h + 1) * tq].astype(BF16)
        o_ref[:, h * V_HEAD:(h + 1) * V_HEAD] = _dot(ch, wuv_ref[h]).astype(o_ref.dtype)


def _mla_prompt_kernel(ql_ref, qr_ref, kl_ref, kr_ref, wuv_ref, o_ref, m_ref, l_ref, acc_ref,
                       *, tq, tk):
    qi = pl.program_id(1)
    kj = pl.program_id(2)
    last = (qi * tq + tq - 1) // tk

    @pl.when(kj == 0)
    def _():
        _softmax_init(m_ref, l_ref, acc_ref)

    @pl.when(kj < last)
    def _():
        _mla_step(ql_ref, qr_ref, kl_ref[...], kr_ref[...], m_ref, l_ref, acc_ref, tq)

    @pl.when(kj == last)
    def _():
        shape = (MLA_HEADS // MLA_HEAD_GROUPS * tq, tk)
        row = lax.broadcasted_iota(jnp.int32, shape, 0)
        col = lax.broadcasted_iota(jnp.int32, shape, 1)
        q_chunk = (qi * tq + row % tq) // CHUNK
        k_chunk = (kj * tk + col) // CHUNK
        _mla_step(ql_ref, qr_ref, kl_ref[...], kr_ref[...], m_ref, l_ref, acc_ref, tq,
                  visible=k_chunk <= q_chunk)

    @pl.when(kj == pl.num_programs(2) - 1)
    def _():
        _mla_finish(wuv_ref, o_ref, l_ref, acc_ref, tq)


def mla_attention_prompt(q_lat, q_rope, k_lat, k_rope, wuv, batch, seq):
    tq = 128
    tk = _tile(seq, 512, CHUNK)
    nq, nk = seq // tq, seq // tk
    h = MLA_HEADS
    rows = h * tq

    def k_map(b, qi, kj):
        return (b * nk + jnp.minimum(kj, (qi * tq + tq - 1) // tk), 0)

    return pl.pallas_call(
        functools.partial(_mla_prompt_kernel, tq=tq, tk=tk),
        grid=(batch, nq, nk),
        in_specs=[pl.BlockSpec((h, tq, KV_LORA), lambda b, qi, kj: (0, b * nq + qi, 0)),
                  pl.BlockSpec((h, tq, QK_ROPE), lambda b, qi, kj: (0, b * nq + qi, 0)),
                  pl.BlockSpec((tk, KV_LORA), k_map),
                  pl.BlockSpec((tk, QK_ROPE), k_map),
                  pl.BlockSpec(wuv.shape, lambda b, qi, kj: (0, 0, 0))],
        out_specs=pl.BlockSpec((tq, h * V_HEAD), lambda b, qi, kj: (b * nq + qi, 0)),
        out_shape=jax.ShapeDtypeStruct((batch * seq, h * V_HEAD), BF16),
        scratch_shapes=[pltpu.VMEM((rows, 1), F32), pltpu.VMEM((rows, 1), F32),
                        pltpu.VMEM((rows, KV_LORA), F32)],
        compiler_params=_params("parallel", "parallel", "arbitrary"),
        name="mla_attention_prompt",
    )(q_lat, q_rope, k_lat, k_rope, wuv)


def _mla_sample_kernel(ql_ref, qr_ref, pl_ref, pr_ref, kl_ref, kr_ref, wuv_ref, o_ref,
                       m_ref, l_ref, acc_ref):
    kj = pl.program_id(1)
    n_past = pl.num_programs(1) - 1

    @pl.when(kj == 0)
    def _():
        _softmax_init(m_ref, l_ref, acc_ref)

    @pl.when(kj < n_past)
    def _():
        _mla_step(ql_ref, qr_ref, pl_ref[0].astype(BF16), pr_ref[0].astype(BF16),
                  m_ref, l_ref, acc_ref, CHUNK)

    @pl.when(kj == n_past)
    def _():
        _mla_step(ql_ref, qr_ref, kl_ref[...], kr_ref[...], m_ref, l_ref, acc_ref, CHUNK)
        _mla_finish(wuv_ref, o_ref, l_ref, acc_ref, CHUNK)


def mla_attention_sample(q_lat, q_rope, past_lat, past_rope, k_lat, k_rope, wuv, batch, tok_block0):
    past = past_lat.shape[1]
    tk = _tile(past, 512, CHUNK)
    n_past = past // tk
    h = MLA_HEADS
    rows = h * CHUNK
    return pl.pallas_call(
        _mla_sample_kernel,
        grid=(batch, n_past + 1),
        in_specs=[pl.BlockSpec((h, CHUNK, KV_LORA), lambda b, kj: (0, tok_block0 + b, 0)),
                  pl.BlockSpec((h, CHUNK, QK_ROPE), lambda b, kj: (0, tok_block0 + b, 0)),
                  pl.BlockSpec((1, tk, KV_LORA), lambda b, kj: (b, jnp.minimum(kj, n_past - 1), 0)),
                  pl.BlockSpec((1, tk, QK_ROPE), lambda b, kj: (b, jnp.minimum(kj, n_past - 1), 0)),
                  pl.BlockSpec((CHUNK, KV_LORA), lambda b, kj: (tok_block0 + b, 0)),
                  pl.BlockSpec((CHUNK, QK_ROPE), lambda b, kj: (tok_block0 + b, 0)),
                  pl.BlockSpec(wuv.shape, lambda b, kj: (0, 0, 0))],
        out_specs=pl.BlockSpec((CHUNK, h * V_HEAD), lambda b, kj: (b, 0)),
        out_shape=jax.ShapeDtypeStruct((batch * CHUNK, h * V_HEAD), BF16),
        scratch_shapes=[pltpu.VMEM((rows, 1), F32), pltpu.VMEM((rows, 1), F32),
                        pltpu.VMEM((rows, KV_LORA), F32)],
        compiler_params=_params("parallel", "arbitrary"),
        name="mla_attention_sample",
    )(q_lat, q_rope, past_lat, past_rope, k_lat, k_rope, wuv)


def _band_span(qchunks):
    tq, nk = qchunks * CHUNK, (qchunks + BAND_PREV) * CHUNK
    return -(-(tq + nk - 1) // LANE) * LANE


def _band_kernel(q_ref, k_ref, v_ref, g_ref, o_ref, bias_ref, *, pad_rows, qchunks):
    n = pl.program_id(1)
    tq, nk = qchunks * CHUNK, (qchunks + BAND_PREV) * CHUNK
    span = _band_span(qchunks)

    @pl.when(jnp.logical_and(pl.program_id(0) == 0, n == 0))
    def _():
        for h in range(BAND_HEADS):
            rows = jnp.broadcast_to(g_ref[h:h + 1, :], (tq, span))
            rolled = pltpu.roll(rows, span - (tq - 1), 1, stride=1, stride_axis=0)
            bias_ref[h] = rolled[:, :nk]

    start = pl.multiple_of(n * tq, CHUNK)
    row = lax.broadcasted_iota(jnp.int32, (tq, nk), 0)
    col = lax.broadcasted_iota(jnp.int32, (tq, nk), 1)
    visible = start + col >= pad_rows
    if qchunks > 1:
        ahead = (col >> CHUNK_SHIFT) - (row >> CHUNK_SHIFT)
        visible = visible & (ahead >= 0) & (ahead <= BAND_PREV)
    q = q_ref[...]
    for h in range(BAND_HEADS):
        lanes = slice(h * BAND_DIM, (h + 1) * BAND_DIM)
        qh = q[:, lanes].astype(BF16)
        kh = k_ref[0, pl.ds(start, nk), lanes]
        vh = v_ref[0, pl.ds(start, nk), lanes]
        s = _dot_nt(qh, kh) * BAND_SCALE + bias_ref[h]
        if pad_rows or qchunks > 1:
            s = jnp.where(visible, s, NEG)
        p = jnp.exp(s - jnp.max(s, axis=-1, keepdims=True))
        o = _dot(p.astype(BF16), vh) / jnp.sum(p, axis=-1, keepdims=True)
        o_ref[:, lanes] = o.astype(o_ref.dtype)


def band_attention(cols, k_rows, v_rows, rel_bias, batch, nchunks, tok_block0, pad_rows, qchunks):
    assert nchunks % qchunks == 0
    width = BAND_HEADS * BAND_DIM
    rows = k_rows.shape[1]
    tq, nk = qchunks * CHUNK, (qchunks + BAND_PREV) * CHUNK
    nblk = nchunks // qchunks
    bias_row = _band_bias_row(rel_bias, qchunks)
    return pl.pallas_call(
        functools.partial(_band_kernel, pad_rows=pad_rows, qchunks=qchunks),
        grid=(batch, nblk),
        in_specs=[pl.BlockSpec((tq, width), lambda b, n: (tok_block0 // qchunks + b * nblk + n, COL_BQ)),
                  pl.BlockSpec((1, rows, width), lambda b, n: (b, 0, 0)),
                  pl.BlockSpec((1, rows, width), lambda b, n: (b, 0, 0)),
                  pl.BlockSpec(bias_row.shape, lambda b, n: (0, 0))],
        out_specs=pl.BlockSpec((tq, width), lambda b, n: (b * nblk + n, 0)),
        out_shape=jax.ShapeDtypeStruct((batch * nchunks * CHUNK, width), BF16),
        scratch_shapes=[pltpu.VMEM((BAND_HEADS, tq, nk), F32)],
        compiler_params=_params("arbitrary", "arbitrary"),
        name="band_attention",
    )(cols, k_rows, v_rows, bias_row)


def _expand_matrix(rows, first, width, per_shift):
    r = lax.broadcasted_iota(jnp.int32, (rows, width), 0)
    c = lax.broadcasted_iota(jnp.int32, (rows, width), 1)
    return (r == first + (c >> per_shift)).astype(F32)


def _unit_lower_inverse(a, row, col, group):
    gl = a.shape[1]
    r = lax.broadcasted_iota(jnp.int32, (gl, gl), 0)
    c = lax.broadcasted_iota(jnp.int32, (gl, gl), 1)
    diag_blocks = ((r >> CHUNK_SHIFT) == (c >> CHUNK_SHIFT)).astype(BF16)

    def block_diag(w):
        return jnp.concatenate([w] * group, axis=0) * diag_blocks

    t = (row == col).astype(F32) - jnp.where((row >> 1) == (col >> 1), a, 0.0)
    for shift in range(1, CHUNK_SHIFT):
        lower_left = ((row >> (shift + 1)) == (col >> (shift + 1))) & ((row >> shift) != (col >> shift))
        x = _dot_3pass(t, jnp.where(lower_left, a, 0.0), block_diag)
        t = t - _dot_3pass(x, t, block_diag)
    return t


def _l2norm_heads(x, heads, width):
    parts = []
    for h in range(heads):
        xh = x[:, h * width:(h + 1) * width]
        parts.append(xh * lax.rsqrt(jnp.sum(xh * xh, axis=-1, keepdims=True) + EPS))
    return jnp.concatenate(parts, axis=1)


def _gdn_pre_kernel(u_ref, uprev_ref, init_ref, tail_ref, cw_ref, aneg_ref, dtb_ref,
                    uv_ref, wk_ref, attn_ref, qg_ref, kt_ref, gl_ref, ext_ref,
                    *, chunks_per_seq, n_prompt_chunks):
    group = GDN_GROUP
    gw = group * CHUNK
    c = pl.program_id(0)
    is_start = jnp.logical_or(c >= n_prompt_chunks, c % chunks_per_seq == 0)
    ext_ref[0:8, :] = jnp.where(is_start, init_ref[0], uprev_ref[...])
    ext_ref[8:8 + CHUNK, :] = u_ref[...]
    cw = cw_ref[...]
    uc = (cw[3:4] * ext_ref[8:8 + CHUNK, :] + cw[2:3] * ext_ref[7:7 + CHUNK, :]
          + cw[1:2] * ext_ref[6:6 + CHUNK, :] + cw[0:1] * ext_ref[5:5 + CHUNK, :])
    uc = uc * _sigmoid(uc)

    tail = tail_ref[...]
    xa = tail[:, TAIL_A:TAIL_A + LANE] + dtb_ref[...]
    softplus = jnp.maximum(xa, 0.0) + jnp.log(1.0 + jnp.exp(-jnp.abs(xa)))
    g = aneg_ref[...] * softplus
    beta = _sigmoid(tail[:, TAIL_B:TAIL_B + LANE])

    r64 = lax.broadcasted_iota(jnp.int32, (CHUNK, CHUNK), 0)
    c64 = lax.broadcasted_iota(jnp.int32, (CHUNK, CHUNK), 1)
    gc = _dot((r64 >= c64).astype(F32), g, HIGHEST)
    egc = jnp.exp(gc)
    ekt = jnp.exp(gc[CHUNK - 1:CHUNK, :] - gc)
    e128 = _expand_matrix(LANE, 0, GDN_HEADS * GDN_DK, 7)
    beta_x = _dot(beta, e128, HIGHEST)
    egc_x = _dot(egc, e128, HIGHEST)
    ekt_x = _dot(ekt, e128, HIGHEST)
    gl_ref[0] = egc_x[CHUNK - 1:CHUNK, :]

    qn = _l2norm_heads(uc[:, :GDN_QK], GDN_HEADS, GDN_DK) * (GDN_DK ** -0.5)
    kn = _l2norm_heads(uc[:, GDN_QK:2 * GDN_QK], GDN_HEADS, GDN_DK)
    v_all = uc[:, 2 * GDN_QK:]
    qg_ref[...] = (qn * egc_x).astype(BF16)
    kt_ref[...] = (kn * ekt_x).astype(BF16)
    rhs_v = beta_x * v_all
    rhs_k = beta_x * egc_x * kn

    row = lax.broadcasted_iota(jnp.int32, (CHUNK, gw), 0)
    col = lax.broadcasted_iota(jnp.int32, (CHUNK, gw), 1) & (CHUNK - 1)
    incl = row >= col
    head_r = lax.broadcasted_iota(jnp.int32, (gw, group * GDN_DK), 0) >> CHUNK_SHIFT
    head_c = lax.broadcasted_iota(jnp.int32, (gw, group * GDN_DK), 1) >> 7
    for grp in range(GDN_HEADS // group):
        lanes = slice(grp * group * GDN_DK, (grp + 1) * group * GDN_DK)
        e64 = _expand_matrix(LANE, grp * group, gw, CHUNK_SHIFT)
        gcw = _dot(gc, e64, HIGHEST)
        bw = _dot(beta, e64, HIGHEST)
        gc_t = jnp.sum(jnp.where(row == col, gcw, 0.0), axis=0, keepdims=True)
        decay = jnp.where(incl, jnp.exp(jnp.where(incl, gcw - gc_t, 0.0)), 0.0)
        kg = kn[:, lanes]
        k_rows = jnp.where(head_r == head_c, jnp.concatenate([kg] * group, axis=0), 0.0).astype(BF16)
        kk = _dot_nt(kg.astype(BF16), k_rows)
        qk = _dot_nt(qn[:, lanes].astype(BF16), k_rows)
        a_mat = jnp.where(row > col, bw * kk * decay, 0.0)
        t_inv = _unit_lower_inverse(a_mat, row, col, group)
        attn_ref[:, grp * gw:(grp + 1) * gw] = (qk * decay).astype(BF16)
        for hh in range(group):
            h = grp * group + hh
            hl = slice(h * GDN_DK, (h + 1) * GDN_DK)
            rhs = jnp.concatenate([rhs_v[:, hl], rhs_k[:, hl]], axis=1)
            sol = _dot_3pass(t_inv[:, hh * CHUNK:(hh + 1) * CHUNK], rhs)
            uv_ref[:, hl] = sol[:, :GDN_DV]
            wk_ref[:, hl] = sol[:, GDN_DV:].astype(BF16)


def gdn_pre(cols, conv_init, conv_w, aneg, dtb, chunks_per_seq, n_prompt_chunks):
    m = cols.shape[0]
    nch = m // CHUNK
    h = GDN_HEADS

    def init_map(c):
        return (jnp.where(c < n_prompt_chunks, c // chunks_per_seq,
                          n_prompt_chunks // chunks_per_seq + c - n_prompt_chunks), 0, 0)

    tok_spec = lambda d: pl.BlockSpec((CHUNK, d), lambda c: (c, 0))
    return pl.pallas_call(
        functools.partial(_gdn_pre_kernel, chunks_per_seq=chunks_per_seq,
                          n_prompt_chunks=n_prompt_chunks),
        grid=(nch,),
        in_specs=[pl.BlockSpec((CHUNK, C_CONV), lambda c: (c, COL_U)),
                  pl.BlockSpec((8, C_CONV), lambda c: (jnp.maximum(c * (CHUNK // 8) - 1, 0), COL_U)),
                  pl.BlockSpec((1, 8, C_CONV), init_map),
                  pl.BlockSpec((CHUNK, SLOT), lambda c: (c, COL_TAIL)),
                  pl.BlockSpec((CONV_W, C_CONV), lambda c: (0, 0)),
                  pl.BlockSpec((1, LANE), lambda c: (0, 0)),
                  pl.BlockSpec((1, LANE), lambda c: (0, 0))],
        out_specs=[tok_spec(GDN_V), tok_spec(GDN_QK), tok_spec(h * CHUNK), tok_spec(GDN_QK),
                   tok_spec(GDN_QK), pl.BlockSpec((1, 1, GDN_QK), lambda c: (c, 0, 0))],
        out_shape=[jax.ShapeDtypeStruct((m, GDN_V), F32),
                   jax.ShapeDtypeStruct((m, GDN_QK), BF16),
                   jax.ShapeDtypeStruct((m, h * CHUNK), BF16),
                   jax.ShapeDtypeStruct((m, GDN_QK), BF16),
                   jax.ShapeDtypeStruct((m, GDN_QK), BF16),
                   jax.ShapeDtypeStruct((nch, 1, GDN_QK), F32)],
        scratch_shapes=[pltpu.VMEM((8 + CHUNK, C_CONV), F32)],
        compiler_params=_params("parallel"),
        name="gdn_pre",
    )(cols, cols, conv_init, cols, conv_w, aneg, dtb)


def _gdn_rec_kernel(uv_ref, wk_ref, attn_ref, qg_ref, kt_ref, gl_ref, s0_ref, z_ref, gn_ref,
                    y_ref, s_ref):
    @pl.when(pl.program_id(1) == 0)
    def _():
        s_ref[...] = s0_ref[...]

    z = z_ref[...]
    gn = gn_ref[...]
    for h in range(GDN_HEADS):
        hl = slice(h * GDN_DK, (h + 1) * GDN_DK)
        s = s_ref[0, h]
        sb = s.astype(BF16)
        u = uv_ref[:, hl] - _dot(wk_ref[:, hl], sb)
        ub = u.astype(BF16)
        o = _dot(qg_ref[:, hl], sb) + _dot(attn_ref[:, h * CHUNK:(h + 1) * CHUNK], ub)
        s_ref[0, h] = gl_ref[0, :, hl] * s + _dot_tn(kt_ref[:, hl], ub)
        y_ref[:, hl] = (_rms(o, gn) * (z[:, hl] * _sigmoid(z[:, hl]))).astype(y_ref.dtype)


def gdn_recurrence(pre, cols, s0, gdn_norm, batch, nchunks, tok_block0):
    uv, wk, attn, qg, kt, gl = pre
    h = GDN_HEADS
    tok = lambda b, n: tok_block0 + b * nchunks + n
    tok_spec = lambda d: pl.BlockSpec((CHUNK, d), lambda b, n: (tok(b, n), 0))
    state_spec = pl.BlockSpec((1, h, GDN_DK, GDN_DV), lambda b, n: (b, 0, 0, 0))
    return pl.pallas_call(
        _gdn_rec_kernel,
        grid=(batch, nchunks),
        in_specs=[tok_spec(GDN_V), tok_spec(GDN_QK), tok_spec(h * CHUNK), tok_spec(GDN_QK),
                  tok_spec(GDN_QK),
                  pl.BlockSpec((1, 1, GDN_QK), lambda b, n: (tok(b, n), 0, 0)),
                  state_spec,
                  pl.BlockSpec((CHUNK, SLOT), lambda b, n: (tok(b, n), COL_Z)),
                  pl.BlockSpec((1, GDN_DV), lambda b, n: (0, 0))],
        out_specs=[pl.BlockSpec((CHUNK, GDN_V), lambda b, n: (b * nchunks + n, 0)), state_spec],
        out_shape=[jax.ShapeDtypeStruct((batch * nchunks * CHUNK, GDN_V), BF16),
                   jax.ShapeDtypeStruct((batch, h, GDN_DK, GDN_DV), F32)],
        compiler_params=_params("parallel", "arbitrary"),
        name="gdn_recurrence",
    )(uv, wk, attn, qg, kt, gl, s0, cols, gdn_norm.reshape(1, -1))


def _pad_cols(w, width):
    return jnp.pad(w, ((0, 0), (0, width - w.shape[1])))


def _layout_w_in(w):
    a0, b0 = 0, Q_LORA + KV_LORA + QK_ROPE
    c0 = b0 + 3 * BAND_HEADS * BAND_DIM
    q_a = w[:, a0:a0 + Q_LORA]
    c_kv = w[:, a0 + Q_LORA:a0 + Q_LORA + KV_LORA]
    k_r = w[:, a0 + Q_LORA + KV_LORA:b0]
    band = w[:, b0:c0]
    u = w[:, c0:c0 + C_CONV]
    z = w[:, c0 + C_CONV:c0 + C_CONV + GDN_V]
    a = w[:, c0 + C_CONV + GDN_V:c0 + C_CONV + GDN_V + GDN_HEADS]
    b = w[:, c0 + C_CONV + GDN_V + GDN_HEADS:]
    half = QK_ROPE // 2
    k_sw = jnp.concatenate([k_r[:, half:], k_r[:, :half]], axis=1)
    tail = jnp.concatenate([c_kv, _pad_cols(k_r, LANE), _pad_cols(k_sw, LANE),
                            _pad_cols(a, LANE), _pad_cols(b, LANE)], axis=1)
    return jnp.concatenate([u, q_a, band, z, tail], axis=1).astype(BF16)


def _layout_w_qb(w):
    k = w.shape[0]
    w = w.reshape(k, MLA_HEADS, QK_NOPE + QK_ROPE)
    nope = w[:, :, :QK_NOPE]
    rope = w[:, :, QK_NOPE:]
    half = QK_ROPE // 2
    rope_sw = jnp.concatenate([rope[:, :, half:], rope[:, :, :half]], axis=2)
    pad = lambda t: jnp.pad(t, ((0, 0), (0, 0), (0, LANE - QK_ROPE)))
    out = jnp.concatenate([nope, pad(rope), pad(rope_sw)], axis=1)
    return out.reshape(k, 3 * MLA_HEADS * LANE).astype(BF16)


def _rope_tables(pos):
    inv = 1.0 / (ROPE_THETA ** (jnp.arange(0, QK_ROPE, 2, dtype=F32) / QK_ROPE))
    ang = pos.astype(F32)[:, None] * inv[None, :]
    cos, sin = jnp.cos(ang), jnp.sin(ang)
    zeros = jnp.zeros((pos.shape[0], LANE - QK_ROPE), F32)
    return (jnp.concatenate([cos, cos, zeros], axis=1), jnp.concatenate([-sin, sin, zeros], axis=1))


def _band_bias_row(rel_bias, qchunks):
    heads = rel_bias.shape[0]
    span = _band_span(qchunks)
    left = BAND_ROWS + qchunks * CHUNK - 1 - REL_CLIP
    right = max(span - left - (2 * REL_CLIP + 1), 0)
    row = jnp.concatenate([jnp.broadcast_to(rel_bias[:, -1:], (heads, left)), rel_bias[:, ::-1],
                           jnp.broadcast_to(rel_bias[:, :1], (heads, right))], axis=1)
    return row[:, :span]


def kernel(x_prompt, x_sample, cache_latent, cache_k_rope, cache_band_k, cache_band_v, state_conv, state_delta, norm_ff1, ff1_w1, ff1_w3, ff1_w2, norm_mix, w_in, q_norm, w_qb, kv_norm, w_uk, w_uv, rel_bias, conv_w, a_log, dt_bias, gdn_norm, w_out, norm_ff2, ff2_w1, ff2_w3, ff2_w2, final_norm):
    batch, seq, d = x_prompt.shape
    dbatch, dseq, _ = x_sample.shape
    depth = w_in.shape[0]
    past = cache_latent.shape[2]
    assert dseq == CHUNK and seq % CHUNK == 0 and cache_band_k.shape[2] == BAND_ROWS
    mp, ms = batch * seq, dbatch * dseq
    m = mp + ms
    nchunks = seq // CHUNK
    blk_s = mp // CHUNK
    band_w = BAND_HEADS * BAND_DIM

    x = jnp.concatenate([x_prompt.reshape(mp, d), x_sample.reshape(ms, d)], axis=0)
    pos = jnp.concatenate([jnp.tile(jnp.arange(seq, dtype=jnp.int32), batch),
                           jnp.tile(past + jnp.arange(dseq, dtype=jnp.int32), dbatch)])
    cos128, sin128 = _rope_tables(pos)
    zero_state = jnp.zeros((batch, GDN_HEADS, GDN_DK, GDN_DV), F32)

    def ffn(x, norm, w1, w3, w2, l):
        h = rmsnorm(x, norm[l], BF16)
        hid = matmul_swiglu(h, w1, w3, l, 544, 512)
        return matmul_residual(hid, w2, l, x, 0.5, 272, 512)

    outs = [[] for _ in range(12)]
    for l in range(depth):
        x = ffn(x, norm_ff1, ff1_w1, ff1_w3, ff1_w2, l)

        h = rmsnorm(x, norm_mix[l], BF16)
        cols = matmul(h, _layout_w_in(w_in[l]), F32, 544, 1024)

        wuk_t = jnp.transpose(w_uk[l], (1, 2, 0)).astype(BF16)
        wuv_t = jnp.transpose(w_uv[l], (1, 0, 2)).astype(BF16)
        q_lat, q_rope, lat, krope, klat_bf, krope_bf = mla_prep(
            cols, q_norm[l], kv_norm[l], _layout_w_qb(w_qb[l]), wuk_t, cos128, sin128)
        ya_p = mla_attention_prompt(q_lat, q_rope, klat_bf, krope_bf, wuv_t, batch, seq)
        ya_s = mla_attention_sample(q_lat, q_rope, cache_latent[l], cache_k_rope[l], klat_bf,
                                    krope_bf, wuv_t, dbatch, blk_s)

        band_k = cols[:, COL_BK * SLOT:(COL_BK + 1) * SLOT]
        band_v = cols[:, COL_BV * SLOT:(COL_BV + 1) * SLOT]

        def prompt_rows(t):
            t = t[:mp].reshape(batch, seq, band_w).astype(BF16)
            return jnp.pad(t, ((0, 0), (BAND_ROWS, 0), (0, 0)))

        def sample_rows(t, cache):
            return jnp.concatenate([cache.reshape(dbatch, BAND_ROWS, band_w).astype(BF16),
                                    t[mp:].reshape(dbatch, dseq, band_w).astype(BF16)], axis=1)

        yb_p = band_attention(cols, prompt_rows(band_k), prompt_rows(band_v), rel_bias[l],
                              batch, nchunks, 0, BAND_ROWS, math.gcd(nchunks, 4))
        yb_s = band_attention(cols, sample_rows(band_k, cache_band_k[l]),
                              sample_rows(band_v, cache_band_v[l]), rel_bias[l], dbatch, 1, blk_s, 0, 1)

        conv_init = jnp.concatenate(
            [jnp.zeros((batch, 8, C_CONV), F32),
             jnp.pad(state_conv[l], ((0, 0), (8 - (CONV_W - 1), 0), (0, 0)))], axis=0)
        aneg = _pad_cols(-jnp.exp(a_log[l].astype(F32)).reshape(1, -1), LANE)
        dtb = _pad_cols(dt_bias[l].astype(F32).reshape(1, -1), LANE)
        pre = gdn_pre(cols, conv_init, conv_w[l], aneg, dtb, nchunks, mp // CHUNK)
        yc_p, delta_p = gdn_recurrence(pre, cols, zero_state, gdn_norm[l], batch, nchunks, 0)
        yc_s, delta_s = gdn_recurrence(pre, cols, state_delta[l], gdn_norm[l], dbatch, 1, blk_s)

        x = matmul_mix(jnp.concatenate([ya_p, ya_s], axis=0), jnp.concatenate([yb_p, yb_s], axis=0),
                       jnp.concatenate([yc_p, yc_s], axis=0), w_out, l, x, 544, 512)
        x = ffn(x, norm_ff2, ff2_w1, ff2_w3, ff2_w2, l)

        u_cols = cols[:, :C_CONV]
        keep = min(BAND_ROWS, seq)
        layer_out = (
            lat[:mp].reshape(batch, seq, KV_LORA),
            krope[:mp].reshape(batch, seq, QK_ROPE),
            band_k[:mp].reshape(batch, seq, BAND_HEADS, BAND_DIM)[:, seq - keep:],
            band_v[:mp].reshape(batch, seq, BAND_HEADS, BAND_DIM)[:, seq - keep:],
            u_cols[:mp].reshape(batch, seq, C_CONV)[:, seq - (CONV_W - 1):],
            delta_p,
            lat[mp:].reshape(dbatch, dseq, KV_LORA),
            krope[mp:].reshape(dbatch, dseq, QK_ROPE),
            band_k[mp:].reshape(dbatch, dseq, BAND_HEADS, BAND_DIM),
            band_v[mp:].reshape(dbatch, dseq, BAND_HEADS, BAND_DIM),
            u_cols[mp:].reshape(dbatch, dseq, C_CONV)[:, dseq - (CONV_W - 1):],
            delta_s,
        )
        for acc, t in zip(outs, layer_out):
            acc.append(t)

    y_prompt = rmsnorm(x, final_norm, F32, rows=mp).reshape(batch, seq, d)
    y_sample = rmsnorm(x, final_norm, F32, rows=ms, row_start=mp).reshape(dbatch, dseq, d)
    return (y_prompt, y_sample) + tuple(jnp.stack(t) for t in outs)
```

```python
import functools
import math

import jax
import jax.numpy as jnp
from jax import lax
from jax.experimental import pallas as pl
from jax.experimental.pallas import tpu as pltpu

F32 = jnp.float32
BF16 = jnp.bfloat16
HIGHEST = lax.Precision.HIGHEST

CHUNK = 64
CHUNK_SHIFT = 6
EPS = 1e-6
NEG = -1e30
Q_LORA = 1024
KV_LORA = 512
QK_NOPE = 128
QK_ROPE = 64
V_HEAD = 128
MLA_HEADS = 16
ROPE_THETA = 10000.0
MLA_SCALE = (QK_NOPE + QK_ROPE) ** -0.5
MLA_HEAD_GROUPS = 8
BAND_HEADS = 8
BAND_DIM = 128
BAND_PREV = 8
BAND_ROWS = BAND_PREV * CHUNK
BAND_KEYS = BAND_ROWS + CHUNK
REL_CLIP = 128
BAND_SCALE = BAND_DIM ** -0.5
GDN_HEADS = 8
GDN_GROUP = 8
GDN_DK = 128
GDN_DV = 128
CONV_W = 4
GDN_QK = GDN_HEADS * GDN_DK
GDN_V = GDN_HEADS * GDN_DV
C_CONV = 2 * GDN_QK + GDN_V
LANE = 128
SLOT = 1024
COL_U, COL_QA, COL_BQ, COL_BK, COL_BV, COL_Z, COL_TAIL = 0, 3, 4, 5, 6, 7, 8
IN_COLS_PAD = 9 * SLOT
TAIL_KR, TAIL_KSW, TAIL_A, TAIL_B = 512, 640, 768, 896

VMEM_LIMIT_BYTES = 56 * 1024 * 1024


def _params(*sem):
    return pltpu.CompilerParams(dimension_semantics=sem, vmem_limit_bytes=VMEM_LIMIT_BYTES)


def _tile(n, target, mult=16):
    best = None
    for t in range(mult, min(n, target) + 1, mult):
        if n % t == 0:
            best = t
    assert best is not None, (n, target)
    return best


def _dot(a, b, precision=None):
    return jnp.dot(a, b, preferred_element_type=F32, precision=precision)


def _dot_nt(a, b, precision=None):
    return lax.dot_general(a, b, (((1,), (1,)), ((), ())), preferred_element_type=F32,
                           precision=precision)


def _dot_tn(a, b, precision=None):
    return lax.dot_general(a, b, (((0,), (0,)), ((), ())), preferred_element_type=F32,
                           precision=precision)


def _split_bf16(x):
    hi = x.astype(BF16)
    return hi, (x - hi.astype(F32)).astype(BF16)


def _dot_3pass(a, b, expand=lambda t: t):
    a_hi, a_lo = _split_bf16(a)
    b_hi, b_lo = _split_bf16(b)
    m = a.shape[0]
    top = _dot(jnp.concatenate([a_hi, a_lo], axis=0), expand(b_hi))
    return top[:m] + top[m:] + _dot(a_hi, expand(b_lo))


def _sigmoid(x):
    return 1.0 / (1.0 + jnp.exp(-x))


def _rms(x, g):
    return x * lax.rsqrt(jnp.mean(x * x, axis=-1, keepdims=True) + EPS) * g


def _rmsnorm_kernel(x_ref, g_ref, o_ref):
    o_ref[...] = _rms(x_ref[...], g_ref[...]).astype(o_ref.dtype)


def rmsnorm(x, g, out_dtype, rows=None, row_start=0):
    m, d = x.shape
    rows = m if rows is None else rows
    tm = _tile(math.gcd(rows, row_start) if row_start else rows, 256, 8)
    off = row_start // tm
    return pl.pallas_call(
        _rmsnorm_kernel,
        grid=(rows // tm,),
        in_specs=[pl.BlockSpec((tm, d), lambda i: (i + off, 0)),
                  pl.BlockSpec((1, d), lambda i: (0, 0))],
        out_specs=pl.BlockSpec((tm, d), lambda i: (i, 0)),
        out_shape=jax.ShapeDtypeStruct((rows, d), out_dtype),
        compiler_params=_params("parallel"),
        name="rmsnorm",
    )(x, g.reshape(1, d))


def _pair_specs(first, second, tm, width, row_axis, col_fn=None):
    n_first = first.shape[0] // tm
    col = (lambda *g: 0) if col_fn is None else col_fn
    return (n_first,
            pl.BlockSpec((tm, width), lambda *g: (jnp.minimum(g[row_axis], n_first - 1), col(*g))),
            pl.BlockSpec((tm, width), lambda *g: (jnp.maximum(g[row_axis] - n_first, 0), col(*g))))


def _rmsnorm_pair_kernel(xp_ref, xs_ref, g_ref, o_ref, *, n_first):
    x = jnp.where(pl.program_id(0) < n_first, xp_ref[...], xs_ref[...])
    o_ref[...] = _rms(x, g_ref[...]).astype(o_ref.dtype)


def rmsnorm_pair(xp, xs, g, out_dtype):
    d = xp.shape[1]
    tm = _tile(math.gcd(xp.shape[0], xs.shape[0]), 256, 8)
    n_first, spec_p, spec_s = _pair_specs(xp, xs, tm, d, 0)
    m = xp.shape[0] + xs.shape[0]
    return pl.pallas_call(
        functools.partial(_rmsnorm_pair_kernel, n_first=n_first),
        grid=(m // tm,),
        in_specs=[spec_p, spec_s, pl.BlockSpec((1, d), lambda i: (0, 0))],
        out_specs=pl.BlockSpec((tm, d), lambda i: (i, 0)),
        out_shape=jax.ShapeDtypeStruct((m, d), out_dtype),
        compiler_params=_params("parallel"),
        name="rmsnorm_pair",
    )(xp, xs, g.reshape(1, d))


def _mm_kernel(a_ref, w_ref, o_ref):
    o_ref[...] = _dot(a_ref[...], w_ref[...]).astype(o_ref.dtype)


def _cast_weight(w_ref, wb_ref):
    @pl.when(pl.program_id(1) == 0)
    def _():
        wb_ref[...] = w_ref[...].astype(BF16)


def _mm_res_kernel(a_ref, w_ref, r_ref, o_ref, wb_ref, *, scale):
    _cast_weight(w_ref, wb_ref)
    o_ref[...] = r_ref[...] + scale * _dot(a_ref[...], wb_ref[...])


def _mm_res_pair_kernel(a_ref, w_ref, rp_ref, rs_ref, o_ref, wb_ref, *, scale, n_first):
    _cast_weight(w_ref, wb_ref)
    res = jnp.where(pl.program_id(1) < n_first, rp_ref[...], rs_ref[...])
    o_ref[...] = res + scale * _dot(a_ref[...], wb_ref[...])


def _mm_mix_kernel(ya_ref, yb_ref, yc_ref, w_ref, r_ref, o_ref, wb_ref):
    _cast_weight(w_ref, wb_ref)
    ka, kb = ya_ref.shape[1], yb_ref.shape[1]
    o_ref[...] = (r_ref[...] + _dot(ya_ref[...], wb_ref[0:ka, :])
                  + _dot(yb_ref[...], wb_ref[ka:ka + kb, :]) + _dot(yc_ref[...], wb_ref[ka + kb:, :]))


def _mm_swiglu_kernel(a_ref, w1_ref, w3_ref, o_ref, w1b_ref, w3b_ref):
    _cast_weight(w1_ref, w1b_ref)
    _cast_weight(w3_ref, w3b_ref)
    a = a_ref[...]
    h1 = _dot(a, w1b_ref[...])
    h3 = _dot(a, w3b_ref[...])
    o_ref[...] = (h1 * _sigmoid(h1) * h3).astype(o_ref.dtype)


def matmul(a, w, out_dtype, tm_target, tn_target):
    m, k = a.shape
    n = w.shape[1]
    tm, tn = _tile(m, tm_target), _tile(n, tn_target, LANE)
    return pl.pallas_call(
        _mm_kernel,
        grid=(n // tn, m // tm),
        in_specs=[pl.BlockSpec((tm, k), lambda j, i: (i, 0)),
                  pl.BlockSpec((k, tn), lambda j, i: (0, j))],
        out_specs=pl.BlockSpec((tm, tn), lambda j, i: (i, j)),
        out_shape=jax.ShapeDtypeStruct((m, n), out_dtype),
        compiler_params=_params("parallel", "parallel"),
        name="matmul",
    )(a, w)


def _layer_weight_spec(layer, k, tn):
    return pl.BlockSpec((None, k, tn), lambda j, i: (layer, 0, j))


def matmul_residual(a, w, layer, res, scale, tm_target, tn_target):
    m, k = a.shape
    n = w.shape[2]
    tm, tn = _tile(m, tm_target), _tile(n, tn_target, LANE)
    return pl.pallas_call(
        functools.partial(_mm_res_kernel, scale=scale),
        grid=(n // tn, m // tm),
        in_specs=[pl.BlockSpec((tm, k), lambda j, i: (i, 0)),
                  _layer_weight_spec(layer, k, tn),
                  pl.BlockSpec((tm, tn), lambda j, i: (i, j))],
        out_specs=pl.BlockSpec((tm, tn), lambda j, i: (i, j)),
        out_shape=jax.ShapeDtypeStruct((m, n), F32),
        scratch_shapes=[pltpu.VMEM((k, tn), BF16)],
        compiler_params=_params("parallel", "arbitrary"),
        name="matmul_residual",
    )(a, w, res)


def matmul_residual_pair(a, w, layer, res_p, res_s, scale, tn_target):
    m, k = a.shape
    n = w.shape[2]
    tm = _tile(math.gcd(res_p.shape[0], res_s.shape[0]), 256)
    tn = _tile(n, tn_target, LANE)
    n_first, spec_p, spec_s = _pair_specs(res_p, res_s, tm, tn, 1, lambda j, i: j)
    return pl.pallas_call(
        functools.partial(_mm_res_pair_kernel, scale=scale, n_first=n_first),
        grid=(n // tn, m // tm),
        in_specs=[pl.BlockSpec((tm, k), lambda j, i: (i, 0)),
                  _layer_weight_spec(layer, k, tn), spec_p, spec_s],
        out_specs=pl.BlockSpec((tm, tn), lambda j, i: (i, j)),
        out_shape=jax.ShapeDtypeStruct((m, n), F32),
        scratch_shapes=[pltpu.VMEM((k, tn), BF16)],
        compiler_params=_params("parallel", "arbitrary"),
        name="matmul_residual_pair",
    )(a, w, res_p, res_s)


def matmul_mix(ya, yb, yc, w, layer, res, tm_target, tn_target):
    m = ya.shape[0]
    k, n = w.shape[1], w.shape[2]
    tm, tn = _tile(m, tm_target), _tile(n, tn_target, LANE)
    row = lambda y: pl.BlockSpec((tm, y.shape[1]), lambda j, i: (i, 0))
    return pl.pallas_call(
        _mm_mix_kernel,
        grid=(n // tn, m // tm),
        in_specs=[row(ya), row(yb), row(yc), _layer_weight_spec(layer, k, tn),
                  pl.BlockSpec((tm, tn), lambda j, i: (i, j))],
        out_specs=pl.BlockSpec((tm, tn), lambda j, i: (i, j)),
        out_shape=jax.ShapeDtypeStruct((m, n), F32),
        scratch_shapes=[pltpu.VMEM((k, tn), BF16)],
        compiler_params=_params("parallel", "arbitrary"),
        name="matmul_mix",
    )(ya, yb, yc, w, res)


def matmul_swiglu(a, w1, w3, layer, tm_target, tn_target):
    m, k = a.shape
    n = w1.shape[2]
    tm, tn = _tile(m, tm_target), _tile(n, tn_target, LANE)
    return pl.pallas_call(
        _mm_swiglu_kernel,
        grid=(n // tn, m // tm),
        in_specs=[pl.BlockSpec((tm, k), lambda j, i: (i, 0)),
                  _layer_weight_spec(layer, k, tn),
                  _layer_weight_spec(layer, k, tn)],
        out_specs=pl.BlockSpec((tm, tn), lambda j, i: (i, j)),
        out_shape=jax.ShapeDtypeStruct((m, n), BF16),
        scratch_shapes=[pltpu.VMEM((k, tn), BF16), pltpu.VMEM((k, tn), BF16)],
        compiler_params=_params("parallel", "arbitrary"),
        name="matmul_swiglu",
    )(a, w1, w3)


def _mla_prep_kernel(qa_ref, tail_ref, qn_ref, kvn_ref, wq_ref, wuk_ref, cos_ref, sin_ref,
                     qlat_ref, qrope_ref, lat_ref, krope_ref, klat_bf_ref, krope_bf_ref):
    hq = _rms(qa_ref[...], qn_ref[...]).astype(BF16)
    q_all = _dot(hq, wq_ref[...])
    cos = cos_ref[...]
    sin = sin_ref[...]
    hw = MLA_HEADS * LANE
    for h in range(MLA_HEADS):
        qn = q_all[:, h * LANE:(h + 1) * LANE].astype(BF16)
        qlat_ref[h] = (_dot(qn, wuk_ref[h]) * MLA_SCALE).astype(BF16)
        qr = (q_all[:, hw + h * LANE:hw + (h + 1) * LANE] * cos
              + q_all[:, 2 * hw + h * LANE:2 * hw + (h + 1) * LANE] * sin)
        qrope_ref[h] = (qr[:, :QK_ROPE] * MLA_SCALE).astype(BF16)
    tail = tail_ref[...]
    lat = _rms(tail[:, :KV_LORA], kvn_ref[...])
    lat_ref[...] = lat
    klat_bf_ref[...] = lat.astype(BF16)
    kr = tail[:, TAIL_KR:TAIL_KR + LANE] * cos + tail[:, TAIL_KSW:TAIL_KSW + LANE] * sin
    krope_ref[...] = kr[:, :QK_ROPE]
    krope_bf_ref[...] = kr[:, :QK_ROPE].astype(BF16)


def mla_prep(cols, q_norm, kv_norm, wq_all, wuk_t, cos128, sin128):
    m = cols.shape[0]
    tm = _tile(m, 256)
    h = MLA_HEADS
    const2 = lambda i: (0, 0)
    return pl.pallas_call(
        _mla_prep_kernel,
        grid=(m // tm,),
        in_specs=[pl.BlockSpec((tm, SLOT), lambda i: (i, COL_QA)),
                  pl.BlockSpec((tm, SLOT), lambda i: (i, COL_TAIL)),
                  pl.BlockSpec((1, Q_LORA), const2),
                  pl.BlockSpec((1, KV_LORA), const2),
                  pl.BlockSpec(wq_all.shape, const2, pipeline_mode=pl.Buffered(1)),
                  pl.BlockSpec(wuk_t.shape, lambda i: (0, 0, 0), pipeline_mode=pl.Buffered(1)),
                  pl.BlockSpec((tm, LANE), lambda i: (i, 0)),
                  pl.BlockSpec((tm, LANE), lambda i: (i, 0))],
        out_specs=[pl.BlockSpec((h, tm, KV_LORA), lambda i: (0, i, 0)),
                   pl.BlockSpec((h, tm, QK_ROPE), lambda i: (0, i, 0)),
                   pl.BlockSpec((tm, KV_LORA), lambda i: (i, 0)),
                   pl.BlockSpec((tm, QK_ROPE), lambda i: (i, 0)),
                   pl.BlockSpec((tm, KV_LORA), lambda i: (i, 0)),
                   pl.BlockSpec((tm, QK_ROPE), lambda i: (i, 0))],
        out_shape=[jax.ShapeDtypeStruct((h, m, KV_LORA), BF16),
                   jax.ShapeDtypeStruct((h, m, QK_ROPE), BF16),
                   jax.ShapeDtypeStruct((m, KV_LORA), F32),
                   jax.ShapeDtypeStruct((m, QK_ROPE), F32),
                   jax.ShapeDtypeStruct((m, KV_LORA), BF16),
                   jax.ShapeDtypeStruct((m, QK_ROPE), BF16)],
        compiler_params=_params("parallel"),
        name="mla_prep",
    )(cols, cols, q_norm.reshape(1, -1), kv_norm.reshape(1, -1), wq_all, wuk_t, cos128, sin128)


def _softmax_init(m_ref, l_ref, acc_ref):
    m_ref[...] = jnp.full(m_ref.shape, NEG, F32)
    l_ref[...] = jnp.zeros(l_ref.shape, F32)
    acc_ref[...] = jnp.zeros(acc_ref.shape, F32)


def _mla_step(ql_ref, qr_ref, k_l, k_r, m_ref, l_ref, acc_ref, tq, visible=None):
    hg = MLA_HEADS // MLA_HEAD_GROUPS
    rows = hg * tq

    def scores(g):
        q_l = ql_ref[g * hg:(g + 1) * hg].reshape(rows, KV_LORA)
        q_r = qr_ref[g * hg:(g + 1) * hg].reshape(rows, QK_ROPE)
        s = _dot_nt(q_l, k_l) + _dot_nt(q_r, k_r)
        return s if visible is None else jnp.where(visible, s, NEG)

    def update(g, s):
        rs = slice(g * rows, (g + 1) * rows)
        m_old = m_ref[rs]
        m_new = jnp.maximum(m_old, jnp.max(s, axis=-1, keepdims=True))
        alpha = jnp.exp(m_old - m_new)
        p = jnp.exp(s - m_new)
        l_ref[rs] = alpha * l_ref[rs] + jnp.sum(p, axis=-1, keepdims=True)
        acc_ref[rs] = alpha * acc_ref[rs] + _dot(p.astype(BF16), k_l)
        m_ref[rs] = m_new

    s = scores(0)
    for g in range(1, MLA_HEAD_GROUPS):
        s_next = scores(g)
        update(g - 1, s)
        s = s_next
    update(MLA_HEAD_GROUPS - 1, s)


def _mla_finish(wuv_ref, o_ref, l_ref, acc_ref, tq):
    ctx = acc_ref[...] / l_ref[...]
    for h in range(MLA_HEADS):
        ch = ctx[h * tq:(h + 1) * tq].astype(BF16)
        o_ref[:, h * V_HEAD:(h + 1) * V_HEAD] = _dot(ch, wuv_ref[h]).astype(o_ref.dtype)


def _mla_prompt_kernel(ql_ref, qr_ref, kl_ref, kr_ref, wuv_ref, y_prev_ref, o_ref,
                       m_ref, l_ref, acc_ref, *, tq, tk):
    del y_prev_ref
    qi = pl.program_id(1)
    kj = pl.program_id(2)
    last = (qi * tq + tq - 1) // tk

    @pl.when(kj == 0)
    def _():
        _softmax_init(m_ref, l_ref, acc_ref)

    @pl.when(kj < last)
    def _():
        _mla_step(ql_ref, qr_ref, kl_ref[...], kr_ref[...], m_ref, l_ref, acc_ref, tq)

    @pl.when(kj == last)
    def _():
        shape = (MLA_HEADS // MLA_HEAD_GROUPS * tq, tk)
        row = lax.broadcasted_iota(jnp.int32, shape, 0)
        col = lax.broadcasted_iota(jnp.int32, shape, 1)
        q_chunk = (qi * tq + row % tq) // CHUNK
        k_chunk = (kj * tk + col) // CHUNK
        _mla_step(ql_ref, qr_ref, kl_ref[...], kr_ref[...], m_ref, l_ref, acc_ref, tq,
                  visible=k_chunk <= q_chunk)

    @pl.when(kj == pl.num_programs(2) - 1)
    def _():
        _mla_finish(wuv_ref, o_ref, l_ref, acc_ref, tq)


def mla_attention_prompt(q_lat, q_rope, k_lat, k_rope, wuv, y_prev, batch, seq):
    tq = 128
    tk = _tile(seq, 512, CHUNK)
    nq, nk = seq // tq, seq // tk
    h = MLA_HEADS
    rows = h * tq

    def k_map(b, qi, kj):
        return (b * nk + jnp.minimum(kj, (qi * tq + tq - 1) // tk), 0)

    return pl.pallas_call(
        functools.partial(_mla_prompt_kernel, tq=tq, tk=tk),
        grid=(batch, nq, nk),
        in_specs=[pl.BlockSpec((h, tq, KV_LORA), lambda b, qi, kj: (0, b * nq + qi, 0)),
                  pl.BlockSpec((h, tq, QK_ROPE), lambda b, qi, kj: (0, b * nq + qi, 0)),
                  pl.BlockSpec((tk, KV_LORA), k_map),
                  pl.BlockSpec((tk, QK_ROPE), k_map),
                  pl.BlockSpec(wuv.shape, lambda b, qi, kj: (0, 0, 0)),
                  pl.BlockSpec(memory_space=pl.ANY)],
        out_specs=pl.BlockSpec((tq, h * V_HEAD), lambda b, qi, kj: (b * nq + qi, 0)),
        out_shape=jax.ShapeDtypeStruct(y_prev.shape, y_prev.dtype),
        input_output_aliases={5: 0},
        scratch_shapes=[pltpu.VMEM((rows, 1), F32), pltpu.VMEM((rows, 1), F32),
                        pltpu.VMEM((rows, KV_LORA), F32)],
        compiler_params=_params("parallel", "parallel", "arbitrary"),
        name="mla_attention_prompt",
    )(q_lat, q_rope, k_lat, k_rope, wuv, y_prev)


def _mla_sample_kernel(ql_ref, qr_ref, pl_ref, pr_ref, kl_ref, kr_ref, wuv_ref, y_prev_ref, o_ref,
                       m_ref, l_ref, acc_ref):
    del y_prev_ref
    kj = pl.program_id(1)
    n_past = pl.num_programs(1) - 1

    @pl.when(kj == 0)
    def _():
        _softmax_init(m_ref, l_ref, acc_ref)

    @pl.when(kj < n_past)
    def _():
        _mla_step(ql_ref, qr_ref, pl_ref[0].astype(BF16), pr_ref[0].astype(BF16),
                  m_ref, l_ref, acc_ref, CHUNK)

    @pl.when(kj == n_past)
    def _():
        _mla_step(ql_ref, qr_ref, kl_ref[...], kr_ref[...], m_ref, l_ref, acc_ref, CHUNK)
        _mla_finish(wuv_ref, o_ref, l_ref, acc_ref, CHUNK)


def mla_attention_sample(q_lat, q_rope, past_lat, past_rope, layer, k_lat, k_rope, wuv, y_prev,
                         batch, tok_block0):
    past = past_lat.shape[2]
    tk = _tile(past, 512, CHUNK)
    n_past = past // tk
    h = MLA_HEADS
    rows = h * CHUNK
    past_map = lambda b, kj: (layer, b, jnp.minimum(kj, n_past - 1), 0)
    return pl.pallas_call(
        _mla_sample_kernel,
        grid=(batch, n_past + 1),
        in_specs=[pl.BlockSpec((h, CHUNK, KV_LORA), lambda b, kj: (0, tok_block0 + b, 0)),
                  pl.BlockSpec((h, CHUNK, QK_ROPE), lambda b, kj: (0, tok_block0 + b, 0)),
                  pl.BlockSpec((None, 1, tk, KV_LORA), past_map),
                  pl.BlockSpec((None, 1, tk, QK_ROPE), past_map),
                  pl.BlockSpec((CHUNK, KV_LORA), lambda b, kj: (tok_block0 + b, 0)),
                  pl.BlockSpec((CHUNK, QK_ROPE), lambda b, kj: (tok_block0 + b, 0)),
                  pl.BlockSpec(wuv.shape, lambda b, kj: (0, 0, 0)),
                  pl.BlockSpec(memory_space=pl.ANY)],
        out_specs=pl.BlockSpec((CHUNK, h * V_HEAD), lambda b, kj: (tok_block0 + b, 0)),
        out_shape=jax.ShapeDtypeStruct(y_prev.shape, y_prev.dtype),
        input_output_aliases={7: 0},
        scratch_shapes=[pltpu.VMEM((rows, 1), F32), pltpu.VMEM((rows, 1), F32),
                        pltpu.VMEM((rows, KV_LORA), F32)],
        compiler_params=_params("parallel", "arbitrary"),
        name="mla_attention_sample",
    )(q_lat, q_rope, past_lat, past_rope, k_lat, k_rope, wuv, y_prev)


def _band_span(qchunks):
    tq, nk = qchunks * CHUNK, (qchunks + BAND_PREV) * CHUNK
    return -(-(tq + nk - 1) // LANE) * LANE


def _band_kernel(q_ref, k_ref, v_ref, g_ref, y_prev_ref, o_ref, bias_ref, *, pad_rows, qchunks):
    del y_prev_ref
    n = pl.program_id(1)
    tq, nk = qchunks * CHUNK, (qchunks + BAND_PREV) * CHUNK
    span = _band_span(qchunks)

    @pl.when(jnp.logical_and(pl.program_id(0) == 0, n == 0))
    def _():
        for h in range(BAND_HEADS):
            rows = jnp.broadcast_to(g_ref[h:h + 1, :], (tq, span))
            rolled = pltpu.roll(rows, span - (tq - 1), 1, stride=1, stride_axis=0)
            bias_ref[h] = rolled[:, :nk]

    start = pl.multiple_of(n * tq, CHUNK)
    row = lax.broadcasted_iota(jnp.int32, (tq, nk), 0)
    col = lax.broadcasted_iota(jnp.int32, (tq, nk), 1)
    visible = start + col >= pad_rows
    if qchunks > 1:
        ahead = (col >> CHUNK_SHIFT) - (row >> CHUNK_SHIFT)
        visible = visible & (ahead >= 0) & (ahead <= BAND_PREV)
    q = q_ref[...]
    for h in range(BAND_HEADS):
        lanes = slice(h * BAND_DIM, (h + 1) * BAND_DIM)
        qh = q[:, lanes].astype(BF16)
        kh = k_ref[0, pl.ds(start, nk), lanes]
        vh = v_ref[0, pl.ds(start, nk), lanes]
        s = _dot_nt(qh, kh) * BAND_SCALE + bias_ref[h]
        if pad_rows or qchunks > 1:
            s = jnp.where(visible, s, NEG)
        p = jnp.exp(s - jnp.max(s, axis=-1, keepdims=True))
        o = _dot(p.astype(BF16), vh) / jnp.sum(p, axis=-1, keepdims=True)
        o_ref[:, lanes] = o.astype(o_ref.dtype)


def band_attention(cols, k_rows, v_rows, rel_bias, y_prev, batch, nchunks, tok_block0, pad_rows, qchunks):
    assert nchunks % qchunks == 0 and tok_block0 % qchunks == 0
    width = BAND_HEADS * BAND_DIM
    rows = k_rows.shape[1]
    tq, nk = qchunks * CHUNK, (qchunks + BAND_PREV) * CHUNK
    nblk = nchunks // qchunks
    bias_row = _band_bias_row(rel_bias, qchunks)
    tok = lambda b, n: tok_block0 // qchunks + b * nblk + n
    return pl.pallas_call(
        functools.partial(_band_kernel, pad_rows=pad_rows, qchunks=qchunks),
        grid=(batch, nblk),
        in_specs=[pl.BlockSpec((tq, width), lambda b, n: (tok(b, n), COL_BQ)),
                  pl.BlockSpec((1, rows, width), lambda b, n: (b, 0, 0)),
                  pl.BlockSpec((1, rows, width), lambda b, n: (b, 0, 0)),
                  pl.BlockSpec(bias_row.shape, lambda b, n: (0, 0)),
                  pl.BlockSpec(memory_space=pl.ANY)],
        out_specs=pl.BlockSpec((tq, width), lambda b, n: (tok(b, n), 0)),
        out_shape=jax.ShapeDtypeStruct(y_prev.shape, y_prev.dtype),
        input_output_aliases={4: 0},
        scratch_shapes=[pltpu.VMEM((BAND_HEADS, tq, nk), F32)],
        compiler_params=_params("arbitrary", "arbitrary"),
        name="band_attention",
    )(cols, k_rows, v_rows, bias_row, y_prev)


def _expand_matrix(rows, first, width, per_shift):
    r = lax.broadcasted_iota(jnp.int32, (rows, width), 0)
    c = lax.broadcasted_iota(jnp.int32, (rows, width), 1)
    return (r == first + (c >> per_shift)).astype(F32)


def _unit_lower_inverse(a, row, col, group):
    gl = a.shape[1]
    r = lax.broadcasted_iota(jnp.int32, (gl, gl), 0)
    c = lax.broadcasted_iota(jnp.int32, (gl, gl), 1)
    diag_blocks = ((r >> CHUNK_SHIFT) == (c >> CHUNK_SHIFT)).astype(BF16)

    def block_diag(w):
        return jnp.concatenate([w] * group, axis=0) * diag_blocks

    t = (row == col).astype(F32) - jnp.where((row >> 1) == (col >> 1), a, 0.0)
    for shift in range(1, CHUNK_SHIFT):
        lower_left = ((row >> (shift + 1)) == (col >> (shift + 1))) & ((row >> shift) != (col >> shift))
        x = _dot_3pass(t, jnp.where(lower_left, a, 0.0), block_diag)
        t = t - _dot_3pass(x, t, block_diag)
    return t


def _l2norm_heads(x, heads, width):
    parts = []
    for h in range(heads):
        xh = x[:, h * width:(h + 1) * width]
        parts.append(xh * lax.rsqrt(jnp.sum(xh * xh, axis=-1, keepdims=True) + EPS))
    return jnp.concatenate(parts, axis=1)


def _gdn_pre_kernel(u_ref, uprev_ref, init_ref, tail_ref, cw_ref, aneg_ref, dtb_ref,
                    uv_ref, wk_ref, attn_ref, qg_ref, kt_ref, gl_ref, ext_ref,
                    *, chunks_per_seq, n_prompt_chunks):
    group = GDN_GROUP
    gw = group * CHUNK
    c = pl.program_id(0)
    is_start = jnp.logical_or(c >= n_prompt_chunks, c % chunks_per_seq == 0)
    ext_ref[0:8, :] = jnp.where(is_start, init_ref[0], uprev_ref[...])
    ext_ref[8:8 + CHUNK, :] = u_ref[...]
    cw = cw_ref[...]
    uc = (cw[3:4] * ext_ref[8:8 + CHUNK, :] + cw[2:3] * ext_ref[7:7 + CHUNK, :]
          + cw[1:2] * ext_ref[6:6 + CHUNK, :] + cw[0:1] * ext_ref[5:5 + CHUNK, :])
    uc = uc * _sigmoid(uc)

    tail = tail_ref[...]
    xa = tail[:, TAIL_A:TAIL_A + LANE] + dtb_ref[...]
    softplus = jnp.maximum(xa, 0.0) + jnp.log(1.0 + jnp.exp(-jnp.abs(xa)))
    g = aneg_ref[...] * softplus
    beta = _sigmoid(tail[:, TAIL_B:TAIL_B + LANE])

    r64 = lax.broadcasted_iota(jnp.int32, (CHUNK, CHUNK), 0)
    c64 = lax.broadcasted_iota(jnp.int32, (CHUNK, CHUNK), 1)
    gc = _dot((r64 >= c64).astype(F32), g, HIGHEST)
    egc = jnp.exp(gc)
    ekt = jnp.exp(gc[CHUNK - 1:CHUNK, :] - gc)
    e128 = _expand_matrix(LANE, 0, GDN_HEADS * GDN_DK, 7)
    beta_x = _dot(beta, e128, HIGHEST)
    egc_x = _dot(egc, e128, HIGHEST)
    ekt_x = _dot(ekt, e128, HIGHEST)
    gl_ref[0] = egc_x[CHUNK - 1:CHUNK, :]

    qn = _l2norm_heads(uc[:, :GDN_QK], GDN_HEADS, GDN_DK) * (GDN_DK ** -0.5)
    kn = _l2norm_heads(uc[:, GDN_QK:2 * GDN_QK], GDN_HEADS, GDN_DK)
    v_all = uc[:, 2 * GDN_QK:]
    qg_ref[...] = (qn * egc_x).astype(BF16)
    kt_ref[...] = (kn * ekt_x).astype(BF16)
    rhs_v = beta_x * v_all
    rhs_k = beta_x * egc_x * kn

    row = lax.broadcasted_iota(jnp.int32, (CHUNK, gw), 0)
    col = lax.broadcasted_iota(jnp.int32, (CHUNK, gw), 1) & (CHUNK - 1)
    incl = row >= col
    head_r = lax.broadcasted_iota(jnp.int32, (gw, group * GDN_DK), 0) >> CHUNK_SHIFT
    head_c = lax.broadcasted_iota(jnp.int32, (gw, group * GDN_DK), 1) >> 7
    for grp in range(GDN_HEADS // group):
        lanes = slice(grp * group * GDN_DK, (grp + 1) * group * GDN_DK)
        e64 = _expand_matrix(LANE, grp * group, gw, CHUNK_SHIFT)
        gcw = _dot(gc, e64, HIGHEST)
        bw = _dot(beta, e64, HIGHEST)
        gc_t = jnp.sum(jnp.where(row == col, gcw, 0.0), axis=0, keepdims=True)
        decay = jnp.where(incl, jnp.exp(jnp.where(incl, gcw - gc_t, 0.0)), 0.0)
        kg = kn[:, lanes]
        k_rows = jnp.where(head_r == head_c, jnp.concatenate([kg] * group, axis=0), 0.0).astype(BF16)
        kk = _dot_nt(kg.astype(BF16), k_rows)
        qk = _dot_nt(qn[:, lanes].astype(BF16), k_rows)
        a_mat = jnp.where(row > col, bw * kk * decay, 0.0)
        t_inv = _unit_lower_inverse(a_mat, row, col, group)
        attn_ref[:, grp * gw:(grp + 1) * gw] = (qk * decay).astype(BF16)
        for hh in range(group):
            h = grp * group + hh
            hl = slice(h * GDN_DK, (h + 1) * GDN_DK)
            rhs = jnp.concatenate([rhs_v[:, hl], rhs_k[:, hl]], axis=1)
            sol = _dot_3pass(t_inv[:, hh * CHUNK:(hh + 1) * CHUNK], rhs)
            uv_ref[:, hl] = sol[:, :GDN_DV]
            wk_ref[:, hl] = sol[:, GDN_DV:].astype(BF16)


def gdn_pre(cols, conv_init, conv_w, aneg, dtb, chunks_per_seq, n_prompt_chunks):
    m = cols.shape[0]
    nch = m // CHUNK
    h = GDN_HEADS

    def init_map(c):
        return (jnp.where(c < n_prompt_chunks, c // chunks_per_seq,
                          n_prompt_chunks // chunks_per_seq + c - n_prompt_chunks), 0, 0)

    tok_spec = lambda d: pl.BlockSpec((CHUNK, d), lambda c: (c, 0))
    return pl.pallas_call(
        functools.partial(_gdn_pre_kernel, chunks_per_seq=chunks_per_seq,
                          n_prompt_chunks=n_prompt_chunks),
        grid=(nch,),
        in_specs=[pl.BlockSpec((CHUNK, C_CONV), lambda c: (c, COL_U)),
                  pl.BlockSpec((8, C_CONV), lambda c: (jnp.maximum(c * (CHUNK // 8) - 1, 0), COL_U)),
                  pl.BlockSpec((1, 8, C_CONV), init_map),
                  pl.BlockSpec((CHUNK, SLOT), lambda c: (c, COL_TAIL)),
                  pl.BlockSpec((CONV_W, C_CONV), lambda c: (0, 0)),
                  pl.BlockSpec((1, LANE), lambda c: (0, 0)),
                  pl.BlockSpec((1, LANE), lambda c: (0, 0))],
        out_specs=[tok_spec(GDN_V), tok_spec(GDN_QK), tok_spec(h * CHUNK), tok_spec(GDN_QK),
                   tok_spec(GDN_QK), pl.BlockSpec((1, 1, GDN_QK), lambda c: (c, 0, 0))],
        out_shape=[jax.ShapeDtypeStruct((m, GDN_V), F32),
                   jax.ShapeDtypeStruct((m, GDN_QK), BF16),
                   jax.ShapeDtypeStruct((m, h * CHUNK), BF16),
                   jax.ShapeDtypeStruct((m, GDN_QK), BF16),
                   jax.ShapeDtypeStruct((m, GDN_QK), BF16),
                   jax.ShapeDtypeStruct((nch, 1, GDN_QK), F32)],
        scratch_shapes=[pltpu.VMEM((8 + CHUNK, C_CONV), F32)],
        compiler_params=_params("parallel"),
        name="gdn_pre",
    )(cols, cols, conv_init, cols, conv_w, aneg, dtb)


def _gdn_rec_kernel(uv_ref, wk_ref, attn_ref, qg_ref, kt_ref, gl_ref, s0_ref, z_ref, gn_ref,
                    y_prev_ref, y_ref, s_ref):
    del y_prev_ref

    @pl.when(pl.program_id(1) == 0)
    def _():
        s_ref[...] = s0_ref[...]

    z = z_ref[...]
    gn = gn_ref[...]
    for h in range(GDN_HEADS):
        hl = slice(h * GDN_DK, (h + 1) * GDN_DK)
        s = s_ref[0, h]
        sb = s.astype(BF16)
        u = uv_ref[:, hl] - _dot(wk_ref[:, hl], sb)
        ub = u.astype(BF16)
        o = _dot(qg_ref[:, hl], sb) + _dot(attn_ref[:, h * CHUNK:(h + 1) * CHUNK], ub)
        s_ref[0, h] = gl_ref[0, :, hl] * s + _dot_tn(kt_ref[:, hl], ub)
        y_ref[:, hl] = (_rms(o, gn) * (z[:, hl] * _sigmoid(z[:, hl]))).astype(y_ref.dtype)


def gdn_recurrence(pre, cols, s0, gdn_norm, y_prev, batch, nchunks, tok_block0):
    uv, wk, attn, qg, kt, gl = pre
    h = GDN_HEADS
    tok = lambda b, n: tok_block0 + b * nchunks + n
    tok_spec = lambda d: pl.BlockSpec((CHUNK, d), lambda b, n: (tok(b, n), 0))
    state_spec = pl.BlockSpec((1, h, GDN_DK, GDN_DV), lambda b, n: (b, 0, 0, 0))
    return pl.pallas_call(
        _gdn_rec_kernel,
        grid=(batch, nchunks),
        in_specs=[tok_spec(GDN_V), tok_spec(GDN_QK), tok_spec(h * CHUNK), tok_spec(GDN_QK),
                  tok_spec(GDN_QK),
                  pl.BlockSpec((1, 1, GDN_QK), lambda b, n: (tok(b, n), 0, 0)),
                  state_spec,
                  pl.BlockSpec((CHUNK, SLOT), lambda b, n: (tok(b, n), COL_Z)),
                  pl.BlockSpec((1, GDN_DV), lambda b, n: (0, 0)),
                  pl.BlockSpec(memory_space=pl.ANY)],
        out_specs=[tok_spec(GDN_V), state_spec],
        out_shape=[jax.ShapeDtypeStruct(y_prev.shape, y_prev.dtype),
                   jax.ShapeDtypeStruct((batch, h, GDN_DK, GDN_DV), F32)],
        input_output_aliases={9: 0},
        compiler_params=_params("parallel", "arbitrary"),
        name="gdn_recurrence",
    )(uv, wk, attn, qg, kt, gl, s0, cols, gdn_norm.reshape(1, -1), y_prev)


def _pad_cols(w, width):
    return jnp.pad(w, ((0, 0), (0, width - w.shape[1])))


def _layout_w_in(w):
    a0, b0 = 0, Q_LORA + KV_LORA + QK_ROPE
    c0 = b0 + 3 * BAND_HEADS * BAND_DIM
    q_a = w[:, a0:a0 + Q_LORA]
    c_kv = w[:, a0 + Q_LORA:a0 + Q_LORA + KV_LORA]
    k_r = w[:, a0 + Q_LORA + KV_LORA:b0]
    band = w[:, b0:c0]
    u = w[:, c0:c0 + C_CONV]
    z = w[:, c0 + C_CONV:c0 + C_CONV + GDN_V]
    a = w[:, c0 + C_CONV + GDN_V:c0 + C_CONV + GDN_V + GDN_HEADS]
    b = w[:, c0 + C_CONV + GDN_V + GDN_HEADS:]
    half = QK_ROPE // 2
    k_sw = jnp.concatenate([k_r[:, half:], k_r[:, :half]], axis=1)
    tail = jnp.concatenate([c_kv, _pad_cols(k_r, LANE), _pad_cols(k_sw, LANE),
                            _pad_cols(a, LANE), _pad_cols(b, LANE)], axis=1)
    return jnp.concatenate([u, q_a, band, z, tail], axis=1).astype(BF16)


def _layout_w_qb(w):
    k = w.shape[0]
    w = w.reshape(k, MLA_HEADS, QK_NOPE + QK_ROPE)
    nope = w[:, :, :QK_NOPE]
    rope = w[:, :, QK_NOPE:]
    half = QK_ROPE // 2
    rope_sw = jnp.concatenate([rope[:, :, half:], rope[:, :, :half]], axis=2)
    pad = lambda t: jnp.pad(t, ((0, 0), (0, 0), (0, LANE - QK_ROPE)))
    out = jnp.concatenate([nope, pad(rope), pad(rope_sw)], axis=1)
    return out.reshape(k, 3 * MLA_HEADS * LANE).astype(BF16)


def _rope_tables(pos):
    inv = 1.0 / (ROPE_THETA ** (jnp.arange(0, QK_ROPE, 2, dtype=F32) / QK_ROPE))
    ang = pos.astype(F32)[:, None] * inv[None, :]
    cos, sin = jnp.cos(ang), jnp.sin(ang)
    zeros = jnp.zeros((pos.shape[0], LANE - QK_ROPE), F32)
    return (jnp.concatenate([cos, cos, zeros], axis=1), jnp.concatenate([-sin, sin, zeros], axis=1))


def _band_bias_row(rel_bias, qchunks):
    heads = rel_bias.shape[0]
    span = _band_span(qchunks)
    left = BAND_ROWS + qchunks * CHUNK - 1 - REL_CLIP
    right = max(span - left - (2 * REL_CLIP + 1), 0)
    row = jnp.concatenate([jnp.broadcast_to(rel_bias[:, -1:], (heads, left)), rel_bias[:, ::-1],
                           jnp.broadcast_to(rel_bias[:, :1], (heads, right))], axis=1)
    return row[:, :span]


def kernel(x_prompt, x_sample, cache_latent, cache_k_rope, cache_band_k, cache_band_v, state_conv, state_delta, norm_ff1, ff1_w1, ff1_w3, ff1_w2, norm_mix, w_in, q_norm, w_qb, kv_norm, w_uk, w_uv, rel_bias, conv_w, a_log, dt_bias, gdn_norm, w_out, norm_ff2, ff2_w1, ff2_w3, ff2_w2, final_norm):
    batch, seq, d = x_prompt.shape
    dbatch, dseq, _ = x_sample.shape
    depth = w_in.shape[0]
    past = cache_latent.shape[2]
    assert dseq == CHUNK and seq % CHUNK == 0 and cache_band_k.shape[2] == BAND_ROWS
    mp, ms = batch * seq, dbatch * dseq
    m = mp + ms
    nchunks = seq // CHUNK
    blk_s = mp // CHUNK
    band_w = BAND_HEADS * BAND_DIM

    xp, xs = x_prompt.reshape(mp, d), x_sample.reshape(ms, d)
    pos = jnp.concatenate([jnp.tile(jnp.arange(seq, dtype=jnp.int32), batch),
                           jnp.tile(past + jnp.arange(dseq, dtype=jnp.int32), dbatch)])
    cos128, sin128 = _rope_tables(pos)
    zero_state = jnp.zeros((batch, GDN_HEADS, GDN_DK, GDN_DV), F32)

    def ffn(x, norm, w1, w3, w2, l):
        h = rmsnorm(x, norm[l], BF16)
        hid = matmul_swiglu(h, w1, w3, l, 544, 512)
        return matmul_residual(hid, w2, l, x, 0.5, 272, 512)

    def mixer_rows(width):
        return jnp.zeros((m, width), BF16)

    outs = [[] for _ in range(12)]
    for l in range(depth):
        if l == 0:
            hid = matmul_swiglu(rmsnorm_pair(xp, xs, norm_ff1[0], BF16), ff1_w1, ff1_w3, 0, 544, 512)
            x = matmul_residual_pair(hid, ff1_w2, 0, xp, xs, 0.5, 512)
        else:
            x = ffn(x, norm_ff1, ff1_w1, ff1_w3, ff1_w2, l)

        h = rmsnorm(x, norm_mix[l], BF16)
        cols = matmul(h, _layout_w_in(w_in[l]), F32, 544, 1024)

        wuk_t = jnp.transpose(w_uk[l], (1, 2, 0)).astype(BF16)
        wuv_t = jnp.transpose(w_uv[l], (1, 0, 2)).astype(BF16)
        q_lat, q_rope, lat, krope, klat_bf, krope_bf = mla_prep(
            cols, q_norm[l], kv_norm[l], _layout_w_qb(w_qb[l]), wuk_t, cos128, sin128)
        ya = mla_attention_prompt(q_lat, q_rope, klat_bf, krope_bf, wuv_t,
                                  mixer_rows(MLA_HEADS * V_HEAD), batch, seq)
        ya = mla_attention_sample(q_lat, q_rope, cache_latent, cache_k_rope, l, klat_bf, krope_bf,
                                  wuv_t, ya, dbatch, blk_s)

        band_k = cols[:, COL_BK * SLOT:(COL_BK + 1) * SLOT]
        band_v = cols[:, COL_BV * SLOT:(COL_BV + 1) * SLOT]

        def prompt_rows(t):
            t = t[:mp].reshape(batch, seq, band_w).astype(BF16)
            return jnp.pad(t, ((0, 0), (BAND_ROWS, 0), (0, 0)))

        def sample_rows(t, cache):
            return jnp.concatenate([cache.reshape(dbatch, BAND_ROWS, band_w).astype(BF16),
                                    t[mp:].reshape(dbatch, dseq, band_w).astype(BF16)], axis=1)

        yb = band_attention(cols, prompt_rows(band_k), prompt_rows(band_v), rel_bias[l],
                            mixer_rows(band_w), batch, nchunks, 0, BAND_ROWS, math.gcd(nchunks, 4))
        yb = band_attention(cols, sample_rows(band_k, cache_band_k[l]),
                            sample_rows(band_v, cache_band_v[l]), rel_bias[l], yb, dbatch, 1, blk_s, 0, 1)

        conv_init = jnp.concatenate(
            [jnp.zeros((batch, 8, C_CONV), F32),
             jnp.pad(state_conv[l], ((0, 0), (8 - (CONV_W - 1), 0), (0, 0)))], axis=0)
        aneg = _pad_cols(-jnp.exp(a_log[l].astype(F32)).reshape(1, -1), LANE)
        dtb = _pad_cols(dt_bias[l].astype(F32).reshape(1, -1), LANE)
        pre = gdn_pre(cols, conv_init, conv_w[l], aneg, dtb, nchunks, mp // CHUNK)
        yc, delta_p = gdn_recurrence(pre, cols, zero_state, gdn_norm[l], mixer_rows(GDN_V),
                                     batch, nchunks, 0)
        yc, delta_s = gdn_recurrence(pre, cols, state_delta[l], gdn_norm[l], yc, dbatch, 1, blk_s)

        x = matmul_mix(ya, yb, yc, w_out, l, x, 544, 512)
        x = ffn(x, norm_ff2, ff2_w1, ff2_w3, ff2_w2, l)

        cols_p = cols[:mp].reshape(batch, seq, IN_COLS_PAD)
        cols_s = cols[mp:].reshape(dbatch, dseq, IN_COLS_PAD)
        keep = min(BAND_ROWS, seq)
        layer_out = (
            lat[:mp].reshape(batch, seq, KV_LORA),
            krope[:mp].reshape(batch, seq, QK_ROPE),
            band_k[:mp].reshape(batch, seq, BAND_HEADS, BAND_DIM)[:, seq - keep:],
            band_v[:mp].reshape(batch, seq, BAND_HEADS, BAND_DIM)[:, seq - keep:],
            cols_p[:, seq - (CONV_W - 1):, :C_CONV],
            delta_p,
            lat[mp:].reshape(dbatch, dseq, KV_LORA),
            krope[mp:].reshape(dbatch, dseq, QK_ROPE),
            band_k[mp:].reshape(dbatch, dseq, BAND_HEADS, BAND_DIM),
            band_v[mp:].reshape(dbatch, dseq, BAND_HEADS, BAND_DIM),
            cols_s[:, dseq - (CONV_W - 1):, :C_CONV],
            delta_s,
        )
        for acc, t in zip(outs, layer_out):
            acc.append(t)

    y_prompt = rmsnorm(x, final_norm, F32, rows=mp).reshape(batch, seq, d)
    y_sample = rmsnorm(x, final_norm, F32, rows=ms, row_start=mp).reshape(dbatch, dseq, d)
    return (y_prompt, y_sample) + tuple(jnp.stack(t) for t in outs)
```

```python
import functools
import math

import jax
import jax.numpy as jnp
from jax import lax
from jax.experimental import pallas as pl
from jax.experimental.pallas import tpu as pltpu

F32 = jnp.float32
BF16 = jnp.bfloat16
HIGHEST = lax.Precision.HIGHEST

CHUNK = 64
CHUNK_SHIFT = 6
EPS = 1e-6
NEG = -1e30
Q_LORA = 1024
KV_LORA = 512
QK_NOPE = 128
QK_ROPE = 64
V_HEAD = 128
MLA_HEADS = 16
ROPE_THETA = 10000.0
MLA_SCALE = (QK_NOPE + QK_ROPE) ** -0.5
MLA_HEAD_GROUPS = 8
BAND_HEADS = 8
BAND_DIM = 128
BAND_PREV = 8
BAND_ROWS = BAND_PREV * CHUNK
BAND_KEYS = BAND_ROWS + CHUNK
REL_CLIP = 128
BAND_SCALE = BAND_DIM ** -0.5
GDN_HEADS = 8
GDN_GROUP = 8
GDN_DK = 128
GDN_DV = 128
CONV_W = 4
GDN_QK = GDN_HEADS * GDN_DK
GDN_V = GDN_HEADS * GDN_DV
C_CONV = 2 * GDN_QK + GDN_V
LANE = 128
SLOT = 1024
COL_U, COL_QA, COL_BQ, COL_BK, COL_BV, COL_Z, COL_TAIL = 0, 3, 4, 5, 6, 7, 8
IN_COLS_PAD = 9 * SLOT
TAIL_KR, TAIL_KSW, TAIL_A, TAIL_B = 512, 640, 768, 896

VMEM_LIMIT_BYTES = 56 * 1024 * 1024


def _params(*sem):
    return pltpu.CompilerParams(dimension_semantics=sem, vmem_limit_bytes=VMEM_LIMIT_BYTES)


def _tile(n, target, mult=16):
    best = None
    for t in range(mult, min(n, target) + 1, mult):
        if n % t == 0:
            best = t
    assert best is not None, (n, target)
    return best


def _dot(a, b, precision=None):
    return jnp.dot(a, b, preferred_element_type=F32, precision=precision)


def _dot_nt(a, b, precision=None):
    return lax.dot_general(a, b, (((1,), (1,)), ((), ())), preferred_element_type=F32,
                           precision=precision)


def _dot_tn(a, b, precision=None):
    return lax.dot_general(a, b, (((0,), (0,)), ((), ())), preferred_element_type=F32,
                           precision=precision)


def _split_bf16(x):
    hi = x.astype(BF16)
    return hi, (x - hi.astype(F32)).astype(BF16)


def _dot_3pass(a, b, expand=lambda t: t):
    a_hi, a_lo = _split_bf16(a)
    b_hi, b_lo = _split_bf16(b)
    m = a.shape[0]
    top = _dot(jnp.concatenate([a_hi, a_lo], axis=0), expand(b_hi))
    return top[:m] + top[m:] + _dot(a_hi, expand(b_lo))


def _sigmoid(x):
    return 1.0 / (1.0 + jnp.exp(-x))


def _rms(x, g):
    return x * lax.rsqrt(jnp.mean(x * x, axis=-1, keepdims=True) + EPS) * g


def _rmsnorm_kernel(x_ref, g_ref, o_ref):
    o_ref[...] = _rms(x_ref[...], g_ref[...]).astype(o_ref.dtype)


def rmsnorm(x, g, out_dtype, rows=None, row_start=0):
    m, d = x.shape
    rows = m if rows is None else rows
    tm = _tile(math.gcd(rows, row_start) if row_start else rows, 256, 8)
    off = row_start // tm
    return pl.pallas_call(
        _rmsnorm_kernel,
        grid=(rows // tm,),
        in_specs=[pl.BlockSpec((tm, d), lambda i: (i + off, 0)),
                  pl.BlockSpec((1, d), lambda i: (0, 0))],
        out_specs=pl.BlockSpec((tm, d), lambda i: (i, 0)),
        out_shape=jax.ShapeDtypeStruct((rows, d), out_dtype),
        compiler_params=_params("parallel"),
        name="rmsnorm",
    )(x, g.reshape(1, d))


def _pair_specs(first, second, tm, width, row_axis, col_fn=None):
    n_first = first.shape[0] // tm
    col = (lambda *g: 0) if col_fn is None else col_fn
    return (n_first,
            pl.BlockSpec((tm, width), lambda *g: (jnp.minimum(g[row_axis], n_first - 1), col(*g))),
            pl.BlockSpec((tm, width), lambda *g: (jnp.maximum(g[row_axis] - n_first, 0), col(*g))))


def _rmsnorm_pair_kernel(xp_ref, xs_ref, g_ref, o_ref, *, n_first):
    x = jnp.where(pl.program_id(0) < n_first, xp_ref[...], xs_ref[...])
    o_ref[...] = _rms(x, g_ref[...]).astype(o_ref.dtype)


def rmsnorm_pair(xp, xs, g, out_dtype):
    d = xp.shape[1]
    tm = _tile(math.gcd(xp.shape[0], xs.shape[0]), 256, 8)
    n_first, spec_p, spec_s = _pair_specs(xp, xs, tm, d, 0)
    m = xp.shape[0] + xs.shape[0]
    return pl.pallas_call(
        functools.partial(_rmsnorm_pair_kernel, n_first=n_first),
        grid=(m // tm,),
        in_specs=[spec_p, spec_s, pl.BlockSpec((1, d), lambda i: (0, 0))],
        out_specs=pl.BlockSpec((tm, d), lambda i: (i, 0)),
        out_shape=jax.ShapeDtypeStruct((m, d), out_dtype),
        compiler_params=_params("parallel"),
        name="rmsnorm_pair",
    )(xp, xs, g.reshape(1, d))


def _mm_kernel(a_ref, w_ref, o_ref):
    o_ref[...] = _dot(a_ref[...], w_ref[...]).astype(o_ref.dtype)


def _cast_weight(w_ref, wb_ref):
    @pl.when(pl.program_id(1) == 0)
    def _():
        wb_ref[...] = w_ref[...].astype(BF16)


def _mm_res_kernel(a_ref, w_ref, r_ref, o_ref, wb_ref, *, scale):
    _cast_weight(w_ref, wb_ref)
    o_ref[...] = r_ref[...] + scale * _dot(a_ref[...], wb_ref[...])


def _mm_res_pair_kernel(a_ref, w_ref, rp_ref, rs_ref, o_ref, wb_ref, *, scale, n_first):
    _cast_weight(w_ref, wb_ref)
    res = jnp.where(pl.program_id(1) < n_first, rp_ref[...], rs_ref[...])
    o_ref[...] = res + scale * _dot(a_ref[...], wb_ref[...])


def _mm_mix_kernel(ya_ref, yb_ref, yc_ref, w_ref, r_ref, o_ref, wb_ref):
    _cast_weight(w_ref, wb_ref)
    ka, kb = ya_ref.shape[1], yb_ref.shape[1]
    o_ref[...] = (r_ref[...] + _dot(ya_ref[...], wb_ref[0:ka, :])
                  + _dot(yb_ref[...], wb_ref[ka:ka + kb, :]) + _dot(yc_ref[...], wb_ref[ka + kb:, :]))


def _mm_swiglu_kernel(a_ref, w1_ref, w3_ref, o_ref, w1b_ref, w3b_ref):
    _cast_weight(w1_ref, w1b_ref)
    _cast_weight(w3_ref, w3b_ref)
    a = a_ref[...]
    h1 = _dot(a, w1b_ref[...])
    h3 = _dot(a, w3b_ref[...])
    o_ref[...] = (h1 * _sigmoid(h1) * h3).astype(o_ref.dtype)


def matmul(a, w, out_dtype, tm_target, tn_target):
    m, k = a.shape
    n = w.shape[1]
    tm, tn = _tile(m, tm_target), _tile(n, tn_target, LANE)
    return pl.pallas_call(
        _mm_kernel,
        grid=(n // tn, m // tm),
        in_specs=[pl.BlockSpec((tm, k), lambda j, i: (i, 0)),
                  pl.BlockSpec((k, tn), lambda j, i: (0, j))],
        out_specs=pl.BlockSpec((tm, tn), lambda j, i: (i, j)),
        out_shape=jax.ShapeDtypeStruct((m, n), out_dtype),
        compiler_params=_params("parallel", "parallel"),
        name="matmul",
    )(a, w)


def _layer_weight_spec(layer, k, tn):
    return pl.BlockSpec((None, k, tn), lambda j, i: (layer, 0, j))


def matmul_residual(a, w, layer, res, scale, tm_target, tn_target):
    m, k = a.shape
    n = w.shape[2]
    tm, tn = _tile(m, tm_target), _tile(n, tn_target, LANE)
    return pl.pallas_call(
        functools.partial(_mm_res_kernel, scale=scale),
        grid=(n // tn, m // tm),
        in_specs=[pl.BlockSpec((tm, k), lambda j, i: (i, 0)),
                  _layer_weight_spec(layer, k, tn),
                  pl.BlockSpec((tm, tn), lambda j, i: (i, j))],
        out_specs=pl.BlockSpec((tm, tn), lambda j, i: (i, j)),
        out_shape=jax.ShapeDtypeStruct((m, n), F32),
        scratch_shapes=[pltpu.VMEM((k, tn), BF16)],
        compiler_params=_params("parallel", "arbitrary"),
        name="matmul_residual",
    )(a, w, res)


def matmul_residual_pair(a, w, layer, res_p, res_s, scale, tn_target):
    m, k = a.shape
    n = w.shape[2]
    tm = _tile(math.gcd(res_p.shape[0], res_s.shape[0]), 256)
    tn = _tile(n, tn_target, LANE)
    n_first, spec_p, spec_s = _pair_specs(res_p, res_s, tm, tn, 1, lambda j, i: j)
    return pl.pallas_call(
        functools.partial(_mm_res_pair_kernel, scale=scale, n_first=n_first),
        grid=(n // tn, m // tm),
        in_specs=[pl.BlockSpec((tm, k), lambda j, i: (i, 0)),
                  _layer_weight_spec(layer, k, tn), spec_p, spec_s],
        out_specs=pl.BlockSpec((tm, tn), lambda j, i: (i, j)),
        out_shape=jax.ShapeDtypeStruct((m, n), F32),
        scratch_shapes=[pltpu.VMEM((k, tn), BF16)],
        compiler_params=_params("parallel", "arbitrary"),
        name="matmul_residual_pair",
    )(a, w, res_p, res_s)


def matmul_mix(ya, yb, yc, w, layer, res, tm_target, tn_target):
    m = ya.shape[0]
    k, n = w.shape[1], w.shape[2]
    tm, tn = _tile(m, tm_target), _tile(n, tn_target, LANE)
    row = lambda y: pl.BlockSpec((tm, y.shape[1]), lambda j, i: (i, 0))
    return pl.pallas_call(
        _mm_mix_kernel,
        grid=(n // tn, m // tm),
        in_specs=[row(ya), row(yb), row(yc), _layer_weight_spec(layer, k, tn),
                  pl.BlockSpec((tm, tn), lambda j, i: (i, j))],
        out_specs=pl.BlockSpec((tm, tn), lambda j, i: (i, j)),
        out_shape=jax.ShapeDtypeStruct((m, n), F32),
        scratch_shapes=[pltpu.VMEM((k, tn), BF16)],
        compiler_params=_params("parallel", "arbitrary"),
        name="matmul_mix",
    )(ya, yb, yc, w, res)


def matmul_swiglu(a, w1, w3, layer, tm_target, tn_target):
    m, k = a.shape
    n = w1.shape[2]
    tm, tn = _tile(m, tm_target), _tile(n, tn_target, LANE)
    return pl.pallas_call(
        _mm_swiglu_kernel,
        grid=(n // tn, m // tm),
        in_specs=[pl.BlockSpec((tm, k), lambda j, i: (i, 0)),
                  _layer_weight_spec(layer, k, tn),
                  _layer_weight_spec(layer, k, tn)],
        out_specs=pl.BlockSpec((tm, tn), lambda j, i: (i, j)),
        out_shape=jax.ShapeDtypeStruct((m, n), BF16),
        scratch_shapes=[pltpu.VMEM((k, tn), BF16), pltpu.VMEM((k, tn), BF16)],
        compiler_params=_params("parallel", "arbitrary"),
        name="matmul_swiglu",
    )(a, w1, w3)


def _mla_prep_kernel(qa_ref, tail_ref, qn_ref, kvn_ref, wq_ref, wuk_ref, cos_ref, sin_ref,
                     qlat_ref, qrope_ref, lat_ref, krope_ref, klat_bf_ref, krope_bf_ref):
    hq = _rms(qa_ref[...], qn_ref[...]).astype(BF16)
    q_all = _dot(hq, wq_ref[...])
    cos = cos_ref[...]
    sin = sin_ref[...]
    hw = MLA_HEADS * LANE
    for h in range(MLA_HEADS):
        qn = q_all[:, h * LANE:(h + 1) * LANE].astype(BF16)
        qlat_ref[h] = (_dot(qn, wuk_ref[h]) * MLA_SCALE).astype(BF16)
        qr = (q_all[:, hw + h * LANE:hw + (h + 1) * LANE] * cos
              + q_all[:, 2 * hw + h * LANE:2 * hw + (h + 1) * LANE] * sin)
        qrope_ref[h] = (qr[:, :QK_ROPE] * MLA_SCALE).astype(BF16)
    tail = tail_ref[...]
    lat = _rms(tail[:, :KV_LORA], kvn_ref[...])
    lat_ref[...] = lat
    klat_bf_ref[...] = lat.astype(BF16)
    kr = tail[:, TAIL_KR:TAIL_KR + LANE] * cos + tail[:, TAIL_KSW:TAIL_KSW + LANE] * sin
    krope_ref[...] = kr[:, :QK_ROPE]
    krope_bf_ref[...] = kr[:, :QK_ROPE].astype(BF16)


def mla_prep(cols, q_norm, kv_norm, wq_all, wuk_t, cos128, sin128):
    m = cols.shape[0]
    tm = _tile(m, 256)
    h = MLA_HEADS
    const2 = lambda i: (0, 0)
    return pl.pallas_call(
        _mla_prep_kernel,
        grid=(m // tm,),
        in_specs=[pl.BlockSpec((tm, SLOT), lambda i: (i, COL_QA)),
                  pl.BlockSpec((tm, SLOT), lambda i: (i, COL_TAIL)),
                  pl.BlockSpec((1, Q_LORA), const2),
                  pl.BlockSpec((1, KV_LORA), const2),
                  pl.BlockSpec(wq_all.shape, const2, pipeline_mode=pl.Buffered(1)),
                  pl.BlockSpec(wuk_t.shape, lambda i: (0, 0, 0), pipeline_mode=pl.Buffered(1)),
                  pl.BlockSpec((tm, LANE), lambda i: (i, 0)),
                  pl.BlockSpec((tm, LANE), lambda i: (i, 0))],
        out_specs=[pl.BlockSpec((h, tm, KV_LORA), lambda i: (0, i, 0)),
                   pl.BlockSpec((h, tm, QK_ROPE), lambda i: (0, i, 0)),
                   pl.BlockSpec((tm, KV_LORA), lambda i: (i, 0)),
                   pl.BlockSpec((tm, QK_ROPE), lambda i: (i, 0)),
                   pl.BlockSpec((tm, KV_LORA), lambda i: (i, 0)),
                   pl.BlockSpec((tm, QK_ROPE), lambda i: (i, 0))],
        out_shape=[jax.ShapeDtypeStruct((h, m, KV_LORA), BF16),
                   jax.ShapeDtypeStruct((h, m, QK_ROPE), BF16),
                   jax.ShapeDtypeStruct((m, KV_LORA), F32),
                   jax.ShapeDtypeStruct((m, QK_ROPE), F32),
                   jax.ShapeDtypeStruct((m, KV_LORA), BF16),
                   jax.ShapeDtypeStruct((m, QK_ROPE), BF16)],
        compiler_params=_params("parallel"),
        name="mla_prep",
    )(cols, cols, q_norm.reshape(1, -1), kv_norm.reshape(1, -1), wq_all, wuk_t, cos128, sin128)


def _softmax_init(m_ref, l_ref, acc_ref):
    m_ref[...] = jnp.full(m_ref.shape, NEG, F32)
    l_ref[...] = jnp.zeros(l_ref.shape, F32)
    acc_ref[...] = jnp.zeros(acc_ref.shape, F32)


def _mla_step(ql_ref, qr_ref, k_l, k_r, m_ref, l_ref, acc_ref, tq, visible=None):
    hg = MLA_HEADS // MLA_HEAD_GROUPS
    rows = hg * tq

    def scores(g):
        q_l = ql_ref[g * hg:(g + 1) * hg].reshape(rows, KV_LORA)
        q_r = qr_ref[g * hg:(g + 1) * hg].reshape(rows, QK_ROPE)
        s = _dot_nt(q_l, k_l) + _dot_nt(q_r, k_r)
        return s if visible is None else jnp.where(visible, s, NEG)

    def update(g, s):
        rs = slice(g * rows, (g + 1) * rows)
        m_old = m_ref[rs]
        m_new = jnp.maximum(m_old, jnp.max(s, axis=-1, keepdims=True))
        alpha = jnp.exp(m_old - m_new)
        p = jnp.exp(s - m_new)
        l_ref[rs] = alpha * l_ref[rs] + jnp.sum(p, axis=-1, keepdims=True)
        acc_ref[rs] = alpha * acc_ref[rs] + _dot(p.astype(BF16), k_l)
        m_ref[rs] = m_new

    s = scores(0)
    for g in range(1, MLA_HEAD_GROUPS):
        s_next = scores(g)
        update(g - 1, s)
        s = s_next
    update(MLA_HEAD_GROUPS - 1, s)


def _mla_finish(wuv_ref, o_ref, l_ref, acc_ref, tq):
    ctx = acc_ref[...] / l_ref[...]
    for h in range(MLA_HEADS):
        ch = ctx[h * tq:(h + 1) * tq].astype(BF16)
        o_ref[:, h * V_HEAD:(h + 1) * V_HEAD] = _dot(ch, wuv_ref[h]).astype(o_ref.dtype)


def _mla_prompt_kernel(ql_ref, qr_ref, kl_ref, kr_ref, wuv_ref, y_prev_ref, o_ref,
                       m_ref, l_ref, acc_ref, *, tq, tk):
    del y_prev_ref
    qi = pl.program_id(1)
    kj = pl.program_id(2)
    last = (qi * tq + tq - 1) // tk

    @pl.when(kj == 0)
    def _():
        _softmax_init(m_ref, l_ref, acc_ref)

    @pl.when(kj < last)
    def _():
        _mla_step(ql_ref, qr_ref, kl_ref[...], kr_ref[...], m_ref, l_ref, acc_ref, tq)

    @pl.when(kj == last)
    def _():
        shape = (MLA_HEADS // MLA_HEAD_GROUPS * tq, tk)
        row = lax.broadcasted_iota(jnp.int32, shape, 0)
        col = lax.broadcasted_iota(jnp.int32, shape, 1)
        q_chunk = (qi * tq + row % tq) // CHUNK
        k_chunk = (kj * tk + col) // CHUNK
        _mla_step(ql_ref, qr_ref, kl_ref[...], kr_ref[...], m_ref, l_ref, acc_ref, tq,
                  visible=k_chunk <= q_chunk)

    @pl.when(kj == pl.num_programs(2) - 1)
    def _():
        _mla_finish(wuv_ref, o_ref, l_ref, acc_ref, tq)


def mla_attention_prompt(q_lat, q_rope, k_lat, k_rope, wuv, y_prev, batch, seq):
    tq = 128
    tk = _tile(seq, 512, CHUNK)
    nq, nk = seq // tq, seq // tk
    h = MLA_HEADS
    rows = h * tq

    def k_map(b, qi, kj):
        return (b * nk + jnp.minimum(kj, (qi * tq + tq - 1) // tk), 0)

    return pl.pallas_call(
        functools.partial(_mla_prompt_kernel, tq=tq, tk=tk),
        grid=(batch, nq, nk),
        in_specs=[pl.BlockSpec((h, tq, KV_LORA), lambda b, qi, kj: (0, b * nq + qi, 0)),
                  pl.BlockSpec((h, tq, QK_ROPE), lambda b, qi, kj: (0, b * nq + qi, 0)),
                  pl.BlockSpec((tk, KV_LORA), k_map),
                  pl.BlockSpec((tk, QK_ROPE), k_map),
                  pl.BlockSpec(wuv.shape, lambda b, qi, kj: (0, 0, 0)),
                  pl.BlockSpec(memory_space=pl.ANY)],
        out_specs=pl.BlockSpec((tq, h * V_HEAD), lambda b, qi, kj: (b * nq + qi, 0)),
        out_shape=jax.ShapeDtypeStruct(y_prev.shape, y_prev.dtype),
        input_output_aliases={5: 0},
        scratch_shapes=[pltpu.VMEM((rows, 1), F32), pltpu.VMEM((rows, 1), F32),
                        pltpu.VMEM((rows, KV_LORA), F32)],
        compiler_params=_params("parallel", "parallel", "arbitrary"),
        name="mla_attention_prompt",
    )(q_lat, q_rope, k_lat, k_rope, wuv, y_prev)


def _mla_sample_kernel(ql_ref, qr_ref, pl_ref, pr_ref, kl_ref, kr_ref, wuv_ref, y_prev_ref, o_ref,
                       m_ref, l_ref, acc_ref):
    del y_prev_ref
    kj = pl.program_id(1)
    n_past = pl.num_programs(1) - 1

    @pl.when(kj == 0)
    def _():
        _softmax_init(m_ref, l_ref, acc_ref)

    @pl.when(kj < n_past)
    def _():
        _mla_step(ql_ref, qr_ref, pl_ref[0].astype(BF16), pr_ref[0].astype(BF16),
                  m_ref, l_ref, acc_ref, CHUNK)

    @pl.when(kj == n_past)
    def _():
        _mla_step(ql_ref, qr_ref, kl_ref[...], kr_ref[...], m_ref, l_ref, acc_ref, CHUNK)
        _mla_finish(wuv_ref, o_ref, l_ref, acc_ref, CHUNK)


def mla_attention_sample(q_lat, q_rope, past_lat, past_rope, layer, k_lat, k_rope, wuv, y_prev,
                         batch, tok_block0):
    past = past_lat.shape[2]
    tk = _tile(past, 512, CHUNK)
    n_past = past // tk
    h = MLA_HEADS
    rows = h * CHUNK
    past_map = lambda b, kj: (layer, b, jnp.minimum(kj, n_past - 1), 0)
    return pl.pallas_call(
        _mla_sample_kernel,
        grid=(batch, n_past + 1),
        in_specs=[pl.BlockSpec((h, CHUNK, KV_LORA), lambda b, kj: (0, tok_block0 + b, 0)),
                  pl.BlockSpec((h, CHUNK, QK_ROPE), lambda b, kj: (0, tok_block0 + b, 0)),
                  pl.BlockSpec((None, 1, tk, KV_LORA), past_map),
                  pl.BlockSpec((None, 1, tk, QK_ROPE), past_map),
                  pl.BlockSpec((CHUNK, KV_LORA), lambda b, kj: (tok_block0 + b, 0)),
                  pl.BlockSpec((CHUNK, QK_ROPE), lambda b, kj: (tok_block0 + b, 0)),
                  pl.BlockSpec(wuv.shape, lambda b, kj: (0, 0, 0)),
                  pl.BlockSpec(memory_space=pl.ANY)],
        out_specs=pl.BlockSpec((CHUNK, h * V_HEAD), lambda b, kj: (tok_block0 + b, 0)),
        out_shape=jax.ShapeDtypeStruct(y_prev.shape, y_prev.dtype),
        input_output_aliases={7: 0},
        scratch_shapes=[pltpu.VMEM((rows, 1), F32), pltpu.VMEM((rows, 1), F32),
                        pltpu.VMEM((rows, KV_LORA), F32)],
        compiler_params=_params("parallel", "arbitrary"),
        name="mla_attention_sample",
    )(q_lat, q_rope, past_lat, past_rope, k_lat, k_rope, wuv, y_prev)


def _band_span(qchunks):
    tq, nk = qchunks * CHUNK, (qchunks + BAND_PREV) * CHUNK
    return -(-(tq + nk - 1) // LANE) * LANE


def _band_kernel(q_ref, k_ref, v_ref, g_ref, y_prev_ref, o_ref, bias_ref, *, pad_rows, qchunks):
    del y_prev_ref
    n = pl.program_id(1)
    tq, nk = qchunks * CHUNK, (qchunks + BAND_PREV) * CHUNK
    span = _band_span(qchunks)

    @pl.when(jnp.logical_and(pl.program_id(0) == 0, n == 0))
    def _():
        for h in range(BAND_HEADS):
            rows = jnp.broadcast_to(g_ref[h:h + 1, :], (tq, span))
            rolled = pltpu.roll(rows, span - (tq - 1), 1, stride=1, stride_axis=0)
            bias_ref[h] = rolled[:, :nk]

    start = pl.multiple_of(n * tq, CHUNK)
    row = lax.broadcasted_iota(jnp.int32, (tq, nk), 0)
    col = lax.broadcasted_iota(jnp.int32, (tq, nk), 1)
    visible = start + col >= pad_rows
    if qchunks > 1:
        ahead = (col >> CHUNK_SHIFT) - (row >> CHUNK_SHIFT)
        visible = visible & (ahead >= 0) & (ahead <= BAND_PREV)
    q = q_ref[...]
    for h in range(BAND_HEADS):
        lanes = slice(h * BAND_DIM, (h + 1) * BAND_DIM)
        qh = q[:, lanes].astype(BF16)
        kh = k_ref[0, pl.ds(start, nk), lanes]
        vh = v_ref[0, pl.ds(start, nk), lanes]
        s = _dot_nt(qh, kh) * BAND_SCALE + bias_ref[h]
        if pad_rows or qchunks > 1:
            s = jnp.where(visible, s, NEG)
        p = jnp.exp(s - jnp.max(s, axis=-1, keepdims=True))
        o = _dot(p.astype(BF16), vh) / jnp.sum(p, axis=-1, keepdims=True)
        o_ref[:, lanes] = o.astype(o_ref.dtype)


def band_attention(cols, k_rows, v_rows, rel_bias, y_prev, batch, nchunks, tok_block0, pad_rows, qchunks):
    assert nchunks % qchunks == 0 and tok_block0 % qchunks == 0
    width = BAND_HEADS * BAND_DIM
    rows = k_rows.shape[1]
    tq, nk = qchunks * CHUNK, (qchunks + BAND_PREV) * CHUNK
    nblk = nchunks // qchunks
    bias_row = _band_bias_row(rel_bias, qchunks)
    tok = lambda b, n: tok_block0 // qchunks + b * nblk + n
    return pl.pallas_call(
        functools.partial(_band_kernel, pad_rows=pad_rows, qchunks=qchunks),
        grid=(batch, nblk),
        in_specs=[pl.BlockSpec((tq, width), lambda b, n: (tok(b, n), COL_BQ)),
                  pl.BlockSpec((1, rows, width), lambda b, n: (b, 0, 0)),
                  pl.BlockSpec((1, rows, width), lambda b, n: (b, 0, 0)),
                  pl.BlockSpec(bias_row.shape, lambda b, n: (0, 0)),
                  pl.BlockSpec(memory_space=pl.ANY)],
        out_specs=pl.BlockSpec((tq, width), lambda b, n: (tok(b, n), 0)),
        out_shape=jax.ShapeDtypeStruct(y_prev.shape, y_prev.dtype),
        input_output_aliases={4: 0},
        scratch_shapes=[pltpu.VMEM((BAND_HEADS, tq, nk), F32)],
        compiler_params=_params("arbitrary", "arbitrary"),
        name="band_attention",
    )(cols, k_rows, v_rows, bias_row, y_prev)


def _expand_matrix(rows, first, width, per_shift):
    r = lax.broadcasted_iota(jnp.int32, (rows, width), 0)
    c = lax.broadcasted_iota(jnp.int32, (rows, width), 1)
    return (r == first + (c >> per_shift)).astype(F32)


def _unit_lower_inverse(a, row, col, group):
    gl = a.shape[1]
    r = lax.broadcasted_iota(jnp.int32, (gl, gl), 0)
    c = lax.broadcasted_iota(jnp.int32, (gl, gl), 1)
    diag_blocks = ((r >> CHUNK_SHIFT) == (c >> CHUNK_SHIFT)).astype(BF16)

    def block_diag(w):
        return jnp.concatenate([w] * group, axis=0) * diag_blocks

    t = (row == col).astype(F32) - jnp.where((row >> 1) == (col >> 1), a, 0.0)
    for shift in range(1, CHUNK_SHIFT):
        lower_left = ((row >> (shift + 1)) == (col >> (shift + 1))) & ((row >> shift) != (col >> shift))
        x = _dot_3pass(t, jnp.where(lower_left, a, 0.0), block_diag)
        t = t - _dot_3pass(x, t, block_diag)
    return t


def _l2norm_heads(x, heads, width):
    parts = []
    for h in range(heads):
        xh = x[:, h * width:(h + 1) * width]
        parts.append(xh * lax.rsqrt(jnp.sum(xh * xh, axis=-1, keepdims=True) + EPS))
    return jnp.concatenate(parts, axis=1)


def _gdn_pre_kernel(u_ref, uprev_ref, init_ref, tail_ref, cw_ref, aneg_ref, dtb_ref,
                    uv_ref, wk_ref, attn_ref, qg_ref, kt_ref, gl_ref, ext_ref,
                    *, chunks_per_seq, n_prompt_chunks):
    group = GDN_GROUP
    gw = group * CHUNK
    c = pl.program_id(0)
    is_start = jnp.logical_or(c >= n_prompt_chunks, c % chunks_per_seq == 0)
    ext_ref[0:8, :] = jnp.where(is_start, init_ref[0], uprev_ref[...])
    ext_ref[8:8 + CHUNK, :] = u_ref[...]
    cw = cw_ref[...]
    uc = (cw[3:4] * ext_ref[8:8 + CHUNK, :] + cw[2:3] * ext_ref[7:7 + CHUNK, :]
          + cw[1:2] * ext_ref[6:6 + CHUNK, :] + cw[0:1] * ext_ref[5:5 + CHUNK, :])
    uc = uc * _sigmoid(uc)

    tail = tail_ref[...]
    xa = tail[:, TAIL_A:TAIL_A + LANE] + dtb_ref[...]
    softplus = jnp.maximum(xa, 0.0) + jnp.log(1.0 + jnp.exp(-jnp.abs(xa)))
    g = aneg_ref[...] * softplus
    beta = _sigmoid(tail[:, TAIL_B:TAIL_B + LANE])

    r64 = lax.broadcasted_iota(jnp.int32, (CHUNK, CHUNK), 0)
    c64 = lax.broadcasted_iota(jnp.int32, (CHUNK, CHUNK), 1)
    gc = _dot((r64 >= c64).astype(F32), g, HIGHEST)
    egc = jnp.exp(gc)
    ekt = jnp.exp(gc[CHUNK - 1:CHUNK, :] - gc)
    e128 = _expand_matrix(LANE, 0, GDN_HEADS * GDN_DK, 7)
    beta_x = _dot(beta, e128, HIGHEST)
    egc_x = _dot(egc, e128, HIGHEST)
    ekt_x = _dot(ekt, e128, HIGHEST)
    gl_ref[0] = egc_x[CHUNK - 1:CHUNK, :]

    qn = _l2norm_heads(uc[:, :GDN_QK], GDN_HEADS, GDN_DK) * (GDN_DK ** -0.5)
    kn = _l2norm_heads(uc[:, GDN_QK:2 * GDN_QK], GDN_HEADS, GDN_DK)
    v_all = uc[:, 2 * GDN_QK:]
    qg_ref[...] = (qn * egc_x).astype(BF16)
    kt_ref[...] = (kn * ekt_x).astype(BF16)
    rhs_v = beta_x * v_all
    rhs_k = beta_x * egc_x * kn

    row = lax.broadcasted_iota(jnp.int32, (CHUNK, gw), 0)
    col = lax.broadcasted_iota(jnp.int32, (CHUNK, gw), 1) & (CHUNK - 1)
    incl = row >= col
    head_r = lax.broadcasted_iota(jnp.int32, (gw, group * GDN_DK), 0) >> CHUNK_SHIFT
    head_c = lax.broadcasted_iota(jnp.int32, (gw, group * GDN_DK), 1) >> 7
    for grp in range(GDN_HEADS // group):
        lanes = slice(grp * group * GDN_DK, (grp + 1) * group * GDN_DK)
        e64 = _expand_matrix(LANE, grp * group, gw, CHUNK_SHIFT)
        gcw = _dot(gc, e64, HIGHEST)
        bw = _dot(beta, e64, HIGHEST)
        gc_t = jnp.sum(jnp.where(row == col, gcw, 0.0), axis=0, keepdims=True)
        decay = jnp.where(incl, jnp.exp(jnp.where(incl, gcw - gc_t, 0.0)), 0.0)
        kg = kn[:, lanes]
        k_rows = jnp.where(head_r == head_c, jnp.concatenate([kg] * group, axis=0), 0.0).astype(BF16)
        kk = _dot_nt(kg.astype(BF16), k_rows)
        qk = _dot_nt(qn[:, lanes].astype(BF16), k_rows)
        a_mat = jnp.where(row > col, bw * kk * decay, 0.0)
        t_inv = _unit_lower_inverse(a_mat, row, col, group)
        attn_ref[:, grp * gw:(grp + 1) * gw] = (qk * decay).astype(BF16)
        for hh in range(group):
            h = grp * group + hh
            hl = slice(h * GDN_DK, (h + 1) * GDN_DK)
            rhs = jnp.concatenate([rhs_v[:, hl], rhs_k[:, hl]], axis=1)
            sol = _dot_3pass(t_inv[:, hh * CHUNK:(hh + 1) * CHUNK], rhs)
            uv_ref[:, hl] = sol[:, :GDN_DV]
            wk_ref[:, hl] = sol[:, GDN_DV:].astype(BF16)


def gdn_pre(cols, conv_init, conv_w, aneg, dtb, chunks_per_seq, n_prompt_chunks):
    m = cols.shape[0]
    nch = m // CHUNK
    h = GDN_HEADS

    def init_map(c):
        return (jnp.where(c < n_prompt_chunks, c // chunks_per_seq,
                          n_prompt_chunks // chunks_per_seq + c - n_prompt_chunks), 0, 0)

    tok_spec = lambda d: pl.BlockSpec((CHUNK, d), lambda c: (c, 0))
    return pl.pallas_call(
        functools.partial(_gdn_pre_kernel, chunks_per_seq=chunks_per_seq,
                          n_prompt_chunks=n_prompt_chunks),
        grid=(nch,),
        in_specs=[pl.BlockSpec((CHUNK, C_CONV), lambda c: (c, COL_U)),
                  pl.BlockSpec((8, C_CONV), lambda c: (jnp.maximum(c * (CHUNK // 8) - 1, 0), COL_U)),
                  pl.BlockSpec((1, 8, C_CONV), init_map),
                  pl.BlockSpec((CHUNK, SLOT), lambda c: (c, COL_TAIL)),
                  pl.BlockSpec((CONV_W, C_CONV), lambda c: (0, 0)),
                  pl.BlockSpec((1, LANE), lambda c: (0, 0)),
                  pl.BlockSpec((1, LANE), lambda c: (0, 0))],
        out_specs=[tok_spec(GDN_V), tok_spec(GDN_QK), tok_spec(h * CHUNK), tok_spec(GDN_QK),
                   tok_spec(GDN_QK), pl.BlockSpec((1, 1, GDN_QK), lambda c: (c, 0, 0))],
        out_shape=[jax.ShapeDtypeStruct((m, GDN_V), F32),
                   jax.ShapeDtypeStruct((m, GDN_QK), BF16),
                   jax.ShapeDtypeStruct((m, h * CHUNK), BF16),
                   jax.ShapeDtypeStruct((m, GDN_QK), BF16),
                   jax.ShapeDtypeStruct((m, GDN_QK), BF16),
                   jax.ShapeDtypeStruct((nch, 1, GDN_QK), F32)],
        scratch_shapes=[pltpu.VMEM((8 + CHUNK, C_CONV), F32)],
        compiler_params=_params("parallel"),
        name="gdn_pre",
    )(cols, cols, conv_init, cols, conv_w, aneg, dtb)


def _gdn_rec_kernel(uv_ref, wk_ref, attn_ref, qg_ref, kt_ref, gl_ref, s0_ref, z_ref, gn_ref,
                    y_prev_ref, y_ref, s_ref):
    del y_prev_ref

    @pl.when(pl.program_id(1) == 0)
    def _():
        s_ref[...] = s0_ref[...]

    z = z_ref[...]
    gn = gn_ref[...]
    heads = range(GDN_HEADS)
    lanes = [slice(h * GDN_DK, (h + 1) * GDN_DK) for h in heads]
    s_old = [s_ref[0, h] for h in heads]
    s_bf = [s.astype(BF16) for s in s_old]
    ws = [_dot(wk_ref[:, lanes[h]], s_bf[h]) for h in heads]
    qs = [_dot(qg_ref[:, lanes[h]], s_bf[h]) for h in heads]
    u_bf = [(uv_ref[:, lanes[h]] - ws[h]).astype(BF16) for h in heads]
    outs = [qs[h] + _dot(attn_ref[:, h * CHUNK:(h + 1) * CHUNK], u_bf[h]) for h in heads]
    for h in heads:
        s_ref[0, h] = gl_ref[0, :, lanes[h]] * s_old[h] + _dot_tn(kt_ref[:, lanes[h]], u_bf[h])
    for h in heads:
        zh = z[:, lanes[h]]
        y_ref[:, lanes[h]] = (_rms(outs[h], gn) * (zh * _sigmoid(zh))).astype(y_ref.dtype)


def gdn_recurrence(pre, cols, s0, gdn_norm, y_prev, batch, nchunks, tok_block0):
    uv, wk, attn, qg, kt, gl = pre
    h = GDN_HEADS
    tok = lambda b, n: tok_block0 + b * nchunks + n
    tok_spec = lambda d: pl.BlockSpec((CHUNK, d), lambda b, n: (tok(b, n), 0))
    state_spec = pl.BlockSpec((1, h, GDN_DK, GDN_DV), lambda b, n: (b, 0, 0, 0))
    return pl.pallas_call(
        _gdn_rec_kernel,
        grid=(batch, nchunks),
        in_specs=[tok_spec(GDN_V), tok_spec(GDN_QK), tok_spec(h * CHUNK), tok_spec(GDN_QK),
                  tok_spec(GDN_QK),
                  pl.BlockSpec((1, 1, GDN_QK), lambda b, n: (tok(b, n), 0, 0)),
                  state_spec,
                  pl.BlockSpec((CHUNK, SLOT), lambda b, n: (tok(b, n), COL_Z)),
                  pl.BlockSpec((1, GDN_DV), lambda b, n: (0, 0)),
                  pl.BlockSpec(memory_space=pl.ANY)],
        out_specs=[tok_spec(GDN_V), state_spec],
        out_shape=[jax.ShapeDtypeStruct(y_prev.shape, y_prev.dtype),
                   jax.ShapeDtypeStruct((batch, h, GDN_DK, GDN_DV), F32)],
        input_output_aliases={9: 0},
        compiler_params=_params("parallel", "arbitrary"),
        name="gdn_recurrence",
    )(uv, wk, attn, qg, kt, gl, s0, cols, gdn_norm.reshape(1, -1), y_prev)


def _pad_cols(w, width):
    return jnp.pad(w, ((0, 0), (0, width - w.shape[1])))


def _layout_w_in(w):
    a0, b0 = 0, Q_LORA + KV_LORA + QK_ROPE
    c0 = b0 + 3 * BAND_HEADS * BAND_DIM
    q_a = w[:, a0:a0 + Q_LORA]
    c_kv = w[:, a0 + Q_LORA:a0 + Q_LORA + KV_LORA]
    k_r = w[:, a0 + Q_LORA + KV_LORA:b0]
    band = w[:, b0:c0]
    u = w[:, c0:c0 + C_CONV]
    z = w[:, c0 + C_CONV:c0 + C_CONV + GDN_V]
    a = w[:, c0 + C_CONV + GDN_V:c0 + C_CONV + GDN_V + GDN_HEADS]
    b = w[:, c0 + C_CONV + GDN_V + GDN_HEADS:]
    half = QK_ROPE // 2
    k_sw = jnp.concatenate([k_r[:, half:], k_r[:, :half]], axis=1)
    tail = jnp.concatenate([c_kv, _pad_cols(k_r, LANE), _pad_cols(k_sw, LANE),
                            _pad_cols(a, LANE), _pad_cols(b, LANE)], axis=1)
    return jnp.concatenate([u, q_a, band, z, tail], axis=1).astype(BF16)


def _layout_w_qb(w):
    k = w.shape[0]
    w = w.reshape(k, MLA_HEADS, QK_NOPE + QK_ROPE)
    nope = w[:, :, :QK_NOPE]
    rope = w[:, :, QK_NOPE:]
    half = QK_ROPE // 2
    rope_sw = jnp.concatenate([rope[:, :, half:], rope[:, :, :half]], axis=2)
    pad = lambda t: jnp.pad(t, ((0, 0), (0, 0), (0, LANE - QK_ROPE)))
    out = jnp.concatenate([nope, pad(rope), pad(rope_sw)], axis=1)
    return out.reshape(k, 3 * MLA_HEADS * LANE).astype(BF16)


def _rope_tables(pos):
    inv = 1.0 / (ROPE_THETA ** (jnp.arange(0, QK_ROPE, 2, dtype=F32) / QK_ROPE))
    ang = pos.astype(F32)[:, None] * inv[None, :]
    cos, sin = jnp.cos(ang), jnp.sin(ang)
    zeros = jnp.zeros((pos.shape[0], LANE - QK_ROPE), F32)
    return (jnp.concatenate([cos, cos, zeros], axis=1), jnp.concatenate([-sin, sin, zeros], axis=1))


def _band_bias_row(rel_bias, qchunks):
    heads = rel_bias.shape[0]
    span = _band_span(qchunks)
    left = BAND_ROWS + qchunks * CHUNK - 1 - REL_CLIP
    right = max(span - left - (2 * REL_CLIP + 1), 0)
    row = jnp.concatenate([jnp.broadcast_to(rel_bias[:, -1:], (heads, left)), rel_bias[:, ::-1],
                           jnp.broadcast_to(rel_bias[:, :1], (heads, right))], axis=1)
    return row[:, :span]


def kernel(x_prompt, x_sample, cache_latent, cache_k_rope, cache_band_k, cache_band_v, state_conv, state_delta, norm_ff1, ff1_w1, ff1_w3, ff1_w2, norm_mix, w_in, q_norm, w_qb, kv_norm, w_uk, w_uv, rel_bias, conv_w, a_log, dt_bias, gdn_norm, w_out, norm_ff2, ff2_w1, ff2_w3, ff2_w2, final_norm):
    batch, seq, d = x_prompt.shape
    dbatch, dseq, _ = x_sample.shape
    depth = w_in.shape[0]
    past = cache_latent.shape[2]
    assert dseq == CHUNK and seq % CHUNK == 0 and cache_band_k.shape[2] == BAND_ROWS
    mp, ms = batch * seq, dbatch * dseq
    m = mp + ms
    nchunks = seq // CHUNK
    blk_s = mp // CHUNK
    band_w = BAND_HEADS * BAND_DIM

    xp, xs = x_prompt.reshape(mp, d), x_sample.reshape(ms, d)
    pos = jnp.concatenate([jnp.tile(jnp.arange(seq, dtype=jnp.int32), batch),
                           jnp.tile(past + jnp.arange(dseq, dtype=jnp.int32), dbatch)])
    cos128, sin128 = _rope_tables(pos)
    zero_state = jnp.zeros((batch, GDN_HEADS, GDN_DK, GDN_DV), F32)

    def ffn(x, norm, w1, w3, w2, l):
        h = rmsnorm(x, norm[l], BF16)
        hid = matmul_swiglu(h, w1, w3, l, 544, 512)
        return matmul_residual(hid, w2, l, x, 0.5, 272, 512)

    def mixer_rows(width):
        return jnp.zeros((m, width), BF16)

    outs = [[] for _ in range(12)]
    for l in range(depth):
        if l == 0:
            hid = matmul_swiglu(rmsnorm_pair(xp, xs, norm_ff1[0], BF16), ff1_w1, ff1_w3, 0, 544, 512)
            x = matmul_residual_pair(hid, ff1_w2, 0, xp, xs, 0.5, 512)
        else:
            x = ffn(x, norm_ff1, ff1_w1, ff1_w3, ff1_w2, l)

        h = rmsnorm(x, norm_mix[l], BF16)
        cols = matmul(h, _layout_w_in(w_in[l]), F32, 544, 1024)

        wuk_t = jnp.transpose(w_uk[l], (1, 2, 0)).astype(BF16)
        wuv_t = jnp.transpose(w_uv[l], (1, 0, 2)).astype(BF16)
        q_lat, q_rope, lat, krope, klat_bf, krope_bf = mla_prep(
            cols, q_norm[l], kv_norm[l], _layout_w_qb(w_qb[l]), wuk_t, cos128, sin128)
        ya = mla_attention_prompt(q_lat, q_rope, klat_bf, krope_bf, wuv_t,
                                  mixer_rows(MLA_HEADS * V_HEAD), batch, seq)
        ya = mla_attention_sample(q_lat, q_rope, cache_latent, cache_k_rope, l, klat_bf, krope_bf,
                                  wuv_t, ya, dbatch, blk_s)

        band_k = cols[:, COL_BK * SLOT:(COL_BK + 1) * SLOT]
        band_v = cols[:, COL_BV * SLOT:(COL_BV + 1) * SLOT]

        def prompt_rows(t):
            t = t[:mp].reshape(batch, seq, band_w).astype(BF16)
            return jnp.pad(t, ((0, 0), (BAND_ROWS, 0), (0, 0)))

        def sample_rows(t, cache):
            return jnp.concatenate([cache.reshape(dbatch, BAND_ROWS, band_w).astype(BF16),
                                    t[mp:].reshape(dbatch, dseq, band_w).astype(BF16)], axis=1)

        yb = band_attention(cols, prompt_rows(band_k), prompt_rows(band_v), rel_bias[l],
                            mixer_rows(band_w), batch, nchunks, 0, BAND_ROWS, math.gcd(nchunks, 4))
        yb = band_attention(cols, sample_rows(band_k, cache_band_k[l]),
                            sample_rows(band_v, cache_band_v[l]), rel_bias[l], yb, dbatch, 1, blk_s, 0, 1)

        conv_init = jnp.concatenate(
            [jnp.zeros((batch, 8, C_CONV), F32),
             jnp.pad(state_conv[l], ((0, 0), (8 - (CONV_W - 1), 0), (0, 0)))], axis=0)
        aneg = _pad_cols(-jnp.exp(a_log[l].astype(F32)).reshape(1, -1), LANE)
        dtb = _pad_cols(dt_bias[l].astype(F32).reshape(1, -1), LANE)
        pre = gdn_pre(cols, conv_init, conv_w[l], aneg, dtb, nchunks, mp // CHUNK)
        yc, delta_p = gdn_recurrence(pre, cols, zero_state, gdn_norm[l], mixer_rows(GDN_V),
                                     batch, nchunks, 0)
        yc, delta_s = gdn_recurrence(pre, cols, state_delta[l], gdn_norm[l], yc, dbatch, 1, blk_s)

        x = matmul_mix(ya, yb, yc, w_out, l, x, 544, 512)
        x = ffn(x, norm_ff2, ff2_w1, ff2_w3, ff2_w2, l)

        def conv_tail(first_row, rows_per_stream, streams):
            ends = [first_row + (s + 1) * rows_per_stream for s in range(streams)]
            return jnp.stack([cols[e - (CONV_W - 1):e, :C_CONV] for e in ends])

        keep = min(BAND_ROWS, seq)
        layer_out = (
            lat[:mp].reshape(batch, seq, KV_LORA),
            krope[:mp].reshape(batch, seq, QK_ROPE),
            band_k[:mp].reshape(batch, seq, BAND_HEADS, BAND_DIM)[:, seq - keep:],
            band_v[:mp].reshape(batch, seq, BAND_HEADS, BAND_DIM)[:, seq - keep:],
            conv_tail(0, seq, batch),
            delta_p,
            lat[mp:].reshape(dbatch, dseq, KV_LORA),
            krope[mp:].reshape(dbatch, dseq, QK_ROPE),
            band_k[mp:].reshape(dbatch, dseq, BAND_HEADS, BAND_DIM),
            band_v[mp:].reshape(dbatch, dseq, BAND_HEADS, BAND_DIM),
            conv_tail(mp, dseq, dbatch),
            delta_s,
        )
        for acc, t in zip(outs, layer_out):
            acc.append(t)

    y_prompt = rmsnorm(x, final_norm, F32, rows=mp).reshape(batch, seq, d)
    y_sample = rmsnorm(x, final_norm, F32, rows=ms, row_start=mp).reshape(dbatch, dseq, d)
    return (y_prompt, y_sample) + tuple(jnp.stack(t) for t in outs)
```

```python
import functools
import math

import jax
import jax.numpy as jnp
from jax import lax
from jax.experimental import pallas as pl
from jax.experimental.pallas import tpu as pltpu

F32 = jnp.float32
BF16 = jnp.bfloat16
HIGHEST = lax.Precision.HIGHEST

CHUNK = 64
CHUNK_SHIFT = 6
EPS = 1e-6
NEG = -1e30
Q_LORA = 1024
KV_LORA = 512
QK_NOPE = 128
QK_ROPE = 64
V_HEAD = 128
MLA_HEADS = 16
ROPE_THETA = 10000.0
MLA_SCALE = (QK_NOPE + QK_ROPE) ** -0.5
MLA_HEAD_GROUPS = 8
BAND_HEADS = 8
BAND_DIM = 128
BAND_PREV = 8
BAND_ROWS = BAND_PREV * CHUNK
BAND_KEYS = BAND_ROWS + CHUNK
REL_CLIP = 128
BAND_SCALE = BAND_DIM ** -0.5
GDN_HEADS = 8
GDN_GROUP = 8
GDN_DK = 128
GDN_DV = 128
CONV_W = 4
GDN_QK = GDN_HEADS * GDN_DK
GDN_V = GDN_HEADS * GDN_DV
C_CONV = 2 * GDN_QK + GDN_V
LANE = 128
SLOT = 1024
COL_U, COL_QA, COL_BQ, COL_BK, COL_BV, COL_Z, COL_TAIL = 0, 3, 4, 5, 6, 7, 8
IN_COLS_PAD = 9 * SLOT
TAIL_KR, TAIL_KSW, TAIL_A, TAIL_B = 512, 640, 768, 896

VMEM_LIMIT_BYTES = 58 * 1024 * 1024


def _params(*sem):
    return pltpu.CompilerParams(dimension_semantics=sem, vmem_limit_bytes=VMEM_LIMIT_BYTES)


def _tile(n, target, mult=16):
    best = None
    for t in range(mult, min(n, target) + 1, mult):
        if n % t == 0:
            best = t
    assert best is not None, (n, target)
    return best


def _dot(a, b, precision=None):
    return jnp.dot(a, b, preferred_element_type=F32, precision=precision)


def _dot_nt(a, b, precision=None):
    return lax.dot_general(a, b, (((1,), (1,)), ((), ())), preferred_element_type=F32,
                           precision=precision)


def _dot_tn(a, b, precision=None):
    return lax.dot_general(a, b, (((0,), (0,)), ((), ())), preferred_element_type=F32,
                           precision=precision)


def _split_bf16(x):
    hi = x.astype(BF16)
    return hi, (x - hi.astype(F32)).astype(BF16)


def _dot_3pass(a, b, expand=lambda t: t):
    a_hi, a_lo = _split_bf16(a)
    b_hi, b_lo = _split_bf16(b)
    m = a.shape[0]
    top = _dot(jnp.concatenate([a_hi, a_lo], axis=0), expand(b_hi))
    return top[:m] + top[m:] + _dot(a_hi, expand(b_lo))


def _sigmoid(x):
    return 1.0 / (1.0 + jnp.exp(-x))


def _rms(x, g):
    return x * lax.rsqrt(jnp.mean(x * x, axis=-1, keepdims=True) + EPS) * g


def _rmsnorm_kernel(x_ref, g_ref, o_ref):
    o_ref[...] = _rms(x_ref[...], g_ref[...]).astype(o_ref.dtype)


def rmsnorm(x, g, out_dtype, rows=None, row_start=0):
    m, d = x.shape
    rows = m if rows is None else rows
    tm = _tile(math.gcd(rows, row_start) if row_start else rows, 256, 8)
    off = row_start // tm
    return pl.pallas_call(
        _rmsnorm_kernel,
        grid=(rows // tm,),
        in_specs=[pl.BlockSpec((tm, d), lambda i: (i + off, 0)),
                  pl.BlockSpec((1, d), lambda i: (0, 0))],
        out_specs=pl.BlockSpec((tm, d), lambda i: (i, 0)),
        out_shape=jax.ShapeDtypeStruct((rows, d), out_dtype),
        compiler_params=_params("parallel"),
        name="rmsnorm",
    )(x, g.reshape(1, d))


def _pair_specs(first, second, tm, width, row_axis, col_fn=None):
    n_first = first.shape[0] // tm
    col = (lambda *g: 0) if col_fn is None else col_fn
    return (n_first,
            pl.BlockSpec((tm, width), lambda *g: (jnp.minimum(g[row_axis], n_first - 1), col(*g))),
            pl.BlockSpec((tm, width), lambda *g: (jnp.maximum(g[row_axis] - n_first, 0), col(*g))))


def _rmsnorm_pair_kernel(xp_ref, xs_ref, g_ref, o_ref, *, n_first):
    @pl.when(pl.program_id(0) < n_first)
    def _():
        o_ref[...] = _rms(xp_ref[...], g_ref[...]).astype(o_ref.dtype)

    @pl.when(pl.program_id(0) >= n_first)
    def _():
        o_ref[...] = _rms(xs_ref[...], g_ref[...]).astype(o_ref.dtype)


def rmsnorm_pair(xp, xs, g, out_dtype):
    d = xp.shape[1]
    tm = _tile(math.gcd(xp.shape[0], xs.shape[0]), 256, 8)
    n_first, spec_p, spec_s = _pair_specs(xp, xs, tm, d, 0)
    m = xp.shape[0] + xs.shape[0]
    return pl.pallas_call(
        functools.partial(_rmsnorm_pair_kernel, n_first=n_first),
        grid=(m // tm,),
        in_specs=[spec_p, spec_s, pl.BlockSpec((1, d), lambda i: (0, 0))],
        out_specs=pl.BlockSpec((tm, d), lambda i: (i, 0)),
        out_shape=jax.ShapeDtypeStruct((m, d), out_dtype),
        compiler_params=_params("parallel"),
        name="rmsnorm_pair",
    )(xp, xs, g.reshape(1, d))


def _mm_kernel(a_ref, w_ref, o_ref):
    o_ref[...] = _dot(a_ref[...], w_ref[...]).astype(o_ref.dtype)


def _cast_weight(w_ref, wb_ref):
    @pl.when(pl.program_id(1) == 0)
    def _():
        wb_ref[...] = w_ref[...].astype(BF16)


def _tiled_kernel(*refs, n_w, n_extra, n_side, is_f32, epilogue):
    a_ref, refs = refs[0], refs[1:]
    w_refs, refs = refs[:n_w], refs[n_w:]
    extra, refs = refs[:n_extra], refs[n_extra:]
    side_in, refs = refs[:n_side], refs[n_side:]
    o_ref, refs = refs[0], refs[1:]
    side_out, scratch = refs[:n_side], list(refs[n_side:])
    for s_in, s_out in zip(side_in, side_out):
        s_out[...] = s_in[...].astype(BF16)
    ws = []
    for w_ref, f32 in zip(w_refs, is_f32):
        if f32:
            wb_ref = scratch.pop(0)
            _cast_weight(w_ref, wb_ref)
            ws.append(wb_ref)
        else:
            ws.append(w_ref)
    o_ref[...] = epilogue(a_ref[...], ws, extra).astype(o_ref.dtype)


def _swiglu_epilogue(a, ws, extra):
    h1 = _dot(a, ws[0][...])
    h3 = _dot(a, ws[1][...])
    return h1 * _sigmoid(h1) * h3


def _residual_epilogue(a, ws, extra, *, scale):
    return extra[0][...] + scale * _dot(a, ws[0][...])


def _residual_pair_epilogue(a, ws, extra, *, scale, n_first):
    res = jnp.where(pl.program_id(1) < n_first, extra[0][...], extra[1][...])
    return res + scale * _dot(a, ws[0][...])


def _tiled_matmul(a, weights, extra, extra_specs, sides, epilogue, out_dtype, tm, tn, name):
    m, k = a.shape
    is_f32 = tuple(isinstance(w, tuple) for w in weights)
    n = weights[0][0].shape[2] if is_f32[0] else weights[0].shape[1]
    nj, ni = n // tn, m // tm
    w_args, w_specs = [], []
    for w, f32 in zip(weights, is_f32):
        if f32:
            w_args.append(w[0])
            w_specs.append(_layer_weight_spec(w[1], k, tn))
        else:
            w_args.append(w)
            w_specs.append(pl.BlockSpec((k, tn), lambda j, i: (0, j)))
    side_args, side_in_specs, side_out_specs, side_shapes = [], [], [], []
    for w, layer in sides:
        slabs = max(t for t in range(1, nj * ni + 1)
                    if w.shape[1] % t == 0 and (w.shape[1] // t) % 16 == 0)
        rows = w.shape[1] // slabs
        side_args.append(w)
        side_in_specs.append(pl.BlockSpec(
            (None, rows, w.shape[2]), functools.partial(_side_in_map, layer=layer, ni=ni, slabs=slabs)))
        side_out_specs.append(pl.BlockSpec(
            (rows, w.shape[2]), functools.partial(_side_out_map, ni=ni, slabs=slabs)))
        side_shapes.append(jax.ShapeDtypeStruct(w.shape[1:], BF16))
    outs = pl.pallas_call(
        functools.partial(_tiled_kernel, n_w=len(weights), n_extra=len(extra), n_side=len(sides),
                          is_f32=is_f32, epilogue=epilogue),
        grid=(nj, ni),
        in_specs=[pl.BlockSpec((tm, k), lambda j, i: (i, 0))] + w_specs + list(extra_specs) + side_in_specs,
        out_specs=[pl.BlockSpec((tm, tn), lambda j, i: (i, j))] + side_out_specs,
        out_shape=[jax.ShapeDtypeStruct((m, n), out_dtype)] + side_shapes,
        scratch_shapes=[pltpu.VMEM((k, tn), BF16) for f32 in is_f32 if f32],
        compiler_params=_params("arbitrary", "arbitrary"),
        name=name,
    )(a, *w_args, *extra, *side_args)
    return outs[0], list(outs[1:])


def _side_in_map(j, i, *, layer, ni, slabs):
    return (layer, jnp.minimum(j * ni + i, slabs - 1), 0)


def _side_out_map(j, i, *, ni, slabs):
    return (jnp.minimum(j * ni + i, slabs - 1), 0)


def _mm_mix_kernel(ya_ref, yb_ref, yc_ref, w_ref, r_ref, o_ref, wb_ref):
    _cast_weight(w_ref, wb_ref)
    ka, kb = ya_ref.shape[1], yb_ref.shape[1]
    o_ref[...] = (r_ref[...] + _dot(ya_ref[...], wb_ref[0:ka, :])
                  + _dot(yb_ref[...], wb_ref[ka:ka + kb, :]) + _dot(yc_ref[...], wb_ref[ka + kb:, :]))


def matmul(a, w, out_dtype, tm_target, tn_target):
    m, k = a.shape
    n = w.shape[1]
    tm, tn = _tile(m, tm_target), _tile(n, tn_target, LANE)
    return pl.pallas_call(
        _mm_kernel,
        grid=(n // tn, m // tm),
        in_specs=[pl.BlockSpec((tm, k), lambda j, i: (i, 0)),
                  pl.BlockSpec((k, tn), lambda j, i: (0, j))],
        out_specs=pl.BlockSpec((tm, tn), lambda j, i: (i, j)),
        out_shape=jax.ShapeDtypeStruct((m, n), out_dtype),
        compiler_params=_params("parallel", "parallel"),
        name="matmul",
    )(a, w)


def _layer_weight_spec(layer, k, tn):
    return pl.BlockSpec((None, k, tn), lambda j, i: (layer, 0, j))


def matmul_residual(a, w, res, scale, tm_target, tn_target, sides=()):
    n = res.shape[1]
    tm, tn = _tile(a.shape[0], tm_target), _tile(n, tn_target, LANE)
    return _tiled_matmul(a, [w], [res], [pl.BlockSpec((tm, tn), lambda j, i: (i, j))], sides,
                         functools.partial(_residual_epilogue, scale=scale), F32, tm, tn,
                         "matmul_residual")


def matmul_residual_pair(a, w, res_p, res_s, scale, tn_target, sides=()):
    tm = _tile(math.gcd(res_p.shape[0], res_s.shape[0]), 256)
    tn = _tile(res_p.shape[1], tn_target, LANE)
    n_first, spec_p, spec_s = _pair_specs(res_p, res_s, tm, tn, 1, lambda j, i: j)
    return _tiled_matmul(a, [w], [res_p, res_s], [spec_p, spec_s], sides,
                         functools.partial(_residual_pair_epilogue, scale=scale, n_first=n_first),
                         F32, tm, tn, "matmul_residual_pair")


def matmul_mix(ya, yb, yc, w, layer, res, tm_target, tn_target):
    m = ya.shape[0]
    k, n = w.shape[1], w.shape[2]
    tm, tn = _tile(m, tm_target), _tile(n, tn_target, LANE)
    row = lambda y: pl.BlockSpec((tm, y.shape[1]), lambda j, i: (i, 0))
    return pl.pallas_call(
        _mm_mix_kernel,
        grid=(n // tn, m // tm),
        in_specs=[row(ya), row(yb), row(yc), _layer_weight_spec(layer, k, tn),
                  pl.BlockSpec((tm, tn), lambda j, i: (i, j))],
        out_specs=pl.BlockSpec((tm, tn), lambda j, i: (i, j)),
        out_shape=jax.ShapeDtypeStruct((m, n), F32),
        scratch_shapes=[pltpu.VMEM((k, tn), BF16)],
        compiler_params=_params("parallel", "arbitrary"),
        name="matmul_mix",
    )(ya, yb, yc, w, res)


def matmul_swiglu(a, w1, w3, tm_target, tn_target, sides=()):
    n = w1[0].shape[2] if isinstance(w1, tuple) else w1.shape[1]
    tm, tn = _tile(a.shape[0], tm_target), _tile(n, tn_target, LANE)
    return _tiled_matmul(a, [w1, w3], [], [], sides, _swiglu_epilogue, BF16, tm, tn, "matmul_swiglu")


def _mla_prep_kernel(qa_ref, tail_ref, qn_ref, kvn_ref, wq_ref, wuk_ref, cos_ref, sin_ref,
                     qlat_ref, qrope_ref, lat_ref, krope_ref, klat_bf_ref, krope_bf_ref):
    hq = _rms(qa_ref[...], qn_ref[...]).astype(BF16)
    q_all = _dot(hq, wq_ref[...])
    cos = cos_ref[...]
    sin = sin_ref[...]
    hw = MLA_HEADS * LANE
    for h in range(MLA_HEADS):
        qn = q_all[:, h * LANE:(h + 1) * LANE].astype(BF16)
        qlat_ref[h] = (_dot(qn, wuk_ref[h]) * MLA_SCALE).astype(BF16)
        qr = (q_all[:, hw + h * LANE:hw + (h + 1) * LANE] * cos
              + q_all[:, 2 * hw + h * LANE:2 * hw + (h + 1) * LANE] * sin)
        qrope_ref[h] = (qr[:, :QK_ROPE] * MLA_SCALE).astype(BF16)
    tail = tail_ref[...]
    lat = _rms(tail[:, :KV_LORA], kvn_ref[...])
    lat_ref[...] = lat
    klat_bf_ref[...] = lat.astype(BF16)
    kr = tail[:, TAIL_KR:TAIL_KR + LANE] * cos + tail[:, TAIL_KSW:TAIL_KSW + LANE] * sin
    krope_ref[...] = kr[:, :QK_ROPE]
    krope_bf_ref[...] = kr[:, :QK_ROPE].astype(BF16)


def mla_prep(cols, q_norm, kv_norm, wq_all, wuk_t, cos128, sin128):
    m = cols.shape[0]
    tm = _tile(m, 256)
    h = MLA_HEADS
    const2 = lambda i: (0, 0)
    return pl.pallas_call(
        _mla_prep_kernel,
        grid=(m // tm,),
        in_specs=[pl.BlockSpec((tm, SLOT), lambda i: (i, COL_QA)),
                  pl.BlockSpec((tm, SLOT), lambda i: (i, COL_TAIL)),
                  pl.BlockSpec((1, Q_LORA), const2),
                  pl.BlockSpec((1, KV_LORA), const2),
                  pl.BlockSpec(wq_all.shape, const2, pipeline_mode=pl.Buffered(1)),
                  pl.BlockSpec(wuk_t.shape, lambda i: (0, 0, 0), pipeline_mode=pl.Buffered(1)),
                  pl.BlockSpec((tm, LANE), lambda i: (i, 0)),
                  pl.BlockSpec((tm, LANE), lambda i: (i, 0))],
        out_specs=[pl.BlockSpec((h, tm, KV_LORA), lambda i: (0, i, 0)),
                   pl.BlockSpec((h, tm, QK_ROPE), lambda i: (0, i, 0)),
                   pl.BlockSpec((tm, KV_LORA), lambda i: (i, 0)),
                   pl.BlockSpec((tm, QK_ROPE), lambda i: (i, 0)),
                   pl.BlockSpec((tm, KV_LORA), lambda i: (i, 0)),
                   pl.BlockSpec((tm, QK_ROPE), lambda i: (i, 0))],
        out_shape=[jax.ShapeDtypeStruct((h, m, KV_LORA), BF16),
                   jax.ShapeDtypeStruct((h, m, QK_ROPE), BF16),
                   jax.ShapeDtypeStruct((m, KV_LORA), F32),
                   jax.ShapeDtypeStruct((m, QK_ROPE), F32),
                   jax.ShapeDtypeStruct((m, KV_LORA), BF16),
                   jax.ShapeDtypeStruct((m, QK_ROPE), BF16)],
        compiler_params=_params("parallel"),
        name="mla_prep",
    )(cols, cols, q_norm.reshape(1, -1), kv_norm.reshape(1, -1), wq_all, wuk_t, cos128, sin128)


def _softmax_init(m_ref, l_ref, acc_ref):
    m_ref[...] = jnp.full(m_ref.shape, NEG, F32)
    l_ref[...] = jnp.zeros(l_ref.shape, F32)
    acc_ref[...] = jnp.zeros(acc_ref.shape, F32)


def _mla_step(ql_ref, qr_ref, k_l, k_r, m_ref, l_ref, acc_ref, tq, visible=None):
    hg = MLA_HEADS // MLA_HEAD_GROUPS
    rows = hg * tq

    def scores(g):
        q_l = ql_ref[g * hg:(g + 1) * hg].reshape(rows, KV_LORA)
        q_r = qr_ref[g * hg:(g + 1) * hg].reshape(rows, QK_ROPE)
        s = _dot_nt(q_l, k_l) + _dot_nt(q_r, k_r)
        return s if visible is None else jnp.where(visible, s, NEG)

    def update(g, s):
        rs = slice(g * rows, (g + 1) * rows)
        m_old = m_ref[rs]
        m_new = jnp.maximum(m_old, jnp.max(s, axis=-1, keepdims=True))
        alpha = jnp.exp(m_old - m_new)
        p = jnp.exp(s - m_new)
        l_ref[rs] = alpha * l_ref[rs] + jnp.sum(p, axis=-1, keepdims=True)
        acc_ref[rs] = alpha * acc_ref[rs] + _dot(p.astype(BF16), k_l)
        m_ref[rs] = m_new

    s = scores(0)
    for g in range(1, MLA_HEAD_GROUPS):
        s_next = scores(g)
        update(g - 1, s)
        s = s_next
    update(MLA_HEAD_GROUPS - 1, s)


def _mla_finish(wuv_ref, o_ref, l_ref, acc_ref, tq):
    ctx = acc_ref[...] / l_ref[...]
    for h in range(MLA_HEADS):
        ch = ctx[h * tq:(h + 1) * tq].astype(BF16)
        o_ref[:, h * V_HEAD:(h + 1) * V_HEAD] = _dot(ch, wuv_ref[h]).astype(o_ref.dtype)


def _mla_prompt_kernel(ql_ref, qr_ref, kl_ref, kr_ref, wuv_ref, y_prev_ref, o_ref,
                       m_ref, l_ref, acc_ref, *, tq, tk):
    del y_prev_ref
    qi = pl.program_id(1)
    kj = pl.program_id(2)
    last = (qi * tq + tq - 1) // tk

    @pl.when(kj == 0)
    def _():
        _softmax_init(m_ref, l_ref, acc_ref)

    @pl.when(kj < last)
    def _():
        _mla_step(ql_ref, qr_ref, kl_ref[...], kr_ref[...], m_ref, l_ref, acc_ref, tq)

    @pl.when(kj == last)
    def _():
        shape = (MLA_HEADS // MLA_HEAD_GROUPS * tq, tk)
        row = lax.broadcasted_iota(jnp.int32, shape, 0)
        col = lax.broadcasted_iota(jnp.int32, shape, 1)
        q_chunk = (qi * tq + row % tq) // CHUNK
        k_chunk = (kj * tk + col) // CHUNK
        _mla_step(ql_ref, qr_ref, kl_ref[...], kr_ref[...], m_ref, l_ref, acc_ref, tq,
                  visible=k_chunk <= q_chunk)

    @pl.when(kj == pl.num_programs(2) - 1)
    def _():
        _mla_finish(wuv_ref, o_ref, l_ref, acc_ref, tq)


def mla_attention_prompt(q_lat, q_rope, k_lat, k_rope, wuv, y_prev, batch, seq):
    tq = 128
    tk = _tile(seq, 512, CHUNK)
    nq, nk = seq // tq, seq // tk
    h = MLA_HEADS
    rows = h * tq

    def k_map(b, qi, kj):
        return (b * nk + jnp.minimum(kj, (qi * tq + tq - 1) // tk), 0)

    return pl.pallas_call(
        functools.partial(_mla_prompt_kernel, tq=tq, tk=tk),
        grid=(batch, nq, nk),
        in_specs=[pl.BlockSpec((h, tq, KV_LORA), lambda b, qi, kj: (0, b * nq + qi, 0)),
                  pl.BlockSpec((h, tq, QK_ROPE), lambda b, qi, kj: (0, b * nq + qi, 0)),
                  pl.BlockSpec((tk, KV_LORA), k_map),
                  pl.BlockSpec((tk, QK_ROPE), k_map),
                  pl.BlockSpec(wuv.shape, lambda b, qi, kj: (0, 0, 0)),
                  pl.BlockSpec(memory_space=pl.ANY)],
        out_specs=pl.BlockSpec((tq, h * V_HEAD), lambda b, qi, kj: (b * nq + qi, 0)),
        out_shape=jax.ShapeDtypeStruct(y_prev.shape, y_prev.dtype),
        input_output_aliases={5: 0},
        scratch_shapes=[pltpu.VMEM((rows, 1), F32), pltpu.VMEM((rows, 1), F32),
                        pltpu.VMEM((rows, KV_LORA), F32)],
        compiler_params=_params("parallel", "parallel", "arbitrary"),
        name="mla_attention_prompt",
    )(q_lat, q_rope, k_lat, k_rope, wuv, y_prev)


def _mla_sample_kernel(ql_ref, qr_ref, pl_ref, pr_ref, kl_ref, kr_ref, wuv_ref, y_prev_ref, o_ref,
                       m_ref, l_ref, acc_ref):
    del y_prev_ref
    kj = pl.program_id(1)
    n_past = pl.num_programs(1) - 1

    @pl.when(kj == 0)
    def _():
        _softmax_init(m_ref, l_ref, acc_ref)

    @pl.when(kj < n_past)
    def _():
        _mla_step(ql_ref, qr_ref, pl_ref[0].astype(BF16), pr_ref[0].astype(BF16),
                  m_ref, l_ref, acc_ref, CHUNK)

    @pl.when(kj == n_past)
    def _():
        _mla_step(ql_ref, qr_ref, kl_ref[...], kr_ref[...], m_ref, l_ref, acc_ref, CHUNK)
        _mla_finish(wuv_ref, o_ref, l_ref, acc_ref, CHUNK)


def mla_attention_sample(q_lat, q_rope, past_lat, past_rope, layer, k_lat, k_rope, wuv, y_prev,
                         batch, tok_block0):
    past = past_lat.shape[2]
    tk = _tile(past, 512, CHUNK)
    n_past = past // tk
    h = MLA_HEADS
    rows = h * CHUNK
    past_map = lambda b, kj: (layer, b, jnp.minimum(kj, n_past - 1), 0)
    return pl.pallas_call(
        _mla_sample_kernel,
        grid=(batch, n_past + 1),
        in_specs=[pl.BlockSpec((h, CHUNK, KV_LORA), lambda b, kj: (0, tok_block0 + b, 0)),
                  pl.BlockSpec((h, CHUNK, QK_ROPE), lambda b, kj: (0, tok_block0 + b, 0)),
                  pl.BlockSpec((None, 1, tk, KV_LORA), past_map),
                  pl.BlockSpec((None, 1, tk, QK_ROPE), past_map),
                  pl.BlockSpec((CHUNK, KV_LORA), lambda b, kj: (tok_block0 + b, 0)),
                  pl.BlockSpec((CHUNK, QK_ROPE), lambda b, kj: (tok_block0 + b, 0)),
                  pl.BlockSpec(wuv.shape, lambda b, kj: (0, 0, 0)),
                  pl.BlockSpec(memory_space=pl.ANY)],
        out_specs=pl.BlockSpec((CHUNK, h * V_HEAD), lambda b, kj: (tok_block0 + b, 0)),
        out_shape=jax.ShapeDtypeStruct(y_prev.shape, y_prev.dtype),
        input_output_aliases={7: 0},
        scratch_shapes=[pltpu.VMEM((rows, 1), F32), pltpu.VMEM((rows, 1), F32),
                        pltpu.VMEM((rows, KV_LORA), F32)],
        compiler_params=_params("parallel", "arbitrary"),
        name="mla_attention_sample",
    )(q_lat, q_rope, past_lat, past_rope, k_lat, k_rope, wuv, y_prev)


def _band_span(qchunks):
    tq, nk = qchunks * CHUNK, (qchunks + BAND_PREV) * CHUNK
    return -(-(tq + nk - 1) // LANE) * LANE


def _band_kernel(q_ref, k_ref, v_ref, g_ref, y_prev_ref, o_ref, bias_ref, *, pad_rows, qchunks):
    del y_prev_ref
    n = pl.program_id(1)
    tq, nk = qchunks * CHUNK, (qchunks + BAND_PREV) * CHUNK
    span = _band_span(qchunks)

    @pl.when(jnp.logical_and(pl.program_id(0) == 0, n == 0))
    def _():
        for h in range(BAND_HEADS):
            rows = jnp.broadcast_to(g_ref[h:h + 1, :], (tq, span))
            rolled = pltpu.roll(rows, span - (tq - 1), 1, stride=1, stride_axis=0)
            bias_ref[h] = rolled[:, :nk]

    start = pl.multiple_of(n * tq, CHUNK)
    row = lax.broadcasted_iota(jnp.int32, (tq, nk), 0)
    col = lax.broadcasted_iota(jnp.int32, (tq, nk), 1)
    visible = start + col >= pad_rows
    if qchunks > 1:
        ahead = (col >> CHUNK_SHIFT) - (row >> CHUNK_SHIFT)
        visible = visible & (ahead >= 0) & (ahead <= BAND_PREV)
    q = q_ref[...]
    for h in range(BAND_HEADS):
        lanes = slice(h * BAND_DIM, (h + 1) * BAND_DIM)
        qh = q[:, lanes].astype(BF16)
        kh = k_ref[0, pl.ds(start, nk), lanes]
        vh = v_ref[0, pl.ds(start, nk), lanes]
        s = _dot_nt(qh, kh) * BAND_SCALE + bias_ref[h]
        if pad_rows or qchunks > 1:
            s = jnp.where(visible, s, NEG)
        p = jnp.exp(s - jnp.max(s, axis=-1, keepdims=True))
        o = _dot(p.astype(BF16), vh) / jnp.sum(p, axis=-1, keepdims=True)
        o_ref[:, lanes] = o.astype(o_ref.dtype)


def band_attention(cols, k_rows, v_rows, rel_bias, y_prev, batch, nchunks, tok_block0, pad_rows, qchunks):
    assert nchunks % qchunks == 0 and tok_block0 % qchunks == 0
    width = BAND_HEADS * BAND_DIM
    rows = k_rows.shape[1]
    tq, nk = qchunks * CHUNK, (qchunks + BAND_PREV) * CHUNK
    nblk = nchunks // qchunks
    bias_row = _band_bias_row(rel_bias, qchunks)
    tok = lambda b, n: tok_block0 // qchunks + b * nblk + n
    return pl.pallas_call(
        functools.partial(_band_kernel, pad_rows=pad_rows, qchunks=qchunks),
        grid=(batch, nblk),
        in_specs=[pl.BlockSpec((tq, width), lambda b, n: (tok(b, n), COL_BQ)),
                  pl.BlockSpec((1, rows, width), lambda b, n: (b, 0, 0)),
                  pl.BlockSpec((1, rows, width), lambda b, n: (b, 0, 0)),
                  pl.BlockSpec(bias_row.shape, lambda b, n: (0, 0)),
                  pl.BlockSpec(memory_space=pl.ANY)],
        out_specs=pl.BlockSpec((tq, width), lambda b, n: (tok(b, n), 0)),
        out_shape=jax.ShapeDtypeStruct(y_prev.shape, y_prev.dtype),
        input_output_aliases={4: 0},
        scratch_shapes=[pltpu.VMEM((BAND_HEADS, tq, nk), F32)],
        compiler_params=_params("arbitrary", "arbitrary"),
        name="band_attention",
    )(cols, k_rows, v_rows, bias_row, y_prev)


def _expand_matrix(rows, first, width, per_shift):
    r = lax.broadcasted_iota(jnp.int32, (rows, width), 0)
    c = lax.broadcasted_iota(jnp.int32, (rows, width), 1)
    return (r == first + (c >> per_shift)).astype(F32)


def _unit_lower_inverse(a, row, col, group):
    gl = a.shape[1]
    r = lax.broadcasted_iota(jnp.int32, (gl, gl), 0)
    c = lax.broadcasted_iota(jnp.int32, (gl, gl), 1)
    diag_blocks = ((r >> CHUNK_SHIFT) == (c >> CHUNK_SHIFT)).astype(BF16)

    def block_diag(w):
        return jnp.concatenate([w] * group, axis=0) * diag_blocks

    t = (row == col).astype(F32) - jnp.where((row >> 1) == (col >> 1), a, 0.0)
    for shift in range(1, CHUNK_SHIFT):
        lower_left = ((row >> (shift + 1)) == (col >> (shift + 1))) & ((row >> shift) != (col >> shift))
        x = _dot_3pass(t, jnp.where(lower_left, a, 0.0), block_diag)
        t = t - _dot_3pass(x, t, block_diag)
    return t


def _l2norm_heads(x, heads, width):
    parts = []
    for h in range(heads):
        xh = x[:, h * width:(h + 1) * width]
        parts.append(xh * lax.rsqrt(jnp.sum(xh * xh, axis=-1, keepdims=True) + EPS))
    return jnp.concatenate(parts, axis=1)


def _gdn_pre_kernel(u_ref, uprev_ref, init_ref, tail_ref, cw_ref, aneg_ref, dtb_ref,
                    uv_ref, wk_ref, attn_ref, qg_ref, kt_ref, gl_ref, ext_ref,
                    *, chunks_per_seq, n_prompt_chunks):
    group = GDN_GROUP
    gw = group * CHUNK
    c = pl.program_id(0)
    is_start = jnp.logical_or(c >= n_prompt_chunks, c % chunks_per_seq == 0)
    ext_ref[0:8, :] = jnp.where(is_start, init_ref[0], uprev_ref[...])
    ext_ref[8:8 + CHUNK, :] = u_ref[...]
    cw = cw_ref[...]
    uc = (cw[3:4] * ext_ref[8:8 + CHUNK, :] + cw[2:3] * ext_ref[7:7 + CHUNK, :]
          + cw[1:2] * ext_ref[6:6 + CHUNK, :] + cw[0:1] * ext_ref[5:5 + CHUNK, :])
    uc = uc * _sigmoid(uc)

    tail = tail_ref[...]
    xa = tail[:, TAIL_A:TAIL_A + LANE] + dtb_ref[...]
    softplus = jnp.maximum(xa, 0.0) + jnp.log(1.0 + jnp.exp(-jnp.abs(xa)))
    g = aneg_ref[...] * softplus
    beta = _sigmoid(tail[:, TAIL_B:TAIL_B + LANE])

    r64 = lax.broadcasted_iota(jnp.int32, (CHUNK, CHUNK), 0)
    c64 = lax.broadcasted_iota(jnp.int32, (CHUNK, CHUNK), 1)
    gc = _dot((r64 >= c64).astype(F32), g, HIGHEST)
    egc = jnp.exp(gc)
    ekt = jnp.exp(gc[CHUNK - 1:CHUNK, :] - gc)
    e128 = _expand_matrix(LANE, 0, GDN_HEADS * GDN_DK, 7)
    beta_x = _dot(beta, e128, HIGHEST)
    egc_x = _dot(egc, e128, HIGHEST)
    ekt_x = _dot(ekt, e128, HIGHEST)
    gl_ref[0] = egc_x[CHUNK - 1:CHUNK, :]

    qn = _l2norm_heads(uc[:, :GDN_QK], GDN_HEADS, GDN_DK) * (GDN_DK ** -0.5)
    kn = _l2norm_heads(uc[:, GDN_QK:2 * GDN_QK], GDN_HEADS, GDN_DK)
    v_all = uc[:, 2 * GDN_QK:]
    qg_ref[...] = (qn * egc_x).astype(BF16)
    kt_ref[...] = (kn * ekt_x).astype(BF16)
    rhs_v = beta_x * v_all
    rhs_k = beta_x * egc_x * kn

    row = lax.broadcasted_iota(jnp.int32, (CHUNK, gw), 0)
    col = lax.broadcasted_iota(jnp.int32, (CHUNK, gw), 1) & (CHUNK - 1)
    incl = row >= col
    head_r = lax.broadcasted_iota(jnp.int32, (gw, group * GDN_DK), 0) >> CHUNK_SHIFT
    head_c = lax.broadcasted_iota(jnp.int32, (gw, group * GDN_DK), 1) >> 7
    for grp in range(GDN_HEADS // group):
        lanes = slice(grp * group * GDN_DK, (grp + 1) * group * GDN_DK)
        e64 = _expand_matrix(LANE, grp * group, gw, CHUNK_SHIFT)
        gcw = _dot(gc, e64, HIGHEST)
        bw = _dot(beta, e64, HIGHEST)
        gc_t = jnp.sum(jnp.where(row == col, gcw, 0.0), axis=0, keepdims=True)
        decay = jnp.where(incl, jnp.exp(jnp.where(incl, gcw - gc_t, 0.0)), 0.0)
        kg = kn[:, lanes]
        k_rows = jnp.where(head_r == head_c, jnp.concatenate([kg] * group, axis=0), 0.0).astype(BF16)
        kk = _dot_nt(kg.astype(BF16), k_rows)
        qk = _dot_nt(qn[:, lanes].astype(BF16), k_rows)
        a_mat = jnp.where(row > col, bw * kk * decay, 0.0)
        t_inv = _unit_lower_inverse(a_mat, row, col, group)
        attn_ref[:, grp * gw:(grp + 1) * gw] = (qk * decay).astype(BF16)
        for hh in range(group):
            h = grp * group + hh
            hl = slice(h * GDN_DK, (h + 1) * GDN_DK)
            rhs = jnp.concatenate([rhs_v[:, hl], rhs_k[:, hl]], axis=1)
            sol = _dot_3pass(t_inv[:, hh * CHUNK:(hh + 1) * CHUNK], rhs)
            uv_ref[:, hl] = sol[:, :GDN_DV]
            wk_ref[:, hl] = sol[:, GDN_DV:].astype(BF16)


def gdn_pre(cols, conv_init, conv_w, aneg, dtb, chunks_per_seq, n_prompt_chunks):
    m = cols.shape[0]
    nch = m // CHUNK
    h = GDN_HEADS

    def init_map(c):
        return (jnp.where(c < n_prompt_chunks, c // chunks_per_seq,
                          n_prompt_chunks // chunks_per_seq + c - n_prompt_chunks), 0, 0)

    tok_spec = lambda d: pl.BlockSpec((CHUNK, d), lambda c: (c, 0))
    return pl.pallas_call(
        functools.partial(_gdn_pre_kernel, chunks_per_seq=chunks_per_seq,
                          n_prompt_chunks=n_prompt_chunks),
        grid=(nch,),
        in_specs=[pl.BlockSpec((CHUNK, C_CONV), lambda c: (c, COL_U)),
                  pl.BlockSpec((8, C_CONV), lambda c: (jnp.maximum(c * (CHUNK // 8) - 1, 0), COL_U)),
                  pl.BlockSpec((1, 8, C_CONV), init_map),
                  pl.BlockSpec((CHUNK, SLOT), lambda c: (c, COL_TAIL)),
                  pl.BlockSpec((CONV_W, C_CONV), lambda c: (0, 0)),
                  pl.BlockSpec((1, LANE), lambda c: (0, 0)),
                  pl.BlockSpec((1, LANE), lambda c: (0, 0))],
        out_specs=[tok_spec(GDN_V), tok_spec(GDN_QK), tok_spec(h * CHUNK), tok_spec(GDN_QK),
                   tok_spec(GDN_QK), pl.BlockSpec((1, 1, GDN_QK), lambda c: (c, 0, 0))],
        out_shape=[jax.ShapeDtypeStruct((m, GDN_V), F32),
                   jax.ShapeDtypeStruct((m, GDN_QK), BF16),
                   jax.ShapeDtypeStruct((m, h * CHUNK), BF16),
                   jax.ShapeDtypeStruct((m, GDN_QK), BF16),
                   jax.ShapeDtypeStruct((m, GDN_QK), BF16),
                   jax.ShapeDtypeStruct((nch, 1, GDN_QK), F32)],
        scratch_shapes=[pltpu.VMEM((8 + CHUNK, C_CONV), F32)],
        compiler_params=_params("parallel"),
        name="gdn_pre",
    )(cols, cols, conv_init, cols, conv_w, aneg, dtb)


def _gdn_rec_kernel(uv_ref, wk_ref, attn_ref, qg_ref, kt_ref, gl_ref, s0_ref, z_ref, gn_ref,
                    y_prev_ref, y_ref, s_ref):
    del y_prev_ref

    @pl.when(pl.program_id(1) == 0)
    def _():
        s_ref[...] = s0_ref[...]

    z = z_ref[...]
    gn = gn_ref[...]
    heads = range(GDN_HEADS)
    lanes = [slice(h * GDN_DK, (h + 1) * GDN_DK) for h in heads]
    s_old = [s_ref[0, h] for h in heads]
    s_bf = [s.astype(BF16) for s in s_old]
    ws = [_dot(wk_ref[:, lanes[h]], s_bf[h]) for h in heads]
    qs = [_dot(qg_ref[:, lanes[h]], s_bf[h]) for h in heads]
    u_bf = [(uv_ref[:, lanes[h]] - ws[h]).astype(BF16) for h in heads]
    outs = [qs[h] + _dot(attn_ref[:, h * CHUNK:(h + 1) * CHUNK], u_bf[h]) for h in heads]
    for h in heads:
        s_ref[0, h] = gl_ref[0, :, lanes[h]] * s_old[h] + _dot_tn(kt_ref[:, lanes[h]], u_bf[h])
    for h in heads:
        zh = z[:, lanes[h]]
        y_ref[:, lanes[h]] = (_rms(outs[h], gn) * (zh * _sigmoid(zh))).astype(y_ref.dtype)


def gdn_recurrence(pre, cols, s0, gdn_norm, y_prev, batch, nchunks, tok_block0):
    uv, wk, attn, qg, kt, gl = pre
    h = GDN_HEADS
    tok = lambda b, n: tok_block0 + b * nchunks + n
    tok_spec = lambda d: pl.BlockSpec((CHUNK, d), lambda b, n: (tok(b, n), 0))
    state_spec = pl.BlockSpec((1, h, GDN_DK, GDN_DV), lambda b, n: (b, 0, 0, 0))
    return pl.pallas_call(
        _gdn_rec_kernel,
        grid=(batch, nchunks),
        in_specs=[tok_spec(GDN_V), tok_spec(GDN_QK), tok_spec(h * CHUNK), tok_spec(GDN_QK),
                  tok_spec(GDN_QK),
                  pl.BlockSpec((1, 1, GDN_QK), lambda b, n: (tok(b, n), 0, 0)),
                  state_spec,
                  pl.BlockSpec((CHUNK, SLOT), lambda b, n: (tok(b, n), COL_Z)),
                  pl.BlockSpec((1, GDN_DV), lambda b, n: (0, 0)),
                  pl.BlockSpec(memory_space=pl.ANY)],
        out_specs=[tok_spec(GDN_V), state_spec],
        out_shape=[jax.ShapeDtypeStruct(y_prev.shape, y_prev.dtype),
                   jax.ShapeDtypeStruct((batch, h, GDN_DK, GDN_DV), F32)],
        input_output_aliases={9: 0},
        compiler_params=_params("parallel", "arbitrary"),
        name="gdn_recurrence",
    )(uv, wk, attn, qg, kt, gl, s0, cols, gdn_norm.reshape(1, -1), y_prev)


def _pad_cols(w, width):
    return jnp.pad(w, ((0, 0), (0, width - w.shape[1])))


def _layout_w_in(w):
    a0, b0 = 0, Q_LORA + KV_LORA + QK_ROPE
    c0 = b0 + 3 * BAND_HEADS * BAND_DIM
    q_a = w[:, a0:a0 + Q_LORA]
    c_kv = w[:, a0 + Q_LORA:a0 + Q_LORA + KV_LORA]
    k_r = w[:, a0 + Q_LORA + KV_LORA:b0]
    band = w[:, b0:c0]
    u = w[:, c0:c0 + C_CONV]
    z = w[:, c0 + C_CONV:c0 + C_CONV + GDN_V]
    a = w[:, c0 + C_CONV + GDN_V:c0 + C_CONV + GDN_V + GDN_HEADS]
    b = w[:, c0 + C_CONV + GDN_V + GDN_HEADS:]
    half = QK_ROPE // 2
    k_sw = jnp.concatenate([k_r[:, half:], k_r[:, :half]], axis=1)
    tail = jnp.concatenate([c_kv, _pad_cols(k_r, LANE), _pad_cols(k_sw, LANE),
                            _pad_cols(a, LANE), _pad_cols(b, LANE)], axis=1)
    return jnp.concatenate([u, q_a, band, z, tail], axis=1).astype(BF16)


def _layout_w_qb(w):
    k = w.shape[0]
    w = w.reshape(k, MLA_HEADS, QK_NOPE + QK_ROPE)
    nope = w[:, :, :QK_NOPE]
    rope = w[:, :, QK_NOPE:]
    half = QK_ROPE // 2
    rope_sw = jnp.concatenate([rope[:, :, half:], rope[:, :, :half]], axis=2)
    pad = lambda t: jnp.pad(t, ((0, 0), (0, 0), (0, LANE - QK_ROPE)))
    out = jnp.concatenate([nope, pad(rope), pad(rope_sw)], axis=1)
    return out.reshape(k, 3 * MLA_HEADS * LANE).astype(BF16)


def _rope_tables(pos):
    inv = 1.0 / (ROPE_THETA ** (jnp.arange(0, QK_ROPE, 2, dtype=F32) / QK_ROPE))
    ang = pos.astype(F32)[:, None] * inv[None, :]
    cos, sin = jnp.cos(ang), jnp.sin(ang)
    zeros = jnp.zeros((pos.shape[0], LANE - QK_ROPE), F32)
    return (jnp.concatenate([cos, cos, zeros], axis=1), jnp.concatenate([-sin, sin, zeros], axis=1))


def _band_bias_row(rel_bias, qchunks):
    heads = rel_bias.shape[0]
    span = _band_span(qchunks)
    left = BAND_ROWS + qchunks * CHUNK - 1 - REL_CLIP
    right = max(span - left - (2 * REL_CLIP + 1), 0)
    row = jnp.concatenate([jnp.broadcast_to(rel_bias[:, -1:], (heads, left)), rel_bias[:, ::-1],
                           jnp.broadcast_to(rel_bias[:, :1], (heads, right))], axis=1)
    return row[:, :span]


def kernel(x_prompt, x_sample, cache_latent, cache_k_rope, cache_band_k, cache_band_v, state_conv, state_delta, norm_ff1, ff1_w1, ff1_w3, ff1_w2, norm_mix, w_in, q_norm, w_qb, kv_norm, w_uk, w_uv, rel_bias, conv_w, a_log, dt_bias, gdn_norm, w_out, norm_ff2, ff2_w1, ff2_w3, ff2_w2, final_norm):
    batch, seq, d = x_prompt.shape
    dbatch, dseq, _ = x_sample.shape
    depth = w_in.shape[0]
    past = cache_latent.shape[2]
    assert dseq == CHUNK and seq % CHUNK == 0 and cache_band_k.shape[2] == BAND_ROWS
    mp, ms = batch * seq, dbatch * dseq
    m = mp + ms
    nchunks = seq // CHUNK
    blk_s = mp // CHUNK
    band_w = BAND_HEADS * BAND_DIM

    xp, xs = x_prompt.reshape(mp, d), x_sample.reshape(ms, d)
    pos = jnp.concatenate([jnp.tile(jnp.arange(seq, dtype=jnp.int32), batch),
                           jnp.tile(past + jnp.arange(dseq, dtype=jnp.int32), dbatch)])
    cos128, sin128 = _rope_tables(pos)
    zero_state = jnp.zeros((batch, GDN_HEADS, GDN_DK, GDN_DV), F32)

    def ffn(x, norm, w13_bf, w2, l, next_w13):
        h = rmsnorm(x, norm[l], BF16)
        hid, (w2_bf,) = matmul_swiglu(h, w13_bf[0], w13_bf[1], 544, 1024, sides=[(w2, l)])
        return matmul_residual(hid, w2_bf, x, 0.5, 272, 1024, sides=next_w13)

    def mixer_rows(width):
        return jnp.zeros((m, width), BF16)

    outs = [[] for _ in range(12)]
    for l in range(depth):
        if l == 0:
            hid, (w2_bf,) = matmul_swiglu(rmsnorm_pair(xp, xs, norm_ff1[0], BF16), (ff1_w1, 0),
                                          (ff1_w3, 0), 544, 512, sides=[(ff1_w2, 0)])
            x, w13_bf = matmul_residual_pair(hid, w2_bf, xp, xs, 0.5, 1024,
                                             sides=[(ff2_w1, 0), (ff2_w3, 0)])
        else:
            x, w13_bf = ffn(x, norm_ff1, w13_bf, ff1_w2, l, [(ff2_w1, l), (ff2_w3, l)])

        h = rmsnorm(x, norm_mix[l], BF16)
        cols = matmul(h, _layout_w_in(w_in[l]), F32, 544, 1024)

        wuk_t = jnp.transpose(w_uk[l], (1, 2, 0)).astype(BF16)
        wuv_t = jnp.transpose(w_uv[l], (1, 0, 2)).astype(BF16)
        q_lat, q_rope, lat, krope, klat_bf, krope_bf = mla_prep(
            cols, q_norm[l], kv_norm[l], _layout_w_qb(w_qb[l]), wuk_t, cos128, sin128)
        ya = mla_attention_prompt(q_lat, q_rope, klat_bf, krope_bf, wuv_t,
                                  mixer_rows(MLA_HEADS * V_HEAD), batch, seq)
        ya = mla_attention_sample(q_lat, q_rope, cache_latent, cache_k_rope, l, klat_bf, krope_bf,
                                  wuv_t, ya, dbatch, blk_s)

        band_k = cols[:, COL_BK * SLOT:(COL_BK + 1) * SLOT]
        band_v = cols[:, COL_BV * SLOT:(COL_BV + 1) * SLOT]

        def prompt_rows(t):
            t = t[:mp].reshape(batch, seq, band_w).astype(BF16)
            return jnp.pad(t, ((0, 0), (BAND_ROWS, 0), (0, 0)))

        def sample_rows(t, cache):
            return jnp.concatenate([cache.reshape(dbatch, BAND_ROWS, band_w).astype(BF16),
                                    t[mp:].reshape(dbatch, dseq, band_w).astype(BF16)], axis=1)

        yb = band_attention(cols, prompt_rows(band_k), prompt_rows(band_v), rel_bias[l],
                            mixer_rows(band_w), batch, nchunks, 0, BAND_ROWS, math.gcd(nchunks, 4))
        yb = band_attention(cols, sample_rows(band_k, cache_band_k[l]),
                            sample_rows(band_v, cache_band_v[l]), rel_bias[l], yb, dbatch, 1, blk_s, 0, 1)

        conv_init = jnp.concatenate(
            [jnp.zeros((batch, 8, C_CONV), F32),
             jnp.pad(state_conv[l], ((0, 0), (8 - (CONV_W - 1), 0), (0, 0)))], axis=0)
        aneg = _pad_cols(-jnp.exp(a_log[l].astype(F32)).reshape(1, -1), LANE)
        dtb = _pad_cols(dt_bias[l].astype(F32).reshape(1, -1), LANE)
        pre = gdn_pre(cols, conv_init, conv_w[l], aneg, dtb, nchunks, mp // CHUNK)
        yc, delta_p = gdn_recurrence(pre, cols, zero_state, gdn_norm[l], mixer_rows(GDN_V),
                                     batch, nchunks, 0)
        yc, delta_s = gdn_recurrence(pre, cols, state_delta[l], gdn_norm[l], yc, dbatch, 1, blk_s)

        x = matmul_mix(ya, yb, yc, w_out, l, x, 544, 512)
        x, w13_bf = ffn(x, norm_ff2, w13_bf, ff2_w2, l,
                        [(ff1_w1, l + 1), (ff1_w3, l + 1)] if l + 1 < depth else [])

        def conv_tail(first_row, rows_per_stream, streams):
            ends = [first_row + (s + 1) * rows_per_stream for s in range(streams)]
            return jnp.stack([cols[e - (CONV_W - 1):e, :C_CONV] for e in ends])

        keep = min(BAND_ROWS, seq)
        layer_out = (
            lat[:mp].reshape(batch, seq, KV_LORA),
            krope[:mp].reshape(batch, seq, QK_ROPE),
            band_k[:mp].reshape(batch, seq, BAND_HEADS, BAND_DIM)[:, seq - keep:],
            band_v[:mp].reshape(batch, seq, BAND_HEADS, BAND_DIM)[:, seq - keep:],
            conv_tail(0, seq, batch),
            delta_p,
            lat[mp:].reshape(dbatch, dseq, KV_LORA),
            krope[mp:].reshape(dbatch, dseq, QK_ROPE),
            band_k[mp:].reshape(dbatch, dseq, BAND_HEADS, BAND_DIM),
            band_v[mp:].reshape(dbatch, dseq, BAND_HEADS, BAND_DIM),
            conv_tail(mp, dseq, dbatch),
            delta_s,
        )
        for acc, t in zip(outs, layer_out):
            acc.append(t)

    y_prompt = rmsnorm(x, final_norm, F32, rows=mp).reshape(batch, seq, d)
    y_sample = rmsnorm(x, final_norm, F32, rows=ms, row_start=mp).reshape(dbatch, dseq, d)
    return (y_prompt, y_sample) + tuple(jnp.stack(t) for t in outs)
```

```python
import functools
import math

import jax
import jax.numpy as jnp
from jax import lax
from jax.experimental import pallas as pl
from jax.experimental.pallas import tpu as pltpu

F32 = jnp.float32
BF16 = jnp.bfloat16
HIGHEST = lax.Precision.HIGHEST

CHUNK = 64
CHUNK_SHIFT = 6
EPS = 1e-6
NEG = -1e30
Q_LORA = 1024
KV_LORA = 512
QK_NOPE = 128
QK_ROPE = 64
V_HEAD = 128
MLA_HEADS = 16
ROPE_THETA = 10000.0
MLA_SCALE = (QK_NOPE + QK_ROPE) ** -0.5
MLA_HEAD_GROUPS = 8
BAND_HEADS = 8
BAND_DIM = 128
BAND_PREV = 8
BAND_ROWS = BAND_PREV * CHUNK
BAND_KEYS = BAND_ROWS + CHUNK
REL_CLIP = 128
BAND_SCALE = BAND_DIM ** -0.5
GDN_HEADS = 8
GDN_GROUP = 8
GDN_DK = 128
GDN_DV = 128
CONV_W = 4
GDN_QK = GDN_HEADS * GDN_DK
GDN_V = GDN_HEADS * GDN_DV
C_CONV = 2 * GDN_QK + GDN_V
LANE = 128
SLOT = 1024
COL_U, COL_QA, COL_BQ, COL_BK, COL_BV, COL_Z, COL_TAIL = 0, 3, 4, 5, 6, 7, 8
IN_COLS_PAD = 9 * SLOT
TAIL_KR, TAIL_KSW, TAIL_A, TAIL_B = 512, 640, 768, 896

VMEM_LIMIT_BYTES = 58 * 1024 * 1024


def _params(*sem):
    return pltpu.CompilerParams(dimension_semantics=sem, vmem_limit_bytes=VMEM_LIMIT_BYTES)


def _tile(n, target, mult=16):
    best = None
    for t in range(mult, min(n, target) + 1, mult):
        if n % t == 0:
            best = t
    assert best is not None, (n, target)
    return best


def _dot(a, b, precision=None):
    return jnp.dot(a, b, preferred_element_type=F32, precision=precision)


def _dot_nt(a, b, precision=None):
    return lax.dot_general(a, b, (((1,), (1,)), ((), ())), preferred_element_type=F32,
                           precision=precision)


def _dot_tn(a, b, precision=None):
    return lax.dot_general(a, b, (((0,), (0,)), ((), ())), preferred_element_type=F32,
                           precision=precision)


def _split_bf16(x):
    hi = x.astype(BF16)
    return hi, (x - hi.astype(F32)).astype(BF16)


def _dot_3pass(a, b, expand=lambda t: t):
    a_hi, a_lo = _split_bf16(a)
    b_hi, b_lo = _split_bf16(b)
    m = a.shape[0]
    top = _dot(jnp.concatenate([a_hi, a_lo], axis=0), expand(b_hi))
    return top[:m] + top[m:] + _dot(a_hi, expand(b_lo))


def _sigmoid(x):
    return 1.0 / (1.0 + jnp.exp(-x))


def _rms(x, g):
    return x * lax.rsqrt(jnp.mean(x * x, axis=-1, keepdims=True) + EPS) * g


def _rmsnorm_kernel(x_ref, g_ref, o_ref):
    o_ref[...] = _rms(x_ref[...], g_ref[...]).astype(o_ref.dtype)


def rmsnorm(x, g, out_dtype, rows=None, row_start=0):
    m, d = x.shape
    rows = m if rows is None else rows
    tm = _tile(math.gcd(rows, row_start) if row_start else rows, 256, 8)
    off = row_start // tm
    return pl.pallas_call(
        _rmsnorm_kernel,
        grid=(rows // tm,),
        in_specs=[pl.BlockSpec((tm, d), lambda i: (i + off, 0)),
                  pl.BlockSpec((1, d), lambda i: (0, 0))],
        out_specs=pl.BlockSpec((tm, d), lambda i: (i, 0)),
        out_shape=jax.ShapeDtypeStruct((rows, d), out_dtype),
        compiler_params=_params("parallel"),
        name="rmsnorm",
    )(x, g.reshape(1, d))


def _pair_specs(first, second, tm, width, row_axis, col_fn=None):
    n_first = first.shape[0] // tm
    col = (lambda *g: 0) if col_fn is None else col_fn
    return (n_first,
            pl.BlockSpec((tm, width), lambda *g: (jnp.minimum(g[row_axis], n_first - 1), col(*g))),
            pl.BlockSpec((tm, width), lambda *g: (jnp.maximum(g[row_axis] - n_first, 0), col(*g))))


def _rmsnorm_pair_kernel(xp_ref, xs_ref, g_ref, o_ref, *, n_first):
    @pl.when(pl.program_id(0) < n_first)
    def _():
        o_ref[...] = _rms(xp_ref[...], g_ref[...]).astype(o_ref.dtype)

    @pl.when(pl.program_id(0) >= n_first)
    def _():
        o_ref[...] = _rms(xs_ref[...], g_ref[...]).astype(o_ref.dtype)


def rmsnorm_pair(xp, xs, g, out_dtype):
    d = xp.shape[1]
    tm = _tile(math.gcd(xp.shape[0], xs.shape[0]), 256, 8)
    n_first, spec_p, spec_s = _pair_specs(xp, xs, tm, d, 0)
    m = xp.shape[0] + xs.shape[0]
    return pl.pallas_call(
        functools.partial(_rmsnorm_pair_kernel, n_first=n_first),
        grid=(m // tm,),
        in_specs=[spec_p, spec_s, pl.BlockSpec((1, d), lambda i: (0, 0))],
        out_specs=pl.BlockSpec((tm, d), lambda i: (i, 0)),
        out_shape=jax.ShapeDtypeStruct((m, d), out_dtype),
        compiler_params=_params("parallel"),
        name="rmsnorm_pair",
    )(xp, xs, g.reshape(1, d))


def _cast_weight(w_ref, wb_ref):
    @pl.when(pl.program_id(1) == 0)
    def _():
        wb_ref[...] = w_ref[...].astype(BF16)


def _tiled_kernel(*refs, n_w, n_extra, n_side, is_f32, epilogue):
    a_ref, refs = refs[0], refs[1:]
    w_refs, refs = refs[:n_w], refs[n_w:]
    extra, refs = refs[:n_extra], refs[n_extra:]
    side_in, refs = refs[:n_side], refs[n_side:]
    o_ref, refs = refs[0], refs[1:]
    side_out, scratch = refs[:n_side], list(refs[n_side:])
    for s_in, s_out in zip(side_in, side_out):
        s_out[...] = s_in[...].astype(BF16)
    ws = []
    for w_ref, f32 in zip(w_refs, is_f32):
        if f32:
            wb_ref = scratch.pop(0)
            _cast_weight(w_ref, wb_ref)
            ws.append(wb_ref)
        else:
            ws.append(w_ref)
    o_ref[...] = epilogue(a_ref[...], ws, extra).astype(o_ref.dtype)


def _swiglu_epilogue(a, ws, extra):
    h1 = _dot(a, ws[0][...])
    h3 = _dot(a, ws[1][...])
    return h1 * _sigmoid(h1) * h3


def _residual_epilogue(a, ws, extra, *, scale):
    return extra[0][...] + scale * _dot(a, ws[0][...])


def _residual_pair_epilogue(a, ws, extra, *, scale, n_first):
    res = jnp.where(pl.program_id(1) < n_first, extra[0][...], extra[1][...])
    return res + scale * _dot(a, ws[0][...])


def _tiled_matmul(a, weights, extra, extra_specs, sides, epilogue, out_dtype, tm, tn, name):
    m, k = a.shape
    is_f32 = tuple(isinstance(w, tuple) for w in weights)
    n = weights[0][0].shape[2] if is_f32[0] else weights[0].shape[1]
    nj, ni = n // tn, m // tm
    w_args, w_specs = [], []
    for w, f32 in zip(weights, is_f32):
        if f32:
            w_args.append(w[0])
            w_specs.append(_layer_weight_spec(w[1], k, tn))
        else:
            w_args.append(w)
            w_specs.append(pl.BlockSpec((k, tn), lambda j, i: (0, j)))
    side_args, side_in_specs, side_out_specs, side_shapes = [], [], [], []
    for w, layer in sides:
        slabs = max(t for t in range(1, nj * ni + 1)
                    if w.shape[1] % t == 0 and (w.shape[1] // t) % 16 == 0)
        rows = w.shape[1] // slabs
        side_args.append(w)
        side_in_specs.append(pl.BlockSpec(
            (None, rows, w.shape[2]), functools.partial(_side_in_map, layer=layer, ni=ni, slabs=slabs)))
        side_out_specs.append(pl.BlockSpec(
            (rows, w.shape[2]), functools.partial(_side_out_map, ni=ni, slabs=slabs)))
        side_shapes.append(jax.ShapeDtypeStruct(w.shape[1:], BF16))
    outs = pl.pallas_call(
        functools.partial(_tiled_kernel, n_w=len(weights), n_extra=len(extra), n_side=len(sides),
                          is_f32=is_f32, epilogue=epilogue),
        grid=(nj, ni),
        in_specs=[pl.BlockSpec((tm, k), lambda j, i: (i, 0))] + w_specs + list(extra_specs) + side_in_specs,
        out_specs=[pl.BlockSpec((tm, tn), lambda j, i: (i, j))] + side_out_specs,
        out_shape=[jax.ShapeDtypeStruct((m, n), out_dtype)] + side_shapes,
        scratch_shapes=[pltpu.VMEM((k, tn), BF16) for f32 in is_f32 if f32],
        compiler_params=_params("arbitrary", "arbitrary"),
        name=name,
    )(a, *w_args, *extra, *side_args)
    return outs[0], list(outs[1:])


def _side_in_map(j, i, *, layer, ni, slabs):
    return (layer, jnp.minimum(j * ni + i, slabs - 1), 0)


def _side_out_map(j, i, *, ni, slabs):
    return (jnp.minimum(j * ni + i, slabs - 1), 0)


def _mm_mix_kernel(ya_ref, yb_ref, yc_ref, w_ref, r_ref, o_ref):
    ka, kb = ya_ref.shape[1], yb_ref.shape[1]
    o_ref[...] = (r_ref[...] + _dot(ya_ref[...], w_ref[0:ka, :])
                  + _dot(yb_ref[...], w_ref[ka:ka + kb, :]) + _dot(yc_ref[...], w_ref[ka + kb:, :]))


def _plain_epilogue(a, ws, extra):
    return _dot(a, ws[0][...])


def matmul(a, w, out_dtype, tm_target, tn_target, sides=()):
    tm, tn = _tile(a.shape[0], tm_target), _tile(w.shape[1], tn_target, LANE)
    return _tiled_matmul(a, [w], [], [], sides, _plain_epilogue, out_dtype, tm, tn, "matmul")


def _layer_weight_spec(layer, k, tn):
    return pl.BlockSpec((None, k, tn), lambda j, i: (layer, 0, j))


def matmul_residual(a, w, res, scale, tm_target, tn_target, sides=()):
    n = res.shape[1]
    tm, tn = _tile(a.shape[0], tm_target), _tile(n, tn_target, LANE)
    return _tiled_matmul(a, [w], [res], [pl.BlockSpec((tm, tn), lambda j, i: (i, j))], sides,
                         functools.partial(_residual_epilogue, scale=scale), F32, tm, tn,
                         "matmul_residual")


def matmul_residual_pair(a, w, res_p, res_s, scale, tn_target, sides=()):
    tm = _tile(math.gcd(res_p.shape[0], res_s.shape[0]), 256)
    tn = _tile(res_p.shape[1], tn_target, LANE)
    n_first, spec_p, spec_s = _pair_specs(res_p, res_s, tm, tn, 1, lambda j, i: j)
    return _tiled_matmul(a, [w], [res_p, res_s], [spec_p, spec_s], sides,
                         functools.partial(_residual_pair_epilogue, scale=scale, n_first=n_first),
                         F32, tm, tn, "matmul_residual_pair")


def matmul_mix(ya, yb, yc, w, res, tm_target, tn_target):
    m = ya.shape[0]
    k, n = w.shape
    tm, tn = _tile(m, tm_target), _tile(n, tn_target, LANE)
    row = lambda y: pl.BlockSpec((tm, y.shape[1]), lambda j, i: (i, 0))
    return pl.pallas_call(
        _mm_mix_kernel,
        grid=(n // tn, m // tm),
        in_specs=[row(ya), row(yb), row(yc), pl.BlockSpec((k, tn), lambda j, i: (0, j)),
                  pl.BlockSpec((tm, tn), lambda j, i: (i, j))],
        out_specs=pl.BlockSpec((tm, tn), lambda j, i: (i, j)),
        out_shape=jax.ShapeDtypeStruct((m, n), F32),
        compiler_params=_params("parallel", "parallel"),
        name="matmul_mix",
    )(ya, yb, yc, w, res)


def matmul_swiglu(a, w1, w3, tm_target, tn_target, sides=()):
    n = w1[0].shape[2] if isinstance(w1, tuple) else w1.shape[1]
    tm, tn = _tile(a.shape[0], tm_target), _tile(n, tn_target, LANE)
    return _tiled_matmul(a, [w1, w3], [], [], sides, _swiglu_epilogue, BF16, tm, tn, "matmul_swiglu")


def _mla_prep_kernel(qa_ref, tail_ref, qn_ref, kvn_ref, wq_ref, wuk_ref, cos_ref, sin_ref,
                     qlat_ref, qrope_ref, lat_ref, krope_ref, klat_bf_ref, krope_bf_ref):
    hq = _rms(qa_ref[...], qn_ref[...]).astype(BF16)
    q_all = _dot(hq, wq_ref[...])
    cos = cos_ref[...]
    sin = sin_ref[...]
    hw = MLA_HEADS * LANE
    for h in range(MLA_HEADS):
        qn = q_all[:, h * LANE:(h + 1) * LANE].astype(BF16)
        qlat_ref[h] = (_dot(qn, wuk_ref[h]) * MLA_SCALE).astype(BF16)
        qr = (q_all[:, hw + h * LANE:hw + (h + 1) * LANE] * cos
              + q_all[:, 2 * hw + h * LANE:2 * hw + (h + 1) * LANE] * sin)
        qrope_ref[h] = (qr[:, :QK_ROPE] * MLA_SCALE).astype(BF16)
    tail = tail_ref[...]
    lat = _rms(tail[:, :KV_LORA], kvn_ref[...])
    lat_ref[...] = lat
    klat_bf_ref[...] = lat.astype(BF16)
    kr = tail[:, TAIL_KR:TAIL_KR + LANE] * cos + tail[:, TAIL_KSW:TAIL_KSW + LANE] * sin
    krope_ref[...] = kr[:, :QK_ROPE]
    krope_bf_ref[...] = kr[:, :QK_ROPE].astype(BF16)


def mla_prep(cols, q_norm, kv_norm, wq_all, wuk_t, cos128, sin128):
    m = cols.shape[0]
    tm = _tile(m, 256)
    h = MLA_HEADS
    const2 = lambda i: (0, 0)
    return pl.pallas_call(
        _mla_prep_kernel,
        grid=(m // tm,),
        in_specs=[pl.BlockSpec((tm, SLOT), lambda i: (i, COL_QA)),
                  pl.BlockSpec((tm, SLOT), lambda i: (i, COL_TAIL)),
                  pl.BlockSpec((1, Q_LORA), const2),
                  pl.BlockSpec((1, KV_LORA), const2),
                  pl.BlockSpec(wq_all.shape, const2, pipeline_mode=pl.Buffered(1)),
                  pl.BlockSpec(wuk_t.shape, lambda i: (0, 0, 0), pipeline_mode=pl.Buffered(1)),
                  pl.BlockSpec((tm, LANE), lambda i: (i, 0)),
                  pl.BlockSpec((tm, LANE), lambda i: (i, 0))],
        out_specs=[pl.BlockSpec((h, tm, KV_LORA), lambda i: (0, i, 0)),
                   pl.BlockSpec((h, tm, QK_ROPE), lambda i: (0, i, 0)),
                   pl.BlockSpec((tm, KV_LORA), lambda i: (i, 0)),
                   pl.BlockSpec((tm, QK_ROPE), lambda i: (i, 0)),
                   pl.BlockSpec((tm, KV_LORA), lambda i: (i, 0)),
                   pl.BlockSpec((tm, QK_ROPE), lambda i: (i, 0))],
        out_shape=[jax.ShapeDtypeStruct((h, m, KV_LORA), BF16),
                   jax.ShapeDtypeStruct((h, m, QK_ROPE), BF16),
                   jax.ShapeDtypeStruct((m, KV_LORA), F32),
                   jax.ShapeDtypeStruct((m, QK_ROPE), F32),
                   jax.ShapeDtypeStruct((m, KV_LORA), BF16),
                   jax.ShapeDtypeStruct((m, QK_ROPE), BF16)],
        compiler_params=_params("parallel"),
        name="mla_prep",
    )(cols, cols, q_norm.reshape(1, -1), kv_norm.reshape(1, -1), wq_all, wuk_t, cos128, sin128)


def _softmax_init(m_ref, l_ref, acc_ref):
    m_ref[...] = jnp.full(m_ref.shape, NEG, F32)
    l_ref[...] = jnp.zeros(l_ref.shape, F32)
    acc_ref[...] = jnp.zeros(acc_ref.shape, F32)


def _mla_step(ql_ref, qr_ref, k_l, k_r, m_ref, l_ref, acc_ref, tq, visible=None):
    hg = MLA_HEADS // MLA_HEAD_GROUPS
    rows = hg * tq

    def scores(g):
        q_l = ql_ref[g * hg:(g + 1) * hg].reshape(rows, KV_LORA)
        q_r = qr_ref[g * hg:(g + 1) * hg].reshape(rows, QK_ROPE)
        s = _dot_nt(q_l, k_l) + _dot_nt(q_r, k_r)
        return s if visible is None else jnp.where(visible, s, NEG)

    def update(g, s):
        rs = slice(g * rows, (g + 1) * rows)
        m_old = m_ref[rs]
        m_new = jnp.maximum(m_old, jnp.max(s, axis=-1, keepdims=True))
        alpha = jnp.exp(m_old - m_new)
        p = jnp.exp(s - m_new)
        l_ref[rs] = alpha * l_ref[rs] + jnp.sum(p, axis=-1, keepdims=True)
        acc_ref[rs] = alpha * acc_ref[rs] + _dot(p.astype(BF16), k_l)
        m_ref[rs] = m_new

    s = scores(0)
    for g in range(1, MLA_HEAD_GROUPS):
        s_next = scores(g)
        update(g - 1, s)
        s = s_next
    update(MLA_HEAD_GROUPS - 1, s)


def _mla_finish(wuv_ref, o_ref, l_ref, acc_ref, tq):
    ctx = acc_ref[...] / l_ref[...]
    for h in range(MLA_HEADS):
        ch = ctx[h * tq:(h + 1) * tq].astype(BF16)
        o_ref[:, h * V_HEAD:(h + 1) * V_HEAD] = _dot(ch, wuv_ref[h]).astype(o_ref.dtype)


def _mla_prompt_kernel(ql_ref, qr_ref, kl_ref, kr_ref, wuv_ref, y_prev_ref, o_ref,
                       m_ref, l_ref, acc_ref, *, tq, tk):
    del y_prev_ref
    qi = pl.program_id(1)
    kj = pl.program_id(2)
    last = (qi * tq + tq - 1) // tk

    @pl.when(kj == 0)
    def _():
        _softmax_init(m_ref, l_ref, acc_ref)

    @pl.when(kj < last)
    def _():
        _mla_step(ql_ref, qr_ref, kl_ref[...], kr_ref[...], m_ref, l_ref, acc_ref, tq)

    @pl.when(kj == last)
    def _():
        shape = (MLA_HEADS // MLA_HEAD_GROUPS * tq, tk)
        row = lax.broadcasted_iota(jnp.int32, shape, 0)
        col = lax.broadcasted_iota(jnp.int32, shape, 1)
        q_chunk = (qi * tq + row % tq) // CHUNK
        k_chunk = (kj * tk + col) // CHUNK
        _mla_step(ql_ref, qr_ref, kl_ref[...], kr_ref[...], m_ref, l_ref, acc_ref, tq,
                  visible=k_chunk <= q_chunk)

    @pl.when(kj == pl.num_programs(2) - 1)
    def _():
        _mla_finish(wuv_ref, o_ref, l_ref, acc_ref, tq)


def mla_attention_prompt(q_lat, q_rope, k_lat, k_rope, wuv, y_prev, batch, seq):
    tq = 128
    tk = _tile(seq, 512, CHUNK)
    nq, nk = seq // tq, seq // tk
    h = MLA_HEADS
    rows = h * tq

    def k_map(b, qi, kj):
        return (b * nk + jnp.minimum(kj, (qi * tq + tq - 1) // tk), 0)

    return pl.pallas_call(
        functools.partial(_mla_prompt_kernel, tq=tq, tk=tk),
        grid=(batch, nq, nk),
        in_specs=[pl.BlockSpec((h, tq, KV_LORA), lambda b, qi, kj: (0, b * nq + qi, 0)),
                  pl.BlockSpec((h, tq, QK_ROPE), lambda b, qi, kj: (0, b * nq + qi, 0)),
                  pl.BlockSpec((tk, KV_LORA), k_map),
                  pl.BlockSpec((tk, QK_ROPE), k_map),
                  pl.BlockSpec(wuv.shape, lambda b, qi, kj: (0, 0, 0)),
                  pl.BlockSpec(memory_space=pl.ANY)],
        out_specs=pl.BlockSpec((tq, h * V_HEAD), lambda b, qi, kj: (b * nq + qi, 0)),
        out_shape=jax.ShapeDtypeStruct(y_prev.shape, y_prev.dtype),
        input_output_aliases={5: 0},
        scratch_shapes=[pltpu.VMEM((rows, 1), F32), pltpu.VMEM((rows, 1), F32),
                        pltpu.VMEM((rows, KV_LORA), F32)],
        compiler_params=_params("parallel", "parallel", "arbitrary"),
        name="mla_attention_prompt",
    )(q_lat, q_rope, k_lat, k_rope, wuv, y_prev)


def _mla_sample_kernel(ql_ref, qr_ref, pl_ref, pr_ref, kl_ref, kr_ref, wuv_ref, y_prev_ref, o_ref,
                       m_ref, l_ref, acc_ref):
    del y_prev_ref
    kj = pl.program_id(1)
    n_past = pl.num_programs(1) - 1

    @pl.when(kj == 0)
    def _():
        _softmax_init(m_ref, l_ref, acc_ref)

    @pl.when(kj < n_past)
    def _():
        _mla_step(ql_ref, qr_ref, pl_ref[0].astype(BF16), pr_ref[0].astype(BF16),
                  m_ref, l_ref, acc_ref, CHUNK)

    @pl.when(kj == n_past)
    def _():
        _mla_step(ql_ref, qr_ref, kl_ref[...], kr_ref[...], m_ref, l_ref, acc_ref, CHUNK)
        _mla_finish(wuv_ref, o_ref, l_ref, acc_ref, CHUNK)


def mla_attention_sample(q_lat, q_rope, past_lat, past_rope, layer, k_lat, k_rope, wuv, y_prev,
                         batch, tok_block0):
    past = past_lat.shape[2]
    tk = _tile(past, 512, CHUNK)
    n_past = past // tk
    h = MLA_HEADS
    rows = h * CHUNK
    past_map = lambda b, kj: (layer, b, jnp.minimum(kj, n_past - 1), 0)
    return pl.pallas_call(
        _mla_sample_kernel,
        grid=(batch, n_past + 1),
        in_specs=[pl.BlockSpec((h, CHUNK, KV_LORA), lambda b, kj: (0, tok_block0 + b, 0)),
                  pl.BlockSpec((h, CHUNK, QK_ROPE), lambda b, kj: (0, tok_block0 + b, 0)),
                  pl.BlockSpec((None, 1, tk, KV_LORA), past_map),
                  pl.BlockSpec((None, 1, tk, QK_ROPE), past_map),
                  pl.BlockSpec((CHUNK, KV_LORA), lambda b, kj: (tok_block0 + b, 0)),
                  pl.BlockSpec((CHUNK, QK_ROPE), lambda b, kj: (tok_block0 + b, 0)),
                  pl.BlockSpec(wuv.shape, lambda b, kj: (0, 0, 0)),
                  pl.BlockSpec(memory_space=pl.ANY)],
        out_specs=pl.BlockSpec((CHUNK, h * V_HEAD), lambda b, kj: (tok_block0 + b, 0)),
        out_shape=jax.ShapeDtypeStruct(y_prev.shape, y_prev.dtype),
        input_output_aliases={7: 0},
        scratch_shapes=[pltpu.VMEM((rows, 1), F32), pltpu.VMEM((rows, 1), F32),
                        pltpu.VMEM((rows, KV_LORA), F32)],
        compiler_params=_params("parallel", "arbitrary"),
        name="mla_attention_sample",
    )(q_lat, q_rope, past_lat, past_rope, k_lat, k_rope, wuv, y_prev)


def _band_span(qchunks):
    tq, nk = qchunks * CHUNK, (qchunks + BAND_PREV) * CHUNK
    return -(-(tq + nk - 1) // LANE) * LANE


def _band_kernel(q_ref, k_ref, v_ref, g_ref, y_prev_ref, o_ref, bias_ref, *, pad_rows, qchunks):
    del y_prev_ref
    n = pl.program_id(1)
    tq, nk = qchunks * CHUNK, (qchunks + BAND_PREV) * CHUNK
    span = _band_span(qchunks)

    @pl.when(jnp.logical_and(pl.program_id(0) == 0, n == 0))
    def _():
        for h in range(BAND_HEADS):
            rows = jnp.broadcast_to(g_ref[h:h + 1, :], (tq, span))
            rolled = pltpu.roll(rows, span - (tq - 1), 1, stride=1, stride_axis=0)
            bias_ref[h] = rolled[:, :nk]

    start = pl.multiple_of(n * tq, CHUNK)
    row = lax.broadcasted_iota(jnp.int32, (tq, nk), 0)
    col = lax.broadcasted_iota(jnp.int32, (tq, nk), 1)
    visible = start + col >= pad_rows
    if qchunks > 1:
        ahead = (col >> CHUNK_SHIFT) - (row >> CHUNK_SHIFT)
        visible = visible & (ahead >= 0) & (ahead <= BAND_PREV)
    q = q_ref[...]
    for h in range(BAND_HEADS):
        lanes = slice(h * BAND_DIM, (h + 1) * BAND_DIM)
        qh = q[:, lanes].astype(BF16)
        kh = k_ref[0, pl.ds(start, nk), lanes]
        vh = v_ref[0, pl.ds(start, nk), lanes]
        s = _dot_nt(qh, kh) * BAND_SCALE + bias_ref[h]
        if pad_rows or qchunks > 1:
            s = jnp.where(visible, s, NEG)
        p = jnp.exp(s - jnp.max(s, axis=-1, keepdims=True))
        o = _dot(p.astype(BF16), vh) / jnp.sum(p, axis=-1, keepdims=True)
        o_ref[:, lanes] = o.astype(o_ref.dtype)


def band_attention(cols, k_rows, v_rows, rel_bias, y_prev, batch, nchunks, tok_block0, pad_rows, qchunks):
    assert nchunks % qchunks == 0 and tok_block0 % qchunks == 0
    width = BAND_HEADS * BAND_DIM
    rows = k_rows.shape[1]
    tq, nk = qchunks * CHUNK, (qchunks + BAND_PREV) * CHUNK
    nblk = nchunks // qchunks
    bias_row = _band_bias_row(rel_bias, qchunks)
    tok = lambda b, n: tok_block0 // qchunks + b * nblk + n
    return pl.pallas_call(
        functools.partial(_band_kernel, pad_rows=pad_rows, qchunks=qchunks),
        grid=(batch, nblk),
        in_specs=[pl.BlockSpec((tq, width), lambda b, n: (tok(b, n), COL_BQ)),
                  pl.BlockSpec((1, rows, width), lambda b, n: (b, 0, 0)),
                  pl.BlockSpec((1, rows, width), lambda b, n: (b, 0, 0)),
                  pl.BlockSpec(bias_row.shape, lambda b, n: (0, 0)),
                  pl.BlockSpec(memory_space=pl.ANY)],
        out_specs=pl.BlockSpec((tq, width), lambda b, n: (tok(b, n), 0)),
        out_shape=jax.ShapeDtypeStruct(y_prev.shape, y_prev.dtype),
        input_output_aliases={4: 0},
        scratch_shapes=[pltpu.VMEM((BAND_HEADS, tq, nk), F32)],
        compiler_params=_params("arbitrary", "arbitrary"),
        name="band_attention",
    )(cols, k_rows, v_rows, bias_row, y_prev)


def _expand_matrix(rows, first, width, per_shift):
    r = lax.broadcasted_iota(jnp.int32, (rows, width), 0)
    c = lax.broadcasted_iota(jnp.int32, (rows, width), 1)
    return (r == first + (c >> per_shift)).astype(F32)


def _unit_lower_inverse(a, row, col, group):
    gl = a.shape[1]
    r = lax.broadcasted_iota(jnp.int32, (gl, gl), 0)
    c = lax.broadcasted_iota(jnp.int32, (gl, gl), 1)
    diag_blocks = ((r >> CHUNK_SHIFT) == (c >> CHUNK_SHIFT)).astype(BF16)

    def block_diag(w):
        return jnp.concatenate([w] * group, axis=0) * diag_blocks

    t = (row == col).astype(F32) - jnp.where((row >> 1) == (col >> 1), a, 0.0)
    for shift in range(1, CHUNK_SHIFT):
        lower_left = ((row >> (shift + 1)) == (col >> (shift + 1))) & ((row >> shift) != (col >> shift))
        x = _dot_3pass(t, jnp.where(lower_left, a, 0.0), block_diag)
        t = t - _dot_3pass(x, t, block_diag)
    return t


def _l2norm_heads(x, heads, width):
    parts = []
    for h in range(heads):
        xh = x[:, h * width:(h + 1) * width]
        parts.append(xh * lax.rsqrt(jnp.sum(xh * xh, axis=-1, keepdims=True) + EPS))
    return jnp.concatenate(parts, axis=1)


def _gdn_pre_kernel(u_ref, uprev_ref, init_ref, tail_ref, cw_ref, aneg_ref, dtb_ref,
                    uv_ref, wk_ref, attn_ref, qg_ref, kt_ref, gl_ref, ext_ref,
                    *, chunks_per_seq, n_prompt_chunks):
    group = GDN_GROUP
    gw = group * CHUNK
    c = pl.program_id(0)
    is_start = jnp.logical_or(c >= n_prompt_chunks, c % chunks_per_seq == 0)
    ext_ref[0:8, :] = jnp.where(is_start, init_ref[0], uprev_ref[...])
    ext_ref[8:8 + CHUNK, :] = u_ref[...]
    cw = cw_ref[...]
    uc = (cw[3:4] * ext_ref[8:8 + CHUNK, :] + cw[2:3] * ext_ref[7:7 + CHUNK, :]
          + cw[1:2] * ext_ref[6:6 + CHUNK, :] + cw[0:1] * ext_ref[5:5 + CHUNK, :])
    uc = uc * _sigmoid(uc)

    tail = tail_ref[...]
    xa = tail[:, TAIL_A:TAIL_A + LANE] + dtb_ref[...]
    softplus = jnp.maximum(xa, 0.0) + jnp.log(1.0 + jnp.exp(-jnp.abs(xa)))
    g = aneg_ref[...] * softplus
    beta = _sigmoid(tail[:, TAIL_B:TAIL_B + LANE])

    r64 = lax.broadcasted_iota(jnp.int32, (CHUNK, CHUNK), 0)
    c64 = lax.broadcasted_iota(jnp.int32, (CHUNK, CHUNK), 1)
    gc = _dot((r64 >= c64).astype(F32), g, HIGHEST)
    egc = jnp.exp(gc)
    ekt = jnp.exp(gc[CHUNK - 1:CHUNK, :] - gc)
    e128 = _expand_matrix(LANE, 0, GDN_HEADS * GDN_DK, 7)
    beta_x = _dot(beta, e128, HIGHEST)
    egc_x = _dot(egc, e128, HIGHEST)
    ekt_x = _dot(ekt, e128, HIGHEST)
    gl_ref[0] = egc_x[CHUNK - 1:CHUNK, :]

    qn = _l2norm_heads(uc[:, :GDN_QK], GDN_HEADS, GDN_DK) * (GDN_DK ** -0.5)
    kn = _l2norm_heads(uc[:, GDN_QK:2 * GDN_QK], GDN_HEADS, GDN_DK)
    v_all = uc[:, 2 * GDN_QK:]
    qg_ref[...] = (qn * egc_x).astype(BF16)
    kt_ref[...] = (kn * ekt_x).astype(BF16)
    rhs_v = beta_x * v_all
    rhs_k = beta_x * egc_x * kn

    row = lax.broadcasted_iota(jnp.int32, (CHUNK, gw), 0)
    col = lax.broadcasted_iota(jnp.int32, (CHUNK, gw), 1) & (CHUNK - 1)
    incl = row >= col
    head_r = lax.broadcasted_iota(jnp.int32, (gw, group * GDN_DK), 0) >> CHUNK_SHIFT
    head_c = lax.broadcasted_iota(jnp.int32, (gw, group * GDN_DK), 1) >> 7
    for grp in range(GDN_HEADS // group):
        lanes = slice(grp * group * GDN_DK, (grp + 1) * group * GDN_DK)
        e64 = _expand_matrix(LANE, grp * group, gw, CHUNK_SHIFT)
        gcw = _dot(gc, e64, HIGHEST)
        bw = _dot(beta, e64, HIGHEST)
        gc_t = jnp.sum(jnp.where(row == col, gcw, 0.0), axis=0, keepdims=True)
        decay = jnp.where(incl, jnp.exp(jnp.where(incl, gcw - gc_t, 0.0)), 0.0)
        kg = kn[:, lanes]
        k_rows = jnp.where(head_r == head_c, jnp.concatenate([kg] * group, axis=0), 0.0).astype(BF16)
        kk = _dot_nt(kg.astype(BF16), k_rows)
        qk = _dot_nt(qn[:, lanes].astype(BF16), k_rows)
        a_mat = jnp.where(row > col, bw * kk * decay, 0.0)
        t_inv = _unit_lower_inverse(a_mat, row, col, group)
        attn_ref[:, grp * gw:(grp + 1) * gw] = (qk * decay).astype(BF16)
        for hh in range(group):
            h = grp * group + hh
            hl = slice(h * GDN_DK, (h + 1) * GDN_DK)
            rhs = jnp.concatenate([rhs_v[:, hl], rhs_k[:, hl]], axis=1)
            sol = _dot_3pass(t_inv[:, hh * CHUNK:(hh + 1) * CHUNK], rhs)
            uv_ref[:, hl] = sol[:, :GDN_DV]
            wk_ref[:, hl] = sol[:, GDN_DV:].astype(BF16)


def gdn_pre(cols, conv_init, conv_w, aneg, dtb, chunks_per_seq, n_prompt_chunks):
    m = cols.shape[0]
    nch = m // CHUNK
    h = GDN_HEADS

    def init_map(c):
        return (jnp.where(c < n_prompt_chunks, c // chunks_per_seq,
                          n_prompt_chunks // chunks_per_seq + c - n_prompt_chunks), 0, 0)

    tok_spec = lambda d: pl.BlockSpec((CHUNK, d), lambda c: (c, 0))
    return pl.pallas_call(
        functools.partial(_gdn_pre_kernel, chunks_per_seq=chunks_per_seq,
                          n_prompt_chunks=n_prompt_chunks),
        grid=(nch,),
        in_specs=[pl.BlockSpec((CHUNK, C_CONV), lambda c: (c, COL_U)),
                  pl.BlockSpec((8, C_CONV), lambda c: (jnp.maximum(c * (CHUNK // 8) - 1, 0), COL_U)),
                  pl.BlockSpec((1, 8, C_CONV), init_map),
                  pl.BlockSpec((CHUNK, SLOT), lambda c: (c, COL_TAIL)),
                  pl.BlockSpec((CONV_W, C_CONV), lambda c: (0, 0)),
                  pl.BlockSpec((1, LANE), lambda c: (0, 0)),
                  pl.BlockSpec((1, LANE), lambda c: (0, 0))],
        out_specs=[tok_spec(GDN_V), tok_spec(GDN_QK), tok_spec(h * CHUNK), tok_spec(GDN_QK),
                   tok_spec(GDN_QK), pl.BlockSpec((1, 1, GDN_QK), lambda c: (c, 0, 0))],
        out_shape=[jax.ShapeDtypeStruct((m, GDN_V), F32),
                   jax.ShapeDtypeStruct((m, GDN_QK), BF16),
                   jax.ShapeDtypeStruct((m, h * CHUNK), BF16),
                   jax.ShapeDtypeStruct((m, GDN_QK), BF16),
                   jax.ShapeDtypeStruct((m, GDN_QK), BF16),
                   jax.ShapeDtypeStruct((nch, 1, GDN_QK), F32)],
        scratch_shapes=[pltpu.VMEM((8 + CHUNK, C_CONV), F32)],
        compiler_params=_params("parallel"),
        name="gdn_pre",
    )(cols, cols, conv_init, cols, conv_w, aneg, dtb)


def _gdn_rec_kernel(uv_ref, wk_ref, attn_ref, qg_ref, kt_ref, gl_ref, s0_ref, z_ref, gn_ref,
                    y_prev_ref, y_ref, s_ref):
    del y_prev_ref

    @pl.when(pl.program_id(1) == 0)
    def _():
        s_ref[...] = s0_ref[...]

    z = z_ref[...]
    gn = gn_ref[...]
    heads = range(GDN_HEADS)
    lanes = [slice(h * GDN_DK, (h + 1) * GDN_DK) for h in heads]
    s_old = [s_ref[0, h] for h in heads]
    s_bf = [s.astype(BF16) for s in s_old]
    ws = [_dot(wk_ref[:, lanes[h]], s_bf[h]) for h in heads]
    qs = [_dot(qg_ref[:, lanes[h]], s_bf[h]) for h in heads]
    u_bf = [(uv_ref[:, lanes[h]] - ws[h]).astype(BF16) for h in heads]
    outs = [qs[h] + _dot(attn_ref[:, h * CHUNK:(h + 1) * CHUNK], u_bf[h]) for h in heads]
    for h in heads:
        s_ref[0, h] = gl_ref[0, :, lanes[h]] * s_old[h] + _dot_tn(kt_ref[:, lanes[h]], u_bf[h])
    for h in heads:
        zh = z[:, lanes[h]]
        y_ref[:, lanes[h]] = (_rms(outs[h], gn) * (zh * _sigmoid(zh))).astype(y_ref.dtype)


def gdn_recurrence(pre, cols, s0, gdn_norm, y_prev, batch, nchunks, tok_block0):
    uv, wk, attn, qg, kt, gl = pre
    h = GDN_HEADS
    tok = lambda b, n: tok_block0 + b * nchunks + n
    tok_spec = lambda d: pl.BlockSpec((CHUNK, d), lambda b, n: (tok(b, n), 0))
    state_spec = pl.BlockSpec((1, h, GDN_DK, GDN_DV), lambda b, n: (b, 0, 0, 0))
    return pl.pallas_call(
        _gdn_rec_kernel,
        grid=(batch, nchunks),
        in_specs=[tok_spec(GDN_V), tok_spec(GDN_QK), tok_spec(h * CHUNK), tok_spec(GDN_QK),
                  tok_spec(GDN_QK),
                  pl.BlockSpec((1, 1, GDN_QK), lambda b, n: (tok(b, n), 0, 0)),
                  state_spec,
                  pl.BlockSpec((CHUNK, SLOT), lambda b, n: (tok(b, n), COL_Z)),
                  pl.BlockSpec((1, GDN_DV), lambda b, n: (0, 0)),
                  pl.BlockSpec(memory_space=pl.ANY)],
        out_specs=[tok_spec(GDN_V), state_spec],
        out_shape=[jax.ShapeDtypeStruct(y_prev.shape, y_prev.dtype),
                   jax.ShapeDtypeStruct((batch, h, GDN_DK, GDN_DV), F32)],
        input_output_aliases={9: 0},
        compiler_params=_params("parallel", "arbitrary"),
        name="gdn_recurrence",
    )(uv, wk, attn, qg, kt, gl, s0, cols, gdn_norm.reshape(1, -1), y_prev)


def _pad_cols(w, width):
    return jnp.pad(w, ((0, 0), (0, width - w.shape[1])))


def _layout_w_in(w):
    a0, b0 = 0, Q_LORA + KV_LORA + QK_ROPE
    c0 = b0 + 3 * BAND_HEADS * BAND_DIM
    q_a = w[:, a0:a0 + Q_LORA]
    c_kv = w[:, a0 + Q_LORA:a0 + Q_LORA + KV_LORA]
    k_r = w[:, a0 + Q_LORA + KV_LORA:b0]
    band = w[:, b0:c0]
    u = w[:, c0:c0 + C_CONV]
    z = w[:, c0 + C_CONV:c0 + C_CONV + GDN_V]
    a = w[:, c0 + C_CONV + GDN_V:c0 + C_CONV + GDN_V + GDN_HEADS]
    b = w[:, c0 + C_CONV + GDN_V + GDN_HEADS:]
    half = QK_ROPE // 2
    k_sw = jnp.concatenate([k_r[:, half:], k_r[:, :half]], axis=1)
    tail = jnp.concatenate([c_kv, _pad_cols(k_r, LANE), _pad_cols(k_sw, LANE),
                            _pad_cols(a, LANE), _pad_cols(b, LANE)], axis=1)
    return jnp.concatenate([u, q_a, band, z, tail], axis=1).astype(BF16)


def _layout_w_qb(w):
    k = w.shape[0]
    w = w.reshape(k, MLA_HEADS, QK_NOPE + QK_ROPE)
    nope = w[:, :, :QK_NOPE]
    rope = w[:, :, QK_NOPE:]
    half = QK_ROPE // 2
    rope_sw = jnp.concatenate([rope[:, :, half:], rope[:, :, :half]], axis=2)
    pad = lambda t: jnp.pad(t, ((0, 0), (0, 0), (0, LANE - QK_ROPE)))
    out = jnp.concatenate([nope, pad(rope), pad(rope_sw)], axis=1)
    return out.reshape(k, 3 * MLA_HEADS * LANE).astype(BF16)


def _rope_tables(pos):
    inv = 1.0 / (ROPE_THETA ** (jnp.arange(0, QK_ROPE, 2, dtype=F32) / QK_ROPE))
    ang = pos.astype(F32)[:, None] * inv[None, :]
    cos, sin = jnp.cos(ang), jnp.sin(ang)
    zeros = jnp.zeros((pos.shape[0], LANE - QK_ROPE), F32)
    return (jnp.concatenate([cos, cos, zeros], axis=1), jnp.concatenate([-sin, sin, zeros], axis=1))


def _band_bias_row(rel_bias, qchunks):
    heads = rel_bias.shape[0]
    span = _band_span(qchunks)
    left = BAND_ROWS + qchunks * CHUNK - 1 - REL_CLIP
    right = max(span - left - (2 * REL_CLIP + 1), 0)
    row = jnp.concatenate([jnp.broadcast_to(rel_bias[:, -1:], (heads, left)), rel_bias[:, ::-1],
                           jnp.broadcast_to(rel_bias[:, :1], (heads, right))], axis=1)
    return row[:, :span]


def kernel(x_prompt, x_sample, cache_latent, cache_k_rope, cache_band_k, cache_band_v, state_conv, state_delta, norm_ff1, ff1_w1, ff1_w3, ff1_w2, norm_mix, w_in, q_norm, w_qb, kv_norm, w_uk, w_uv, rel_bias, conv_w, a_log, dt_bias, gdn_norm, w_out, norm_ff2, ff2_w1, ff2_w3, ff2_w2, final_norm):
    batch, seq, d = x_prompt.shape
    dbatch, dseq, _ = x_sample.shape
    depth = w_in.shape[0]
    past = cache_latent.shape[2]
    assert dseq == CHUNK and seq % CHUNK == 0 and cache_band_k.shape[2] == BAND_ROWS
    mp, ms = batch * seq, dbatch * dseq
    m = mp + ms
    nchunks = seq // CHUNK
    blk_s = mp // CHUNK
    band_w = BAND_HEADS * BAND_DIM

    xp, xs = x_prompt.reshape(mp, d), x_sample.reshape(ms, d)
    pos = jnp.concatenate([jnp.tile(jnp.arange(seq, dtype=jnp.int32), batch),
                           jnp.tile(past + jnp.arange(dseq, dtype=jnp.int32), dbatch)])
    cos128, sin128 = _rope_tables(pos)
    zero_state = jnp.zeros((batch, GDN_HEADS, GDN_DK, GDN_DV), F32)

    def ffn(x, norm, w13_bf, w2, l, next_w13):
        h = rmsnorm(x, norm[l], BF16)
        hid, (w2_bf,) = matmul_swiglu(h, w13_bf[0], w13_bf[1], 544, 1024, sides=[(w2, l)])
        return matmul_residual(hid, w2_bf, x, 0.5, 272, 1024, sides=next_w13)

    def mixer_rows(width):
        return jnp.zeros((m, width), BF16)

    outs = [[] for _ in range(12)]
    for l in range(depth):
        if l == 0:
            hid, (w2_bf,) = matmul_swiglu(rmsnorm_pair(xp, xs, norm_ff1[0], BF16), (ff1_w1, 0),
                                          (ff1_w3, 0), 544, 512, sides=[(ff1_w2, 0)])
            x, w13_bf = matmul_residual_pair(hid, w2_bf, xp, xs, 0.5, 1024,
                                             sides=[(ff2_w1, 0), (ff2_w3, 0)])
        else:
            x, w13_bf = ffn(x, norm_ff1, w13_bf, ff1_w2, l, [(ff2_w1, l), (ff2_w3, l)])

        h = rmsnorm(x, norm_mix[l], BF16)
        cols, (w_out_bf,) = matmul(h, _layout_w_in(w_in[l]), F32, 544, 1024, sides=[(w_out, l)])

        wuk_t = jnp.transpose(w_uk[l], (1, 2, 0)).astype(BF16)
        wuv_t = jnp.transpose(w_uv[l], (1, 0, 2)).astype(BF16)
        q_lat, q_rope, lat, krope, klat_bf, krope_bf = mla_prep(
            cols, q_norm[l], kv_norm[l], _layout_w_qb(w_qb[l]), wuk_t, cos128, sin128)
        ya = mla_attention_prompt(q_lat, q_rope, klat_bf, krope_bf, wuv_t,
                                  mixer_rows(MLA_HEADS * V_HEAD), batch, seq)
        ya = mla_attention_sample(q_lat, q_rope, cache_latent, cache_k_rope, l, klat_bf, krope_bf,
                                  wuv_t, ya, dbatch, blk_s)

        band_k = cols[:, COL_BK * SLOT:(COL_BK + 1) * SLOT]
        band_v = cols[:, COL_BV * SLOT:(COL_BV + 1) * SLOT]

        def prompt_rows(t):
            t = t[:mp].reshape(batch, seq, band_w).astype(BF16)
            return jnp.pad(t, ((0, 0), (BAND_ROWS, 0), (0, 0)))

        def sample_rows(t, cache):
            return jnp.concatenate([cache.reshape(dbatch, BAND_ROWS, band_w).astype(BF16),
                                    t[mp:].reshape(dbatch, dseq, band_w).astype(BF16)], axis=1)

        yb = band_attention(cols, prompt_rows(band_k), prompt_rows(band_v), rel_bias[l],
                            mixer_rows(band_w), batch, nchunks, 0, BAND_ROWS, math.gcd(nchunks, 4))
        yb = band_attention(cols, sample_rows(band_k, cache_band_k[l]),
                            sample_rows(band_v, cache_band_v[l]), rel_bias[l], yb, dbatch, 1, blk_s, 0, 1)

        conv_init = jnp.concatenate(
            [jnp.zeros((batch, 8, C_CONV), F32),
             jnp.pad(state_conv[l], ((0, 0), (8 - (CONV_W - 1), 0), (0, 0)))], axis=0)
        aneg = _pad_cols(-jnp.exp(a_log[l].astype(F32)).reshape(1, -1), LANE)
        dtb = _pad_cols(dt_bias[l].astype(F32).reshape(1, -1), LANE)
        pre = gdn_pre(cols, conv_init, conv_w[l], aneg, dtb, nchunks, mp // CHUNK)
        yc, delta_p = gdn_recurrence(pre, cols, zero_state, gdn_norm[l], mixer_rows(GDN_V),
                                     batch, nchunks, 0)
        yc, delta_s = gdn_recurrence(pre, cols, state_delta[l], gdn_norm[l], yc, dbatch, 1, blk_s)

        x = matmul_mix(ya, yb, yc, w_out_bf, x, 544, 1024)
        x, w13_bf = ffn(x, norm_ff2, w13_bf, ff2_w2, l,
                        [(ff1_w1, l + 1), (ff1_w3, l + 1)] if l + 1 < depth else [])

        def conv_tail(first_row, rows_per_stream, streams):
            ends = [first_row + (s + 1) * rows_per_stream for s in range(streams)]
            return jnp.stack([cols[e - (CONV_W - 1):e, :C_CONV] for e in ends])

        keep = min(BAND_ROWS, seq)
        layer_out = (
            lat[:mp].reshape(batch, seq, KV_LORA),
            krope[:mp].reshape(batch, seq, QK_ROPE),
            band_k[:mp].reshape(batch, seq, BAND_HEADS, BAND_DIM)[:, seq - keep:],
            band_v[:mp].reshape(batch, seq, BAND_HEADS, BAND_DIM)[:, seq - keep:],
            conv_tail(0, seq, batch),
            delta_p,
            lat[mp:].reshape(dbatch, dseq, KV_LORA),
            krope[mp:].reshape(dbatch, dseq, QK_ROPE),
            band_k[mp:].reshape(dbatch, dseq, BAND_HEADS, BAND_DIM),
            band_v[mp:].reshape(dbatch, dseq, BAND_HEADS, BAND_DIM),
            conv_tail(mp, dseq, dbatch),
            delta_s,
        )
        for acc, t in zip(outs, layer_out):
            acc.append(t)

    y_prompt = rmsnorm(x, final_norm, F32, rows=mp).reshape(batch, seq, d)
    y_sample = rmsnorm(x, final_norm, F32, rows=ms, row_start=mp).reshape(dbatch, dseq, d)
    return (y_prompt, y_sample) + tuple(jnp.stack(t) for t in outs)
```

```python
import functools
import math

import jax
import jax.numpy as jnp
from jax import lax
from jax.experimental import pallas as pl
from jax.experimental.pallas import tpu as pltpu

F32 = jnp.float32
BF16 = jnp.bfloat16
HIGHEST = lax.Precision.HIGHEST

D_MODEL = 4096
CHUNK = 64
CHUNK_SHIFT = 6
EPS = 1e-6
NEG = -1e30
Q_LORA = 1024
KV_LORA = 512
QK_NOPE = 128
QK_ROPE = 64
V_HEAD = 128
MLA_HEADS = 16
ROPE_THETA = 10000.0
MLA_SCALE = (QK_NOPE + QK_ROPE) ** -0.5
MLA_HEAD_GROUPS = 8
BAND_HEADS = 8
BAND_DIM = 128
BAND_PREV = 8
BAND_ROWS = BAND_PREV * CHUNK
BAND_KEYS = BAND_ROWS + CHUNK
REL_CLIP = 128
BAND_SCALE = BAND_DIM ** -0.5
GDN_HEADS = 8
GDN_GROUP = 8
GDN_DK = 128
GDN_DV = 128
CONV_W = 4
GDN_QK = GDN_HEADS * GDN_DK
GDN_V = GDN_HEADS * GDN_DV
C_CONV = 2 * GDN_QK + GDN_V
LANE = 128
SLOT = 1024
COL_U, COL_QA, COL_BQ, COL_BK, COL_BV, COL_Z, COL_TAIL = 0, 3, 4, 5, 6, 7, 8
IN_COLS_PAD = 9 * SLOT
TAIL_KR, TAIL_KSW, TAIL_A, TAIL_B = 512, 640, 768, 896

VMEM_LIMIT_BYTES = 58 * 1024 * 1024


def _params(*sem):
    return pltpu.CompilerParams(dimension_semantics=sem, vmem_limit_bytes=VMEM_LIMIT_BYTES)


def _tile(n, target, mult=16):
    best = None
    for t in range(mult, min(n, target) + 1, mult):
        if n % t == 0:
            best = t
    assert best is not None, (n, target)
    return best


def _dot(a, b, precision=None):
    return jnp.dot(a, b, preferred_element_type=F32, precision=precision)


def _dot_nt(a, b, precision=None):
    return lax.dot_general(a, b, (((1,), (1,)), ((), ())), preferred_element_type=F32,
                           precision=precision)


def _dot_tn(a, b, precision=None):
    return lax.dot_general(a, b, (((0,), (0,)), ((), ())), preferred_element_type=F32,
                           precision=precision)


def _split_bf16(x):
    hi = x.astype(BF16)
    return hi, (x - hi.astype(F32)).astype(BF16)


def _dot_3pass(a, b, expand=lambda t: t):
    a_hi, a_lo = _split_bf16(a)
    b_hi, b_lo = _split_bf16(b)
    m = a.shape[0]
    top = _dot(jnp.concatenate([a_hi, a_lo], axis=0), expand(b_hi))
    return top[:m] + top[m:] + _dot(a_hi, expand(b_lo))


def _sigmoid(x):
    return 1.0 / (1.0 + jnp.exp(-x))


def _rms(x, g):
    return x * lax.rsqrt(jnp.mean(x * x, axis=-1, keepdims=True) + EPS) * g


def _rmsnorm_kernel(x_ref, g_ref, o_ref):
    o_ref[...] = _rms(x_ref[...], g_ref[...]).astype(o_ref.dtype)


def rmsnorm(x, g, out_dtype, rows=None, row_start=0):
    m, d = x.shape
    rows = m if rows is None else rows
    tm = _tile(math.gcd(rows, row_start) if row_start else rows, 256, 8)
    off = row_start // tm
    return pl.pallas_call(
        _rmsnorm_kernel,
        grid=(rows // tm,),
        in_specs=[pl.BlockSpec((tm, d), lambda i: (i + off, 0)),
                  pl.BlockSpec((1, d), lambda i: (0, 0))],
        out_specs=pl.BlockSpec((tm, d), lambda i: (i, 0)),
        out_shape=jax.ShapeDtypeStruct((rows, d), out_dtype),
        compiler_params=_params("parallel"),
        name="rmsnorm",
    )(x, g.reshape(1, d))


def _pair_specs(first, second, tm, width, row_axis, col_fn=None):
    n_first = first.shape[0] // tm
    col = (lambda *g: 0) if col_fn is None else col_fn
    return (n_first,
            pl.BlockSpec((tm, width), lambda *g: (jnp.minimum(g[row_axis], n_first - 1), col(*g))),
            pl.BlockSpec((tm, width), lambda *g: (jnp.maximum(g[row_axis] - n_first, 0), col(*g))))


def _rmsnorm_pair_kernel(xp_ref, xs_ref, g_ref, o_ref, *, n_first):
    @pl.when(pl.program_id(0) < n_first)
    def _():
        o_ref[...] = _rms(xp_ref[...], g_ref[...]).astype(o_ref.dtype)

    @pl.when(pl.program_id(0) >= n_first)
    def _():
        o_ref[...] = _rms(xs_ref[...], g_ref[...]).astype(o_ref.dtype)


def rmsnorm_pair(xp, xs, g, out_dtype):
    d = xp.shape[1]
    tm = _tile(math.gcd(xp.shape[0], xs.shape[0]), 256, 8)
    n_first, spec_p, spec_s = _pair_specs(xp, xs, tm, d, 0)
    m = xp.shape[0] + xs.shape[0]
    return pl.pallas_call(
        functools.partial(_rmsnorm_pair_kernel, n_first=n_first),
        grid=(m // tm,),
        in_specs=[spec_p, spec_s, pl.BlockSpec((1, d), lambda i: (0, 0))],
        out_specs=pl.BlockSpec((tm, d), lambda i: (i, 0)),
        out_shape=jax.ShapeDtypeStruct((m, d), out_dtype),
        compiler_params=_params("parallel"),
        name="rmsnorm_pair",
    )(xp, xs, g.reshape(1, d))


def _cast_weight(w_ref, wb_ref):
    @pl.when(pl.program_id(1) == 0)
    def _():
        wb_ref[...] = w_ref[...].astype(BF16)


def _tiled_kernel(*refs, n_w, n_extra, n_side, is_f32, epilogue, emit_norm):
    a_ref, refs = refs[0], refs[1:]
    w_refs, refs = refs[:n_w], refs[n_w:]
    extra, refs = refs[:n_extra], refs[n_extra:]
    side_in, refs = refs[:n_side], refs[n_side:]
    o_ref, refs = refs[0], refs[1:]
    if emit_norm:
        (hb_ref, ssq_ref), refs = refs[:2], refs[2:]
    side_out, scratch = refs[:n_side], list(refs[n_side:])
    for s_in, s_out in zip(side_in, side_out):
        s_out[...] = s_in[...].astype(BF16)
    ws = []
    for w_ref, f32 in zip(w_refs, is_f32):
        if f32:
            wb_ref = scratch.pop(0)
            _cast_weight(w_ref, wb_ref)
            ws.append(wb_ref)
        else:
            ws.append(w_ref)
    out = epilogue(a_ref[...], ws, extra)
    o_ref[...] = out.astype(o_ref.dtype)
    if emit_norm:
        hb_ref[...] = (out * extra[-1][...]).astype(BF16)
        ssq_ref[...] = jnp.broadcast_to(jnp.sum(out * out, axis=-1, keepdims=True), ssq_ref.shape)


def _scale_rows(t, ssq_ref):
    r = lax.rsqrt(jnp.sum(ssq_ref[...], axis=0) / D_MODEL + EPS)
    return t * jnp.concatenate([r] * (t.shape[1] // LANE), axis=1)


def _swiglu_epilogue(a, ws, extra):
    h1 = _dot(a, ws[0][...])
    h3 = _dot(a, ws[1][...])
    return h1 * _sigmoid(h1) * h3


def _swiglu_scaled_epilogue(a, ws, extra):
    h1 = _scale_rows(_dot(a, ws[0][...]), extra[0])
    h3 = _scale_rows(_dot(a, ws[1][...]), extra[0])
    return h1 * _sigmoid(h1) * h3


def _plain_scaled_epilogue(a, ws, extra):
    return _scale_rows(_dot(a, ws[0][...]), extra[0])


def _mix_epilogue(a, ws, extra):
    yb, yc, res = extra[0][...], extra[1][...], extra[2][...]
    ka, kb = a.shape[1], yb.shape[1]
    w = ws[0]
    return res + _dot(a, w[0:ka, :]) + _dot(yb, w[ka:ka + kb, :]) + _dot(yc, w[ka + kb:, :])


def _residual_epilogue(a, ws, extra, *, scale):
    return extra[0][...] + scale * _dot(a, ws[0][...])


def _residual_pair_epilogue(a, ws, extra, *, scale, n_first):
    res = jnp.where(pl.program_id(1) < n_first, extra[0][...], extra[1][...])
    return res + scale * _dot(a, ws[0][...])


def _tiled_matmul(a, weights, extra, extra_specs, sides, epilogue, out_dtype, tm, tn, name,
                  norm_gain=None):
    m = a.shape[0]
    is_f32 = tuple(isinstance(w, tuple) for w in weights)
    k, n = weights[0][0].shape[1:] if is_f32[0] else weights[0].shape
    nj, ni = n // tn, m // tm
    extra, extra_specs = list(extra), list(extra_specs)
    norm_specs, norm_shapes = [], []
    if norm_gain is not None:
        assert n == D_MODEL
        extra.append(norm_gain.reshape(1, n))
        extra_specs.append(pl.BlockSpec((1, tn), lambda j, i: (0, j)))
        norm_specs = [pl.BlockSpec((tm, tn), lambda j, i: (i, j)),
                      pl.BlockSpec((None, tm, LANE), lambda j, i: (j, i, 0))]
        norm_shapes = [jax.ShapeDtypeStruct((m, n), BF16), jax.ShapeDtypeStruct((nj, m, LANE), F32)]
    w_args, w_specs = [], []
    for w, f32 in zip(weights, is_f32):
        if f32:
            w_args.append(w[0])
            w_specs.append(_layer_weight_spec(w[1], k, tn))
        else:
            w_args.append(w)
            w_specs.append(pl.BlockSpec((k, tn), lambda j, i: (0, j)))
    side_args, side_in_specs, side_out_specs, side_shapes = [], [], [], []
    for w, layer in sides:
        slabs = max(t for t in range(1, nj * ni + 1)
                    if w.shape[1] % t == 0 and (w.shape[1] // t) % 16 == 0)
        rows = w.shape[1] // slabs
        side_args.append(w)
        side_in_specs.append(pl.BlockSpec(
            (None, rows, w.shape[2]), functools.partial(_side_in_map, layer=layer, ni=ni, slabs=slabs)))
        side_out_specs.append(pl.BlockSpec(
            (rows, w.shape[2]), functools.partial(_side_out_map, ni=ni, slabs=slabs)))
        side_shapes.append(jax.ShapeDtypeStruct(w.shape[1:], BF16))
    outs = pl.pallas_call(
        functools.partial(_tiled_kernel, n_w=len(weights), n_extra=len(extra), n_side=len(sides),
                          is_f32=is_f32, epilogue=epilogue, emit_norm=bool(norm_specs)),
        grid=(nj, ni),
        in_specs=[pl.BlockSpec((tm, a.shape[1]), lambda j, i: (i, 0))] + w_specs + extra_specs + side_in_specs,
        out_specs=[pl.BlockSpec((tm, tn), lambda j, i: (i, j))] + norm_specs + side_out_specs,
        out_shape=[jax.ShapeDtypeStruct((m, n), out_dtype)] + norm_shapes + side_shapes,
        scratch_shapes=[pltpu.VMEM((k, tn), BF16) for f32 in is_f32 if f32],
        compiler_params=_params("arbitrary", "arbitrary"),
        name=name,
    )(a, *w_args, *extra, *side_args)
    n_norm = len(norm_specs)
    return outs[0], list(outs[1 + n_norm:]), (tuple(outs[1:1 + n_norm]) if n_norm else None)


def _side_in_map(j, i, *, layer, ni, slabs):
    return (layer, jnp.minimum(j * ni + i, slabs - 1), 0)


def _side_out_map(j, i, *, ni, slabs):
    return (jnp.minimum(j * ni + i, slabs - 1), 0)


def _row_scale_spec(row_scale, tm):
    return pl.BlockSpec((row_scale.shape[0], tm, LANE), lambda j, i: (0, i, 0))


def matmul_scaled(a, row_scale, w, out_dtype, tm_target, tn_target, sides=()):
    tm, tn = _tile(a.shape[0], tm_target), _tile(w.shape[1], tn_target, LANE)
    return _tiled_matmul(a, [w], [row_scale], [_row_scale_spec(row_scale, tm)], sides, _plain_scaled_epilogue,
                         out_dtype, tm, tn, "matmul_scaled")


def _layer_weight_spec(layer, k, tn):
    return pl.BlockSpec((None, k, tn), lambda j, i: (layer, 0, j))


def matmul_residual(a, w, res, scale, tm_target, tn_target, sides=(), norm_gain=None):
    n = res.shape[1]
    tm, tn = _tile(a.shape[0], tm_target), _tile(n, tn_target, LANE)
    return _tiled_matmul(a, [w], [res], [pl.BlockSpec((tm, tn), lambda j, i: (i, j))], sides,
                         functools.partial(_residual_epilogue, scale=scale), F32, tm, tn,
                         "matmul_residual", norm_gain)


def matmul_residual_pair(a, w, res_p, res_s, scale, tn_target, sides=(), norm_gain=None):
    tm = _tile(math.gcd(res_p.shape[0], res_s.shape[0]), 256)
    tn = _tile(res_p.shape[1], tn_target, LANE)
    n_first, spec_p, spec_s = _pair_specs(res_p, res_s, tm, tn, 1, lambda j, i: j)
    return _tiled_matmul(a, [w], [res_p, res_s], [spec_p, spec_s], sides,
                         functools.partial(_residual_pair_epilogue, scale=scale, n_first=n_first),
                         F32, tm, tn, "matmul_residual_pair", norm_gain)


def matmul_mix(ya, yb, yc, w, res, tm_target, tn_target, norm_gain=None):
    tm, tn = _tile(ya.shape[0], tm_target), _tile(w.shape[1], tn_target, LANE)
    row = lambda y: pl.BlockSpec((tm, y.shape[1]), lambda j, i: (i, 0))
    return _tiled_matmul(ya, [w], [yb, yc, res],
                         [row(yb), row(yc), pl.BlockSpec((tm, tn), lambda j, i: (i, j))], (),
                         _mix_epilogue, F32, tm, tn, "matmul_mix", norm_gain)


def matmul_swiglu(a, w1, w3, tm_target, tn_target, sides=(), row_scale=None):
    n = w1[0].shape[2] if isinstance(w1, tuple) else w1.shape[1]
    tm, tn = _tile(a.shape[0], tm_target), _tile(n, tn_target, LANE)
    if row_scale is None:
        return _tiled_matmul(a, [w1, w3], [], [], sides, _swiglu_epilogue, BF16, tm, tn,
                             "matmul_swiglu")
    return _tiled_matmul(a, [w1, w3], [row_scale], [_row_scale_spec(row_scale, tm)], sides,
                         _swiglu_scaled_epilogue, BF16, tm, tn, "matmul_swiglu")


def _mla_prep_kernel(qa_ref, tail_ref, qn_ref, kvn_ref, wq_ref, wuk_ref, cos_ref, sin_ref,
                     qlat_ref, qrope_ref, lat_ref, krope_ref, klat_bf_ref, krope_bf_ref):
    hq = _rms(qa_ref[...], qn_ref[...]).astype(BF16)
    q_all = _dot(hq, wq_ref[...])
    cos = cos_ref[...]
    sin = sin_ref[...]
    hw = MLA_HEADS * LANE
    for h in range(MLA_HEADS):
        qn = q_all[:, h * LANE:(h + 1) * LANE].astype(BF16)
        qlat_ref[h] = (_dot(qn, wuk_ref[h]) * MLA_SCALE).astype(BF16)
        qr = (q_all[:, hw + h * LANE:hw + (h + 1) * LANE] * cos
              + q_all[:, 2 * hw + h * LANE:2 * hw + (h + 1) * LANE] * sin)
        qrope_ref[h] = (qr[:, :QK_ROPE] * MLA_SCALE).astype(BF16)
    tail = tail_ref[...]
    lat = _rms(tail[:, :KV_LORA], kvn_ref[...])
    lat_ref[...] = lat
    klat_bf_ref[...] = lat.astype(BF16)
    kr = tail[:, TAIL_KR:TAIL_KR + LANE] * cos + tail[:, TAIL_KSW:TAIL_KSW + LANE] * sin
    krope_ref[...] = kr[:, :QK_ROPE]
    krope_bf_ref[...] = kr[:, :QK_ROPE].astype(BF16)


def mla_prep(cols, q_norm, kv_norm, wq_all, wuk_t, cos128, sin128):
    m = cols.shape[0]
    tm = _tile(m, 256)
    h = MLA_HEADS
    const2 = lambda i: (0, 0)
    return pl.pallas_call(
        _mla_prep_kernel,
        grid=(m // tm,),
        in_specs=[pl.BlockSpec((tm, SLOT), lambda i: (i, COL_QA)),
                  pl.BlockSpec((tm, SLOT), lambda i: (i, COL_TAIL)),
                  pl.BlockSpec((1, Q_LORA), const2),
                  pl.BlockSpec((1, KV_LORA), const2),
                  pl.BlockSpec(wq_all.shape, const2, pipeline_mode=pl.Buffered(1)),
                  pl.BlockSpec(wuk_t.shape, lambda i: (0, 0, 0), pipeline_mode=pl.Buffered(1)),
                  pl.BlockSpec((tm, LANE), lambda i: (i, 0)),
                  pl.BlockSpec((tm, LANE), lambda i: (i, 0))],
        out_specs=[pl.BlockSpec((h, tm, KV_LORA), lambda i: (0, i, 0)),
                   pl.BlockSpec((h, tm, QK_ROPE), lambda i: (0, i, 0)),
                   pl.BlockSpec((tm, KV_LORA), lambda i: (i, 0)),
                   pl.BlockSpec((tm, QK_ROPE), lambda i: (i, 0)),
                   pl.BlockSpec((tm, KV_LORA), lambda i: (i, 0)),
                   pl.BlockSpec((tm, QK_ROPE), lambda i: (i, 0))],
        out_shape=[jax.ShapeDtypeStruct((h, m, KV_LORA), BF16),
                   jax.ShapeDtypeStruct((h, m, QK_ROPE), BF16),
                   jax.ShapeDtypeStruct((m, KV_LORA), F32),
                   jax.ShapeDtypeStruct((m, QK_ROPE), F32),
                   jax.ShapeDtypeStruct((m, KV_LORA), BF16),
                   jax.ShapeDtypeStruct((m, QK_ROPE), BF16)],
        compiler_params=_params("parallel"),
        name="mla_prep",
    )(cols, cols, q_norm.reshape(1, -1), kv_norm.reshape(1, -1), wq_all, wuk_t, cos128, sin128)


def _softmax_init(m_ref, l_ref, acc_ref):
    m_ref[...] = jnp.full(m_ref.shape, NEG, F32)
    l_ref[...] = jnp.zeros(l_ref.shape, F32)
    acc_ref[...] = jnp.zeros(acc_ref.shape, F32)


def _mla_step(ql_ref, qr_ref, k_l, k_r, m_ref, l_ref, acc_ref, tq, visible=None):
    hg = MLA_HEADS // MLA_HEAD_GROUPS
    rows = hg * tq

    def scores(g):
        q_l = ql_ref[g * hg:(g + 1) * hg].reshape(rows, KV_LORA)
        q_r = qr_ref[g * hg:(g + 1) * hg].reshape(rows, QK_ROPE)
        s = _dot_nt(q_l, k_l) + _dot_nt(q_r, k_r)
        return s if visible is None else jnp.where(visible, s, NEG)

    def update(g, s):
        rs = slice(g * rows, (g + 1) * rows)
        m_old = m_ref[rs]
        m_new = jnp.maximum(m_old, jnp.max(s, axis=-1, keepdims=True))
        alpha = jnp.exp(m_old - m_new)
        p = jnp.exp(s - m_new)
        l_ref[rs] = alpha * l_ref[rs] + jnp.sum(p, axis=-1, keepdims=True)
        acc_ref[rs] = alpha * acc_ref[rs] + _dot(p.astype(BF16), k_l)
        m_ref[rs] = m_new

    s = scores(0)
    for g in range(1, MLA_HEAD_GROUPS):
        s_next = scores(g)
        update(g - 1, s)
        s = s_next
    update(MLA_HEAD_GROUPS - 1, s)


def _mla_finish(wuv_ref, o_ref, l_ref, acc_ref, tq):
    ctx = acc_ref[...] / l_ref[...]
    for h in range(MLA_HEADS):
        ch = ctx[h * tq:(h + 1) * tq].astype(BF16)
        o_ref[:, h * V_HEAD:(h + 1) * V_HEAD] = _dot(ch, wuv_ref[h]).astype(o_ref.dtype)


def _mla_prompt_kernel(ql_ref, qr_ref, kl_ref, kr_ref, wuv_ref, y_prev_ref, o_ref,
                       m_ref, l_ref, acc_ref, *, tq, tk):
    del y_prev_ref
    qi = pl.program_id(1)
    kj = pl.program_id(2)
    last = (qi * tq + tq - 1) // tk

    @pl.when(kj == 0)
    def _():
        _softmax_init(m_ref, l_ref, acc_ref)

    @pl.when(kj < last)
    def _():
        _mla_step(ql_ref, qr_ref, kl_ref[...], kr_ref[...], m_ref, l_ref, acc_ref, tq)

    @pl.when(kj == last)
    def _():
        shape = (MLA_HEADS // MLA_HEAD_GROUPS * tq, tk)
        row = lax.broadcasted_iota(jnp.int32, shape, 0)
        col = lax.broadcasted_iota(jnp.int32, shape, 1)
        q_chunk = (qi * tq + row % tq) // CHUNK
        k_chunk = (kj * tk + col) // CHUNK
        _mla_step(ql_ref, qr_ref, kl_ref[...], kr_ref[...], m_ref, l_ref, acc_ref, tq,
                  visible=k_chunk <= q_chunk)

    @pl.when(kj == pl.num_programs(2) - 1)
    def _():
        _mla_finish(wuv_ref, o_ref, l_ref, acc_ref, tq)


def mla_attention_prompt(q_lat, q_rope, k_lat, k_rope, wuv, y_prev, batch, seq):
    tq = 128
    tk = _tile(seq, 512, CHUNK)
    nq, nk = seq // tq, seq // tk
    h = MLA_HEADS
    rows = h * tq

    def k_map(b, qi, kj):
        return (b * nk + jnp.minimum(kj, (qi * tq + tq - 1) // tk), 0)

    return pl.pallas_call(
        functools.partial(_mla_prompt_kernel, tq=tq, tk=tk),
        grid=(batch, nq, nk),
        in_specs=[pl.BlockSpec((h, tq, KV_LORA), lambda b, qi, kj: (0, b * nq + qi, 0)),
                  pl.BlockSpec((h, tq, QK_ROPE), lambda b, qi, kj: (0, b * nq + qi, 0)),
                  pl.BlockSpec((tk, KV_LORA), k_map),
                  pl.BlockSpec((tk, QK_ROPE), k_map),
                  pl.BlockSpec(wuv.shape, lambda b, qi, kj: (0, 0, 0)),
                  pl.BlockSpec(memory_space=pl.ANY)],
        out_specs=pl.BlockSpec((tq, h * V_HEAD), lambda b, qi, kj: (b * nq + qi, 0)),
        out_shape=jax.ShapeDtypeStruct(y_prev.shape, y_prev.dtype),
        input_output_aliases={5: 0},
        scratch_shapes=[pltpu.VMEM((rows, 1), F32), pltpu.VMEM((rows, 1), F32),
                        pltpu.VMEM((rows, KV_LORA), F32)],
        compiler_params=_params("parallel", "parallel", "arbitrary"),
        name="mla_attention_prompt",
    )(q_lat, q_rope, k_lat, k_rope, wuv, y_prev)


def _mla_sample_kernel(ql_ref, qr_ref, pl_ref, pr_ref, kl_ref, kr_ref, wuv_ref, y_prev_ref, o_ref,
                       m_ref, l_ref, acc_ref):
    del y_prev_ref
    kj = pl.program_id(1)
    n_past = pl.num_programs(1) - 1

    @pl.when(kj == 0)
    def _():
        _softmax_init(m_ref, l_ref, acc_ref)

    @pl.when(kj < n_past)
    def _():
        _mla_step(ql_ref, qr_ref, pl_ref[0].astype(BF16), pr_ref[0].astype(BF16),
                  m_ref, l_ref, acc_ref, CHUNK)

    @pl.when(kj == n_past)
    def _():
        _mla_step(ql_ref, qr_ref, kl_ref[...], kr_ref[...], m_ref, l_ref, acc_ref, CHUNK)
        _mla_finish(wuv_ref, o_ref, l_ref, acc_ref, CHUNK)


def mla_attention_sample(q_lat, q_rope, past_lat, past_rope, layer, k_lat, k_rope, wuv, y_prev,
                         batch, tok_block0):
    past = past_lat.shape[2]
    tk = _tile(past, 512, CHUNK)
    n_past = past // tk
    h = MLA_HEADS
    rows = h * CHUNK
    past_map = lambda b, kj: (layer, b, jnp.minimum(kj, n_past - 1), 0)
    return pl.pallas_call(
        _mla_sample_kernel,
        grid=(batch, n_past + 1),
        in_specs=[pl.BlockSpec((h, CHUNK, KV_LORA), lambda b, kj: (0, tok_block0 + b, 0)),
                  pl.BlockSpec((h, CHUNK, QK_ROPE), lambda b, kj: (0, tok_block0 + b, 0)),
                  pl.BlockSpec((None, 1, tk, KV_LORA), past_map),
                  pl.BlockSpec((None, 1, tk, QK_ROPE), past_map),
                  pl.BlockSpec((CHUNK, KV_LORA), lambda b, kj: (tok_block0 + b, 0)),
                  pl.BlockSpec((CHUNK, QK_ROPE), lambda b, kj: (tok_block0 + b, 0)),
                  pl.BlockSpec(wuv.shape, lambda b, kj: (0, 0, 0)),
                  pl.BlockSpec(memory_space=pl.ANY)],
        out_specs=pl.BlockSpec((CHUNK, h * V_HEAD), lambda b, kj: (tok_block0 + b, 0)),
        out_shape=jax.ShapeDtypeStruct(y_prev.shape, y_prev.dtype),
        input_output_aliases={7: 0},
        scratch_shapes=[pltpu.VMEM((rows, 1), F32), pltpu.VMEM((rows, 1), F32),
                        pltpu.VMEM((rows, KV_LORA), F32)],
        compiler_params=_params("parallel", "arbitrary"),
        name="mla_attention_sample",
    )(q_lat, q_rope, past_lat, past_rope, k_lat, k_rope, wuv, y_prev)


def _band_span(qchunks):
    tq, nk = qchunks * CHUNK, (qchunks + BAND_PREV) * CHUNK
    return -(-(tq + nk - 1) // LANE) * LANE


def _band_kernel(q_ref, k_ref, v_ref, g_ref, y_prev_ref, o_ref, bias_ref, *, pad_rows, qchunks):
    del y_prev_ref
    n = pl.program_id(1)
    tq, nk = qchunks * CHUNK, (qchunks + BAND_PREV) * CHUNK
    span = _band_span(qchunks)

    @pl.when(jnp.logical_and(pl.program_id(0) == 0, n == 0))
    def _():
        for h in range(BAND_HEADS):
            rows = jnp.broadcast_to(g_ref[h:h + 1, :], (tq, span))
            rolled = pltpu.roll(rows, span - (tq - 1), 1, stride=1, stride_axis=0)
            bias_ref[h] = rolled[:, :nk]

    start = pl.multiple_of(n * tq, CHUNK)
    row = lax.broadcasted_iota(jnp.int32, (tq, nk), 0)
    col = lax.broadcasted_iota(jnp.int32, (tq, nk), 1)
    visible = start + col >= pad_rows
    if qchunks > 1:
        ahead = (col >> CHUNK_SHIFT) - (row >> CHUNK_SHIFT)
        visible = visible & (ahead >= 0) & (ahead <= BAND_PREV)
    q = q_ref[...]
    for h in range(BAND_HEADS):
        lanes = slice(h * BAND_DIM, (h + 1) * BAND_DIM)
        qh = q[:, lanes].astype(BF16)
        kh = k_ref[0, pl.ds(start, nk), lanes]
        vh = v_ref[0, pl.ds(start, nk), lanes]
        s = _dot_nt(qh, kh) * BAND_SCALE + bias_ref[h]
        if pad_rows or qchunks > 1:
            s = jnp.where(visible, s, NEG)
        p = jnp.exp(s - jnp.max(s, axis=-1, keepdims=True))
        o = _dot(p.astype(BF16), vh) / jnp.sum(p, axis=-1, keepdims=True)
        o_ref[:, lanes] = o.astype(o_ref.dtype)


def band_attention(cols, k_rows, v_rows, rel_bias, y_prev, batch, nchunks, tok_block0, pad_rows, qchunks):
    assert nchunks % qchunks == 0 and tok_block0 % qchunks == 0
    width = BAND_HEADS * BAND_DIM
    rows = k_rows.shape[1]
    tq, nk = qchunks * CHUNK, (qchunks + BAND_PREV) * CHUNK
    nblk = nchunks // qchunks
    bias_row = _band_bias_row(rel_bias, qchunks)
    tok = lambda b, n: tok_block0 // qchunks + b * nblk + n
    return pl.pallas_call(
        functools.partial(_band_kernel, pad_rows=pad_rows, qchunks=qchunks),
        grid=(batch, nblk),
        in_specs=[pl.BlockSpec((tq, width), lambda b, n: (tok(b, n), COL_BQ)),
                  pl.BlockSpec((1, rows, width), lambda b, n: (b, 0, 0)),
                  pl.BlockSpec((1, rows, width), lambda b, n: (b, 0, 0)),
                  pl.BlockSpec(bias_row.shape, lambda b, n: (0, 0)),
                  pl.BlockSpec(memory_space=pl.ANY)],
        out_specs=pl.BlockSpec((tq, width), lambda b, n: (tok(b, n), 0)),
        out_shape=jax.ShapeDtypeStruct(y_prev.shape, y_prev.dtype),
        input_output_aliases={4: 0},
        scratch_shapes=[pltpu.VMEM((BAND_HEADS, tq, nk), F32)],
        compiler_params=_params("arbitrary", "arbitrary"),
        name="band_attention",
    )(cols, k_rows, v_rows, bias_row, y_prev)


def _expand_matrix(rows, first, width, per_shift):
    r = lax.broadcasted_iota(jnp.int32, (rows, width), 0)
    c = lax.broadcasted_iota(jnp.int32, (rows, width), 1)
    return (r == first + (c >> per_shift)).astype(F32)


def _unit_lower_inverse(a, row, col, group):
    gl = a.shape[1]
    r = lax.broadcasted_iota(jnp.int32, (gl, gl), 0)
    c = lax.broadcasted_iota(jnp.int32, (gl, gl), 1)
    diag_blocks = ((r >> CHUNK_SHIFT) == (c >> CHUNK_SHIFT)).astype(BF16)

    def block_diag(w):
        return jnp.concatenate([w] * group, axis=0) * diag_blocks

    t = (row == col).astype(F32) - jnp.where((row >> 1) == (col >> 1), a, 0.0)
    for shift in range(1, CHUNK_SHIFT):
        lower_left = ((row >> (shift + 1)) == (col >> (shift + 1))) & ((row >> shift) != (col >> shift))
        x = _dot_3pass(t, jnp.where(lower_left, a, 0.0), block_diag)
        t = t - _dot_3pass(x, t, block_diag)
    return t


def _l2norm_heads(x, heads, width):
    parts = []
    for h in range(heads):
        xh = x[:, h * width:(h + 1) * width]
        parts.append(xh * lax.rsqrt(jnp.sum(xh * xh, axis=-1, keepdims=True) + EPS))
    return jnp.concatenate(parts, axis=1)


def _gdn_pre_kernel(u_ref, uprev_ref, init_ref, tail_ref, cw_ref, aneg_ref, dtb_ref,
                    uv_ref, wk_ref, attn_ref, qg_ref, kt_ref, gl_ref, ext_ref,
                    *, chunks_per_seq, n_prompt_chunks):
    group = GDN_GROUP
    gw = group * CHUNK
    c = pl.program_id(0)
    is_start = jnp.logical_or(c >= n_prompt_chunks, c % chunks_per_seq == 0)
    ext_ref[0:8, :] = jnp.where(is_start, init_ref[0], uprev_ref[...])
    ext_ref[8:8 + CHUNK, :] = u_ref[...]
    cw = cw_ref[...]
    uc = (cw[3:4] * ext_ref[8:8 + CHUNK, :] + cw[2:3] * ext_ref[7:7 + CHUNK, :]
          + cw[1:2] * ext_ref[6:6 + CHUNK, :] + cw[0:1] * ext_ref[5:5 + CHUNK, :])
    uc = uc * _sigmoid(uc)

    tail = tail_ref[...]
    xa = tail[:, TAIL_A:TAIL_A + LANE] + dtb_ref[...]
    softplus = jnp.maximum(xa, 0.0) + jnp.log(1.0 + jnp.exp(-jnp.abs(xa)))
    g = aneg_ref[...] * softplus
    beta = _sigmoid(tail[:, TAIL_B:TAIL_B + LANE])

    r64 = lax.broadcasted_iota(jnp.int32, (CHUNK, CHUNK), 0)
    c64 = lax.broadcasted_iota(jnp.int32, (CHUNK, CHUNK), 1)
    gc = _dot((r64 >= c64).astype(F32), g, HIGHEST)
    egc = jnp.exp(gc)
    ekt = jnp.exp(gc[CHUNK - 1:CHUNK, :] - gc)
    e128 = _expand_matrix(LANE, 0, GDN_HEADS * GDN_DK, 7)
    beta_x = _dot(beta, e128, HIGHEST)
    egc_x = _dot(egc, e128, HIGHEST)
    ekt_x = _dot(ekt, e128, HIGHEST)
    gl_ref[0] = egc_x[CHUNK - 1:CHUNK, :]

    qn = _l2norm_heads(uc[:, :GDN_QK], GDN_HEADS, GDN_DK) * (GDN_DK ** -0.5)
    kn = _l2norm_heads(uc[:, GDN_QK:2 * GDN_QK], GDN_HEADS, GDN_DK)
    v_all = uc[:, 2 * GDN_QK:]
    qg_ref[...] = (qn * egc_x).astype(BF16)
    kt_ref[...] = (kn * ekt_x).astype(BF16)
    rhs_v = beta_x * v_all
    rhs_k = beta_x * egc_x * kn

    row = lax.broadcasted_iota(jnp.int32, (CHUNK, gw), 0)
    col = lax.broadcasted_iota(jnp.int32, (CHUNK, gw), 1) & (CHUNK - 1)
    incl = row >= col
    head_r = lax.broadcasted_iota(jnp.int32, (gw, group * GDN_DK), 0) >> CHUNK_SHIFT
    head_c = lax.broadcasted_iota(jnp.int32, (gw, group * GDN_DK), 1) >> 7
    for grp in range(GDN_HEADS // group):
        lanes = slice(grp * group * GDN_DK, (grp + 1) * group * GDN_DK)
        e64 = _expand_matrix(LANE, grp * group, gw, CHUNK_SHIFT)
        gcw = _dot(gc, e64, HIGHEST)
        bw = _dot(beta, e64, HIGHEST)
        gc_t = jnp.sum(jnp.where(row == col, gcw, 0.0), axis=0, keepdims=True)
        decay = jnp.where(incl, jnp.exp(jnp.where(incl, gcw - gc_t, 0.0)), 0.0)
        kg = kn[:, lanes]
        k_rows = jnp.where(head_r == head_c, jnp.concatenate([kg] * group, axis=0), 0.0).astype(BF16)
        kk = _dot_nt(kg.astype(BF16), k_rows)
        qk = _dot_nt(qn[:, lanes].astype(BF16), k_rows)
        a_mat = jnp.where(row > col, bw * kk * decay, 0.0)
        t_inv = _unit_lower_inverse(a_mat, row, col, group)
        attn_ref[:, grp * gw:(grp + 1) * gw] = (qk * decay).astype(BF16)
        for hh in range(group):
            h = grp * group + hh
            hl = slice(h * GDN_DK, (h + 1) * GDN_DK)
            rhs = jnp.concatenate([rhs_v[:, hl], rhs_k[:, hl]], axis=1)
            sol = _dot_3pass(t_inv[:, hh * CHUNK:(hh + 1) * CHUNK], rhs)
            uv_ref[:, hl] = sol[:, :GDN_DV]
            wk_ref[:, hl] = sol[:, GDN_DV:].astype(BF16)


def gdn_pre(cols, conv_init, conv_w, aneg, dtb, chunks_per_seq, n_prompt_chunks):
    m = cols.shape[0]
    nch = m // CHUNK
    h = GDN_HEADS

    def init_map(c):
        return (jnp.where(c < n_prompt_chunks, c // chunks_per_seq,
                          n_prompt_chunks // chunks_per_seq + c - n_prompt_chunks), 0, 0)

    tok_spec = lambda d: pl.BlockSpec((CHUNK, d), lambda c: (c, 0))
    return pl.pallas_call(
        functools.partial(_gdn_pre_kernel, chunks_per_seq=chunks_per_seq,
                          n_prompt_chunks=n_prompt_chunks),
        grid=(nch,),
        in_specs=[pl.BlockSpec((CHUNK, C_CONV), lambda c: (c, COL_U)),
                  pl.BlockSpec((8, C_CONV), lambda c: (jnp.maximum(c * (CHUNK // 8) - 1, 0), COL_U)),
                  pl.BlockSpec((1, 8, C_CONV), init_map),
                  pl.BlockSpec((CHUNK, SLOT), lambda c: (c, COL_TAIL)),
                  pl.BlockSpec((CONV_W, C_CONV), lambda c: (0, 0)),
                  pl.BlockSpec((1, LANE), lambda c: (0, 0)),
                  pl.BlockSpec((1, LANE), lambda c: (0, 0))],
        out_specs=[tok_spec(GDN_V), tok_spec(GDN_QK), tok_spec(h * CHUNK), tok_spec(GDN_QK),
                   tok_spec(GDN_QK), pl.BlockSpec((1, 1, GDN_QK), lambda c: (c, 0, 0))],
        out_shape=[jax.ShapeDtypeStruct((m, GDN_V), F32),
                   jax.ShapeDtypeStruct((m, GDN_QK), BF16),
                   jax.ShapeDtypeStruct((m, h * CHUNK), BF16),
                   jax.ShapeDtypeStruct((m, GDN_QK), BF16),
                   jax.ShapeDtypeStruct((m, GDN_QK), BF16),
                   jax.ShapeDtypeStruct((nch, 1, GDN_QK), F32)],
        scratch_shapes=[pltpu.VMEM((8 + CHUNK, C_CONV), F32)],
        compiler_params=_params("parallel"),
        name="gdn_pre",
    )(cols, cols, conv_init, cols, conv_w, aneg, dtb)


def _gdn_rec_kernel(uv_ref, wk_ref, attn_ref, qg_ref, kt_ref, gl_ref, s0_ref, z_ref, gn_ref,
                    y_prev_ref, y_ref, s_ref):
    del y_prev_ref

    @pl.when(pl.program_id(1) == 0)
    def _():
        s_ref[...] = s0_ref[...]

    z = z_ref[...]
    gn = gn_ref[...]
    heads = range(GDN_HEADS)
    lanes = [slice(h * GDN_DK, (h + 1) * GDN_DK) for h in heads]
    s_old = [s_ref[0, h] for h in heads]
    s_bf = [s.astype(BF16) for s in s_old]
    ws = [_dot(wk_ref[:, lanes[h]], s_bf[h]) for h in heads]
    qs = [_dot(qg_ref[:, lanes[h]], s_bf[h]) for h in heads]
    u_bf = [(uv_ref[:, lanes[h]] - ws[h]).astype(BF16) for h in heads]
    outs = [qs[h] + _dot(attn_ref[:, h * CHUNK:(h + 1) * CHUNK], u_bf[h]) for h in heads]
    for h in heads:
        s_ref[0, h] = gl_ref[0, :, lanes[h]] * s_old[h] + _dot_tn(kt_ref[:, lanes[h]], u_bf[h])
    for h in heads:
        zh = z[:, lanes[h]]
        y_ref[:, lanes[h]] = (_rms(outs[h], gn) * (zh * _sigmoid(zh))).astype(y_ref.dtype)


def gdn_recurrence(pre, cols, s0, gdn_norm, y_prev, batch, nchunks, tok_block0):
    uv, wk, attn, qg, kt, gl = pre
    h = GDN_HEADS
    tok = lambda b, n: tok_block0 + b * nchunks + n
    tok_spec = lambda d: pl.BlockSpec((CHUNK, d), lambda b, n: (tok(b, n), 0))
    state_spec = pl.BlockSpec((1, h, GDN_DK, GDN_DV), lambda b, n: (b, 0, 0, 0))
    return pl.pallas_call(
        _gdn_rec_kernel,
        grid=(batch, nchunks),
        in_specs=[tok_spec(GDN_V), tok_spec(GDN_QK), tok_spec(h * CHUNK), tok_spec(GDN_QK),
                  tok_spec(GDN_QK),
                  pl.BlockSpec((1, 1, GDN_QK), lambda b, n: (tok(b, n), 0, 0)),
                  state_spec,
                  pl.BlockSpec((CHUNK, SLOT), lambda b, n: (tok(b, n), COL_Z)),
                  pl.BlockSpec((1, GDN_DV), lambda b, n: (0, 0)),
                  pl.BlockSpec(memory_space=pl.ANY)],
        out_specs=[tok_spec(GDN_V), state_spec],
        out_shape=[jax.ShapeDtypeStruct(y_prev.shape, y_prev.dtype),
                   jax.ShapeDtypeStruct((batch, h, GDN_DK, GDN_DV), F32)],
        input_output_aliases={9: 0},
        compiler_params=_params("parallel", "arbitrary"),
        name="gdn_recurrence",
    )(uv, wk, attn, qg, kt, gl, s0, cols, gdn_norm.reshape(1, -1), y_prev)


def _pad_cols(w, width):
    return jnp.pad(w, ((0, 0), (0, width - w.shape[1])))


def _layout_w_in(w):
    a0, b0 = 0, Q_LORA + KV_LORA + QK_ROPE
    c0 = b0 + 3 * BAND_HEADS * BAND_DIM
    q_a = w[:, a0:a0 + Q_LORA]
    c_kv = w[:, a0 + Q_LORA:a0 + Q_LORA + KV_LORA]
    k_r = w[:, a0 + Q_LORA + KV_LORA:b0]
    band = w[:, b0:c0]
    u = w[:, c0:c0 + C_CONV]
    z = w[:, c0 + C_CONV:c0 + C_CONV + GDN_V]
    a = w[:, c0 + C_CONV + GDN_V:c0 + C_CONV + GDN_V + GDN_HEADS]
    b = w[:, c0 + C_CONV + GDN_V + GDN_HEADS:]
    half = QK_ROPE // 2
    k_sw = jnp.concatenate([k_r[:, half:], k_r[:, :half]], axis=1)
    tail = jnp.concatenate([c_kv, _pad_cols(k_r, LANE), _pad_cols(k_sw, LANE),
                            _pad_cols(a, LANE), _pad_cols(b, LANE)], axis=1)
    return jnp.concatenate([u, q_a, band, z, tail], axis=1).astype(BF16)


def _layout_w_qb(w):
    k = w.shape[0]
    w = w.reshape(k, MLA_HEADS, QK_NOPE + QK_ROPE)
    nope = w[:, :, :QK_NOPE]
    rope = w[:, :, QK_NOPE:]
    half = QK_ROPE // 2
    rope_sw = jnp.concatenate([rope[:, :, half:], rope[:, :, :half]], axis=2)
    pad = lambda t: jnp.pad(t, ((0, 0), (0, 0), (0, LANE - QK_ROPE)))
    out = jnp.concatenate([nope, pad(rope), pad(rope_sw)], axis=1)
    return out.reshape(k, 3 * MLA_HEADS * LANE).astype(BF16)


def _rope_tables(pos):
    inv = 1.0 / (ROPE_THETA ** (jnp.arange(0, QK_ROPE, 2, dtype=F32) / QK_ROPE))
    ang = pos.astype(F32)[:, None] * inv[None, :]
    cos, sin = jnp.cos(ang), jnp.sin(ang)
    zeros = jnp.zeros((pos.shape[0], LANE - QK_ROPE), F32)
    return (jnp.concatenate([cos, cos, zeros], axis=1), jnp.concatenate([-sin, sin, zeros], axis=1))


def _band_bias_row(rel_bias, qchunks):
    heads = rel_bias.shape[0]
    span = _band_span(qchunks)
    left = BAND_ROWS + qchunks * CHUNK - 1 - REL_CLIP
    right = max(span - left - (2 * REL_CLIP + 1), 0)
    row = jnp.concatenate([jnp.broadcast_to(rel_bias[:, -1:], (heads, left)), rel_bias[:, ::-1],
                           jnp.broadcast_to(rel_bias[:, :1], (heads, right))], axis=1)
    return row[:, :span]


def kernel(x_prompt, x_sample, cache_latent, cache_k_rope, cache_band_k, cache_band_v, state_conv, state_delta, norm_ff1, ff1_w1, ff1_w3, ff1_w2, norm_mix, w_in, q_norm, w_qb, kv_norm, w_uk, w_uv, rel_bias, conv_w, a_log, dt_bias, gdn_norm, w_out, norm_ff2, ff2_w1, ff2_w3, ff2_w2, final_norm):
    batch, seq, d = x_prompt.shape
    dbatch, dseq, _ = x_sample.shape
    depth = w_in.shape[0]
    past = cache_latent.shape[2]
    assert d == D_MODEL and dseq == CHUNK and seq % CHUNK == 0 and cache_band_k.shape[2] == BAND_ROWS
    mp, ms = batch * seq, dbatch * dseq
    m = mp + ms
    nchunks = seq // CHUNK
    blk_s = mp // CHUNK
    band_w = BAND_HEADS * BAND_DIM

    xp, xs = x_prompt.reshape(mp, d), x_sample.reshape(ms, d)
    pos = jnp.concatenate([jnp.tile(jnp.arange(seq, dtype=jnp.int32), batch),
                           jnp.tile(past + jnp.arange(dseq, dtype=jnp.int32), dbatch)])
    cos128, sin128 = _rope_tables(pos)
    zero_state = jnp.zeros((batch, GDN_HEADS, GDN_DK, GDN_DV), F32)

    def ffn(x, normed, w13_bf, w2, l, next_w13, next_gain):
        hid, (w2_bf,), _ = matmul_swiglu(normed[0], w13_bf[0], w13_bf[1], 544, 1024, sides=[(w2, l)],
                                         row_scale=normed[1])
        return matmul_residual(hid, w2_bf, x, 0.5, 272, 1024, sides=next_w13, norm_gain=next_gain)

    def mixer_rows(width):
        return jnp.zeros((m, width), BF16)

    outs = [[] for _ in range(12)]
    for l in range(depth):
        if l == 0:
            hid, (w2_bf,), _ = matmul_swiglu(rmsnorm_pair(xp, xs, norm_ff1[0], BF16), (ff1_w1, 0),
                                             (ff1_w3, 0), 544, 512, sides=[(ff1_w2, 0)])
            x, w13_bf, normed = matmul_residual_pair(hid, w2_bf, xp, xs, 0.5, 1024,
                                                     sides=[(ff2_w1, 0), (ff2_w3, 0)],
                                                     norm_gain=norm_mix[0])
        else:
            x, w13_bf, normed = ffn(x, normed, w13_bf, ff1_w2, l, [(ff2_w1, l), (ff2_w3, l)],
                                    norm_mix[l])

        cols, (w_out_bf,), _ = matmul_scaled(normed[0], normed[1], _layout_w_in(w_in[l]), F32,
                                             544, 1024, sides=[(w_out, l)])

        wuk_t = jnp.transpose(w_uk[l], (1, 2, 0)).astype(BF16)
        wuv_t = jnp.transpose(w_uv[l], (1, 0, 2)).astype(BF16)
        q_lat, q_rope, lat, krope, klat_bf, krope_bf = mla_prep(
            cols, q_norm[l], kv_norm[l], _layout_w_qb(w_qb[l]), wuk_t, cos128, sin128)
        ya = mla_attention_prompt(q_lat, q_rope, klat_bf, krope_bf, wuv_t,
                                  mixer_rows(MLA_HEADS * V_HEAD), batch, seq)
        ya = mla_attention_sample(q_lat, q_rope, cache_latent, cache_k_rope, l, klat_bf, krope_bf,
                                  wuv_t, ya, dbatch, blk_s)

        band_k = cols[:, COL_BK * SLOT:(COL_BK + 1) * SLOT]
        band_v = cols[:, COL_BV * SLOT:(COL_BV + 1) * SLOT]

        def prompt_rows(t):
            t = t[:mp].reshape(batch, seq, band_w).astype(BF16)
            return jnp.pad(t, ((0, 0), (BAND_ROWS, 0), (0, 0)))

        def sample_rows(t, cache):
            return jnp.concatenate([cache.reshape(dbatch, BAND_ROWS, band_w).astype(BF16),
                                    t[mp:].reshape(dbatch, dseq, band_w).astype(BF16)], axis=1)

        yb = band_attention(cols, prompt_rows(band_k), prompt_rows(band_v), rel_bias[l],
                            mixer_rows(band_w), batch, nchunks, 0, BAND_ROWS, math.gcd(nchunks, 4))
        yb = band_attention(cols, sample_rows(band_k, cache_band_k[l]),
                            sample_rows(band_v, cache_band_v[l]), rel_bias[l], yb, dbatch, 1, blk_s, 0, 1)

        conv_init = jnp.concatenate(
            [jnp.zeros((batch, 8, C_CONV), F32),
             jnp.pad(state_conv[l], ((0, 0), (8 - (CONV_W - 1), 0), (0, 0)))], axis=0)
        aneg = _pad_cols(-jnp.exp(a_log[l].astype(F32)).reshape(1, -1), LANE)
        dtb = _pad_cols(dt_bias[l].astype(F32).reshape(1, -1), LANE)
        pre = gdn_pre(cols, conv_init, conv_w[l], aneg, dtb, nchunks, mp // CHUNK)
        yc, delta_p = gdn_recurrence(pre, cols, zero_state, gdn_norm[l], mixer_rows(GDN_V),
                                     batch, nchunks, 0)
        yc, delta_s = gdn_recurrence(pre, cols, state_delta[l], gdn_norm[l], yc, dbatch, 1, blk_s)

        x, _, normed = matmul_mix(ya, yb, yc, w_out_bf, x, 544, 1024, norm_gain=norm_ff2[l])
        last = l + 1 == depth
        x, w13_bf, normed = ffn(x, normed, w13_bf, ff2_w2, l,
                                [] if last else [(ff1_w1, l + 1), (ff1_w3, l + 1)],
                                None if last else norm_ff1[l + 1])

        def conv_tail(first_row, rows_per_stream, streams):
            ends = [first_row + (s + 1) * rows_per_stream for s in range(streams)]
            return jnp.stack([cols[e - (CONV_W - 1):e, :C_CONV] for e in ends])

        keep = min(BAND_ROWS, seq)
        layer_out = (
            lat[:mp].reshape(batch, seq, KV_LORA),
            krope[:mp].reshape(batch, seq, QK_ROPE),
            band_k[:mp].reshape(batch, seq, BAND_HEADS, BAND_DIM)[:, seq - keep:],
            band_v[:mp].reshape(batch, seq, BAND_HEADS, BAND_DIM)[:, seq - keep:],
            conv_tail(0, seq, batch),
            delta_p,
            lat[mp:].reshape(dbatch, dseq, KV_LORA),
            krope[mp:].reshape(dbatch, dseq, QK_ROPE),
            band_k[mp:].reshape(dbatch, dseq, BAND_HEADS, BAND_DIM),
            band_v[mp:].reshape(dbatch, dseq, BAND_HEADS, BAND_DIM),
            conv_tail(mp, dseq, dbatch),
            delta_s,
        )
        for acc, t in zip(outs, layer_out):
            acc.append(t)

    y_prompt = rmsnorm(x, final_norm, F32, rows=mp).reshape(batch, seq, d)
    y_sample = rmsnorm(x, final_norm, F32, rows=ms, row_start=mp).reshape(dbatch, dseq, d)
    return (y_prompt, y_sample) + tuple(jnp.stack(t) for t in outs)
```

```python
import functools
import math

import jax
import jax.numpy as jnp
from jax import lax
from jax.experimental import pallas as pl
from jax.experimental.pallas import tpu as pltpu

F32 = jnp.float32
BF16 = jnp.bfloat16
HIGHEST = lax.Precision.HIGHEST

D_MODEL = 4096
CHUNK = 64
CHUNK_SHIFT = 6
EPS = 1e-6
NEG = -1e30
Q_LORA = 1024
KV_LORA = 512
QK_NOPE = 128
QK_ROPE = 64
V_HEAD = 128
MLA_HEADS = 16
ROPE_THETA = 10000.0
MLA_SCALE = (QK_NOPE + QK_ROPE) ** -0.5
MLA_HEAD_GROUPS = 8
BAND_HEADS = 8
BAND_DIM = 128
BAND_PREV = 8
BAND_ROWS = BAND_PREV * CHUNK
BAND_KEYS = BAND_ROWS + CHUNK
REL_CLIP = 128
BAND_SCALE = BAND_DIM ** -0.5
GDN_HEADS = 8
GDN_GROUP = 8
GDN_DK = 128
GDN_DV = 128
CONV_W = 4
GDN_QK = GDN_HEADS * GDN_DK
GDN_V = GDN_HEADS * GDN_DV
C_CONV = 2 * GDN_QK + GDN_V
LANE = 128
SLOT = 1024
COL_U, COL_QA, COL_BQ, COL_BK, COL_BV, COL_Z, COL_TAIL = 0, 3, 4, 5, 6, 7, 8
IN_COLS_PAD = 9 * SLOT
TAIL_KR, TAIL_KSW, TAIL_A, TAIL_B = 512, 640, 768, 896

VMEM_LIMIT_BYTES = 58 * 1024 * 1024


def _params(*sem):
    return pltpu.CompilerParams(dimension_semantics=sem, vmem_limit_bytes=VMEM_LIMIT_BYTES)


def _tile(n, target, mult=16):
    best = None
    for t in range(mult, min(n, target) + 1, mult):
        if n % t == 0:
            best = t
    assert best is not None, (n, target)
    return best


def _dot(a, b, precision=None):
    return jnp.dot(a, b, preferred_element_type=F32, precision=precision)


def _dot_nt(a, b, precision=None):
    return lax.dot_general(a, b, (((1,), (1,)), ((), ())), preferred_element_type=F32,
                           precision=precision)


def _dot_tn(a, b, precision=None):
    return lax.dot_general(a, b, (((0,), (0,)), ((), ())), preferred_element_type=F32,
                           precision=precision)


def _split_bf16(x):
    hi = x.astype(BF16)
    return hi, (x - hi.astype(F32)).astype(BF16)


def _dot_3pass(a, b, expand=lambda t: t):
    a_hi, a_lo = _split_bf16(a)
    b_hi, b_lo = _split_bf16(b)
    m = a.shape[0]
    top = _dot(jnp.concatenate([a_hi, a_lo], axis=0), expand(b_hi))
    return top[:m] + top[m:] + _dot(a_hi, expand(b_lo))


def _sigmoid(x):
    return 1.0 / (1.0 + jnp.exp(-x))


def _rms(x, g):
    return x * lax.rsqrt(jnp.mean(x * x, axis=-1, keepdims=True) + EPS) * g


def _rmsnorm_kernel(x_ref, g_ref, o_ref):
    o_ref[...] = _rms(x_ref[...], g_ref[...]).astype(o_ref.dtype)


def rmsnorm(x, g, out_dtype, rows=None, row_start=0):
    m, d = x.shape
    rows = m if rows is None else rows
    tm = _tile(math.gcd(rows, row_start) if row_start else rows, 256, 8)
    off = row_start // tm
    return pl.pallas_call(
        _rmsnorm_kernel,
        grid=(rows // tm,),
        in_specs=[pl.BlockSpec((tm, d), lambda i: (i + off, 0)),
                  pl.BlockSpec((1, d), lambda i: (0, 0))],
        out_specs=pl.BlockSpec((tm, d), lambda i: (i, 0)),
        out_shape=jax.ShapeDtypeStruct((rows, d), out_dtype),
        compiler_params=_params("parallel"),
        name="rmsnorm",
    )(x, g.reshape(1, d))


def _pair_specs(first, second, tm, width, row_axis, col_fn=None):
    n_first = first.shape[0] // tm
    col = (lambda *g: 0) if col_fn is None else col_fn
    return (n_first,
            pl.BlockSpec((tm, width), lambda *g: (jnp.minimum(g[row_axis], n_first - 1), col(*g))),
            pl.BlockSpec((tm, width), lambda *g: (jnp.maximum(g[row_axis] - n_first, 0), col(*g))))


def _rmsnorm_pair_kernel(xp_ref, xs_ref, g_ref, o_ref, *, n_first):
    @pl.when(pl.program_id(0) < n_first)
    def _():
        o_ref[...] = _rms(xp_ref[...], g_ref[...]).astype(o_ref.dtype)

    @pl.when(pl.program_id(0) >= n_first)
    def _():
        o_ref[...] = _rms(xs_ref[...], g_ref[...]).astype(o_ref.dtype)


def rmsnorm_pair(xp, xs, g, out_dtype):
    d = xp.shape[1]
    tm = _tile(math.gcd(xp.shape[0], xs.shape[0]), 256, 8)
    n_first, spec_p, spec_s = _pair_specs(xp, xs, tm, d, 0)
    m = xp.shape[0] + xs.shape[0]
    return pl.pallas_call(
        functools.partial(_rmsnorm_pair_kernel, n_first=n_first),
        grid=(m // tm,),
        in_specs=[spec_p, spec_s, pl.BlockSpec((1, d), lambda i: (0, 0))],
        out_specs=pl.BlockSpec((tm, d), lambda i: (i, 0)),
        out_shape=jax.ShapeDtypeStruct((m, d), out_dtype),
        compiler_params=_params("parallel"),
        name="rmsnorm_pair",
    )(xp, xs, g.reshape(1, d))


def _cast_weight(w_ref, wb_ref):
    @pl.when(pl.program_id(1) == 0)
    def _():
        wb_ref[...] = w_ref[...].astype(BF16)


def _tiled_kernel(*refs, n_w, n_extra, n_side, is_f32, epilogue, emit_norm):
    a_ref, refs = refs[0], refs[1:]
    w_refs, refs = refs[:n_w], refs[n_w:]
    extra, refs = refs[:n_extra], refs[n_extra:]
    side_in, refs = refs[:n_side], refs[n_side:]
    o_ref, refs = refs[0], refs[1:]
    if emit_norm:
        (hb_ref, ssq_ref), refs = refs[:2], refs[2:]
    side_out, scratch = refs[:n_side], list(refs[n_side:])
    for s_in, s_out in zip(side_in, side_out):
        s_out[...] = s_in[...].astype(BF16)
    ws = []
    for w_ref, f32 in zip(w_refs, is_f32):
        if f32:
            wb_ref = scratch.pop(0)
            _cast_weight(w_ref, wb_ref)
            ws.append(wb_ref)
        else:
            ws.append(w_ref)
    out = epilogue(a_ref[...], ws, extra)
    o_ref[...] = out.astype(o_ref.dtype)
    if emit_norm:
        hb_ref[...] = (out * extra[-1][...]).astype(BF16)
        ssq_ref[...] = jnp.broadcast_to(jnp.sum(out * out, axis=-1, keepdims=True), ssq_ref.shape)


def _scale_rows(t, ssq_ref):
    r = lax.rsqrt(jnp.sum(ssq_ref[...], axis=0) / D_MODEL + EPS)
    return t * jnp.concatenate([r] * (t.shape[1] // LANE), axis=1)


def _swiglu_epilogue(a, ws, extra):
    h1 = _dot(a, ws[0][...])
    h3 = _dot(a, ws[1][...])
    return h1 * _sigmoid(h1) * h3


def _swiglu_scaled_epilogue(a, ws, extra):
    h1 = _scale_rows(_dot(a, ws[0][...]), extra[0])
    h3 = _scale_rows(_dot(a, ws[1][...]), extra[0])
    return h1 * _sigmoid(h1) * h3


def _plain_scaled_epilogue(a, ws, extra):
    return _scale_rows(_dot(a, ws[0][...]), extra[0])


def _mix_epilogue(a, ws, extra):
    yb, yc, res = extra[0][...], extra[1][...], extra[2][...]
    ka, kb = a.shape[1], yb.shape[1]
    w = ws[0]
    return res + _dot(a, w[0:ka, :]) + _dot(yb, w[ka:ka + kb, :]) + _dot(yc, w[ka + kb:, :])


def _residual_epilogue(a, ws, extra, *, scale):
    return extra[0][...] + scale * _dot(a, ws[0][...])


def _residual_pair_epilogue(a, ws, extra, *, scale, n_first):
    res = jnp.where(pl.program_id(1) < n_first, extra[0][...], extra[1][...])
    return res + scale * _dot(a, ws[0][...])


def _tiled_matmul(a, weights, extra, extra_specs, sides, epilogue, out_dtype, tm, tn, name,
                  norm_gain=None):
    m = a.shape[0]
    is_f32 = tuple(isinstance(w, tuple) for w in weights)
    k, n = weights[0][0].shape[1:] if is_f32[0] else weights[0].shape
    nj, ni = n // tn, m // tm
    extra, extra_specs = list(extra), list(extra_specs)
    norm_specs, norm_shapes = [], []
    if norm_gain is not None:
        assert n == D_MODEL
        extra.append(norm_gain.reshape(1, n))
        extra_specs.append(pl.BlockSpec((1, tn), lambda j, i: (0, j)))
        norm_specs = [pl.BlockSpec((tm, tn), lambda j, i: (i, j)),
                      pl.BlockSpec((None, tm, LANE), lambda j, i: (j, i, 0))]
        norm_shapes = [jax.ShapeDtypeStruct((m, n), BF16), jax.ShapeDtypeStruct((nj, m, LANE), F32)]
    w_args, w_specs = [], []
    for w, f32 in zip(weights, is_f32):
        if f32:
            w_args.append(w[0])
            w_specs.append(_layer_weight_spec(w[1], k, tn))
        else:
            w_args.append(w)
            w_specs.append(pl.BlockSpec((k, tn), lambda j, i: (0, j)))
    side_args, side_in_specs, side_out_specs, side_shapes = [], [], [], []
    for w, layer in sides:
        slabs = max(t for t in range(1, nj * ni + 1)
                    if w.shape[1] % t == 0 and (w.shape[1] // t) % 16 == 0)
        rows = w.shape[1] // slabs
        side_args.append(w)
        side_in_specs.append(pl.BlockSpec(
            (None, rows, w.shape[2]), functools.partial(_side_in_map, layer=layer, ni=ni, slabs=slabs)))
        side_out_specs.append(pl.BlockSpec(
            (rows, w.shape[2]), functools.partial(_side_out_map, ni=ni, slabs=slabs)))
        side_shapes.append(jax.ShapeDtypeStruct(w.shape[1:], BF16))
    outs = pl.pallas_call(
        functools.partial(_tiled_kernel, n_w=len(weights), n_extra=len(extra), n_side=len(sides),
                          is_f32=is_f32, epilogue=epilogue, emit_norm=bool(norm_specs)),
        grid=(nj, ni),
        in_specs=[pl.BlockSpec((tm, a.shape[1]), lambda j, i: (i, 0))] + w_specs + extra_specs + side_in_specs,
        out_specs=[pl.BlockSpec((tm, tn), lambda j, i: (i, j))] + norm_specs + side_out_specs,
        out_shape=[jax.ShapeDtypeStruct((m, n), out_dtype)] + norm_shapes + side_shapes,
        scratch_shapes=[pltpu.VMEM((k, tn), BF16) for f32 in is_f32 if f32],
        compiler_params=_params("arbitrary", "arbitrary"),
        name=name,
    )(a, *w_args, *extra, *side_args)
    n_norm = len(norm_specs)
    return outs[0], list(outs[1 + n_norm:]), (tuple(outs[1:1 + n_norm]) if n_norm else None)


def _side_in_map(j, i, *, layer, ni, slabs):
    return (layer, jnp.minimum(j * ni + i, slabs - 1), 0)


def _side_out_map(j, i, *, ni, slabs):
    return (jnp.minimum(j * ni + i, slabs - 1), 0)


def _row_scale_spec(row_scale, tm):
    return pl.BlockSpec((row_scale.shape[0], tm, LANE), lambda j, i: (0, i, 0))


def matmul_scaled(a, row_scale, w, out_dtype, tm_target, tn_target, sides=()):
    tm, tn = _tile(a.shape[0], tm_target), _tile(w.shape[1], tn_target, LANE)
    return _tiled_matmul(a, [w], [row_scale], [_row_scale_spec(row_scale, tm)], sides, _plain_scaled_epilogue,
                         out_dtype, tm, tn, "matmul_scaled")


def _layer_weight_spec(layer, k, tn):
    return pl.BlockSpec((None, k, tn), lambda j, i: (layer, 0, j))


def matmul_residual(a, w, res, scale, tm_target, tn_target, sides=(), norm_gain=None):
    n = res.shape[1]
    tm, tn = _tile(a.shape[0], tm_target), _tile(n, tn_target, LANE)
    return _tiled_matmul(a, [w], [res], [pl.BlockSpec((tm, tn), lambda j, i: (i, j))], sides,
                         functools.partial(_residual_epilogue, scale=scale), F32, tm, tn,
                         "matmul_residual", norm_gain)


def matmul_residual_pair(a, w, res_p, res_s, scale, tn_target, sides=(), norm_gain=None):
    tm = _tile(math.gcd(res_p.shape[0], res_s.shape[0]), 256)
    tn = _tile(res_p.shape[1], tn_target, LANE)
    n_first, spec_p, spec_s = _pair_specs(res_p, res_s, tm, tn, 1, lambda j, i: j)
    return _tiled_matmul(a, [w], [res_p, res_s], [spec_p, spec_s], sides,
                         functools.partial(_residual_pair_epilogue, scale=scale, n_first=n_first),
                         F32, tm, tn, "matmul_residual_pair", norm_gain)


def matmul_mix(ya, yb, yc, w, res, tm_target, tn_target, norm_gain=None):
    tm, tn = _tile(ya.shape[0], tm_target), _tile(w.shape[1], tn_target, LANE)
    row = lambda y: pl.BlockSpec((tm, y.shape[1]), lambda j, i: (i, 0))
    return _tiled_matmul(ya, [w], [yb, yc, res],
                         [row(yb), row(yc), pl.BlockSpec((tm, tn), lambda j, i: (i, j))], (),
                         _mix_epilogue, F32, tm, tn, "matmul_mix", norm_gain)


def matmul_swiglu(a, w1, w3, tm_target, tn_target, sides=(), row_scale=None):
    n = w1[0].shape[2] if isinstance(w1, tuple) else w1.shape[1]
    tm, tn = _tile(a.shape[0], tm_target), _tile(n, tn_target, LANE)
    if row_scale is None:
        return _tiled_matmul(a, [w1, w3], [], [], sides, _swiglu_epilogue, BF16, tm, tn,
                             "matmul_swiglu")
    return _tiled_matmul(a, [w1, w3], [row_scale], [_row_scale_spec(row_scale, tm)], sides,
                         _swiglu_scaled_epilogue, BF16, tm, tn, "matmul_swiglu")


def _mla_prep_kernel(qa_ref, tail_ref, qn_ref, kvn_ref, wq_ref, wuk_ref, cos_ref, sin_ref,
                     qlat_ref, qrope_ref, lat_ref, krope_ref, klat_bf_ref, krope_bf_ref):
    hq = _rms(qa_ref[...], qn_ref[...]).astype(BF16)
    q_all = _dot(hq, wq_ref[...])
    cos = cos_ref[...]
    sin = sin_ref[...]
    hw = MLA_HEADS * LANE
    for h in range(MLA_HEADS):
        qn = q_all[:, h * LANE:(h + 1) * LANE].astype(BF16)
        qlat_ref[h] = (_dot(qn, wuk_ref[h]) * MLA_SCALE).astype(BF16)
        qr = (q_all[:, hw + h * LANE:hw + (h + 1) * LANE] * cos
              + q_all[:, 2 * hw + h * LANE:2 * hw + (h + 1) * LANE] * sin)
        qrope_ref[h] = (qr[:, :QK_ROPE] * MLA_SCALE).astype(BF16)
    tail = tail_ref[...]
    lat = _rms(tail[:, :KV_LORA], kvn_ref[...])
    lat_ref[...] = lat
    klat_bf_ref[...] = lat.astype(BF16)
    kr = tail[:, TAIL_KR:TAIL_KR + LANE] * cos + tail[:, TAIL_KSW:TAIL_KSW + LANE] * sin
    krope_ref[...] = kr[:, :QK_ROPE]
    krope_bf_ref[...] = kr[:, :QK_ROPE].astype(BF16)


def mla_prep(cols, q_norm, kv_norm, wq_all, wuk_t, cos128, sin128):
    m = cols.shape[0]
    tm = _tile(m, 256)
    h = MLA_HEADS
    const2 = lambda i: (0, 0)
    return pl.pallas_call(
        _mla_prep_kernel,
        grid=(m // tm,),
        in_specs=[pl.BlockSpec((tm, SLOT), lambda i: (i, COL_QA)),
                  pl.BlockSpec((tm, SLOT), lambda i: (i, COL_TAIL)),
                  pl.BlockSpec((1, Q_LORA), const2),
                  pl.BlockSpec((1, KV_LORA), const2),
                  pl.BlockSpec(wq_all.shape, const2, pipeline_mode=pl.Buffered(1)),
                  pl.BlockSpec(wuk_t.shape, lambda i: (0, 0, 0), pipeline_mode=pl.Buffered(1)),
                  pl.BlockSpec((tm, LANE), lambda i: (i, 0)),
                  pl.BlockSpec((tm, LANE), lambda i: (i, 0))],
        out_specs=[pl.BlockSpec((h, tm, KV_LORA), lambda i: (0, i, 0)),
                   pl.BlockSpec((h, tm, QK_ROPE), lambda i: (0, i, 0)),
                   pl.BlockSpec((tm, KV_LORA), lambda i: (i, 0)),
                   pl.BlockSpec((tm, QK_ROPE), lambda i: (i, 0)),
                   pl.BlockSpec((tm, KV_LORA), lambda i: (i, 0)),
                   pl.BlockSpec((tm, QK_ROPE), lambda i: (i, 0))],
        out_shape=[jax.ShapeDtypeStruct((h, m, KV_LORA), BF16),
                   jax.ShapeDtypeStruct((h, m, QK_ROPE), BF16),
                   jax.ShapeDtypeStruct((m, KV_LORA), F32),
                   jax.ShapeDtypeStruct((m, QK_ROPE), F32),
                   jax.ShapeDtypeStruct((m, KV_LORA), BF16),
                   jax.ShapeDtypeStruct((m, QK_ROPE), BF16)],
        compiler_params=_params("parallel"),
        name="mla_prep",
    )(cols, cols, q_norm.reshape(1, -1), kv_norm.reshape(1, -1), wq_all, wuk_t, cos128, sin128)


def _softmax_init(m_ref, l_ref, acc_ref):
    m_ref[...] = jnp.full(m_ref.shape, NEG, F32)
    l_ref[...] = jnp.zeros(l_ref.shape, F32)
    acc_ref[...] = jnp.zeros(acc_ref.shape, F32)


def _mla_step(ql_ref, qr_ref, k_l, k_r, m_ref, l_ref, acc_ref, tq, visible=None):
    hg = MLA_HEADS // MLA_HEAD_GROUPS
    rows = hg * tq

    def scores(g):
        q_l = ql_ref[g * hg:(g + 1) * hg].reshape(rows, KV_LORA)
        q_r = qr_ref[g * hg:(g + 1) * hg].reshape(rows, QK_ROPE)
        s = _dot_nt(q_l, k_l) + _dot_nt(q_r, k_r)
        return s if visible is None else jnp.where(visible, s, NEG)

    def update(g, s):
        rs = slice(g * rows, (g + 1) * rows)
        m_old = m_ref[rs]
        m_new = jnp.maximum(m_old, jnp.max(s, axis=-1, keepdims=True))
        alpha = jnp.exp(m_old - m_new)
        p = jnp.exp(s - m_new)
        l_ref[rs] = alpha * l_ref[rs] + jnp.sum(p, axis=-1, keepdims=True)
        acc_ref[rs] = alpha * acc_ref[rs] + _dot(p.astype(BF16), k_l)
        m_ref[rs] = m_new

    s = scores(0)
    for g in range(1, MLA_HEAD_GROUPS):
        s_next = scores(g)
        update(g - 1, s)
        s = s_next
    update(MLA_HEAD_GROUPS - 1, s)


def _mla_finish(wuv_ref, o_ref, l_ref, acc_ref, tq):
    ctx = acc_ref[...] / l_ref[...]
    for h in range(MLA_HEADS):
        ch = ctx[h * tq:(h + 1) * tq].astype(BF16)
        o_ref[:, h * V_HEAD:(h + 1) * V_HEAD] = _dot(ch, wuv_ref[h]).astype(o_ref.dtype)


def _mla_prompt_kernel(qi_ref, kj_ref, ql_ref, qr_ref, kl_ref, kr_ref, wuv_ref, y_prev_ref, o_ref,
                       m_ref, l_ref, acc_ref, *, tq, tk):
    del y_prev_ref
    qi = qi_ref[pl.program_id(1)]
    kj = kj_ref[pl.program_id(1)]
    last = (qi * tq + tq - 1) // tk

    @pl.when(kj == 0)
    def _():
        _softmax_init(m_ref, l_ref, acc_ref)

    @pl.when(kj < last)
    def _():
        _mla_step(ql_ref, qr_ref, kl_ref[...], kr_ref[...], m_ref, l_ref, acc_ref, tq)

    @pl.when(kj == last)
    def _():
        shape = (MLA_HEADS // MLA_HEAD_GROUPS * tq, tk)
        row = lax.broadcasted_iota(jnp.int32, shape, 0)
        col = lax.broadcasted_iota(jnp.int32, shape, 1)
        q_chunk = (qi * tq + row % tq) // CHUNK
        k_chunk = (kj * tk + col) // CHUNK
        _mla_step(ql_ref, qr_ref, kl_ref[...], kr_ref[...], m_ref, l_ref, acc_ref, tq,
                  visible=k_chunk <= q_chunk)
        _mla_finish(wuv_ref, o_ref, l_ref, acc_ref, tq)


def mla_attention_prompt(q_lat, q_rope, k_lat, k_rope, wuv, y_prev, batch, seq):
    tq = _tile(seq, 128, CHUNK)
    tk = _tile(seq, 512, CHUNK)
    nq, nk = seq // tq, seq // tk
    h = MLA_HEADS
    rows = h * tq
    pairs = [(qi, kj) for qi in range(nq) for kj in range((qi * tq + tq - 1) // tk + 1)]
    qi_tab = jnp.asarray([p[0] for p in pairs], jnp.int32)
    kj_tab = jnp.asarray([p[1] for p in pairs], jnp.int32)
    q_map = lambda b, t, qi_ref, kj_ref: (0, b * nq + qi_ref[t], 0)
    k_map = lambda b, t, qi_ref, kj_ref: (b * nk + kj_ref[t], 0)
    return pl.pallas_call(
        functools.partial(_mla_prompt_kernel, tq=tq, tk=tk),
        grid_spec=pltpu.PrefetchScalarGridSpec(
            num_scalar_prefetch=2,
            grid=(batch, len(pairs)),
            in_specs=[pl.BlockSpec((h, tq, KV_LORA), q_map),
                      pl.BlockSpec((h, tq, QK_ROPE), q_map),
                      pl.BlockSpec((tk, KV_LORA), k_map),
                      pl.BlockSpec((tk, QK_ROPE), k_map),
                      pl.BlockSpec(wuv.shape, lambda b, t, qi_ref, kj_ref: (0, 0, 0)),
                      pl.BlockSpec(memory_space=pl.ANY)],
            out_specs=pl.BlockSpec((tq, h * V_HEAD),
                                   lambda b, t, qi_ref, kj_ref: (b * nq + qi_ref[t], 0)),
            scratch_shapes=[pltpu.VMEM((rows, 1), F32), pltpu.VMEM((rows, 1), F32),
                            pltpu.VMEM((rows, KV_LORA), F32)]),
        out_shape=jax.ShapeDtypeStruct(y_prev.shape, y_prev.dtype),
        input_output_aliases={7: 0},
        compiler_params=_params("parallel", "arbitrary"),
        name="mla_attention_prompt",
    )(qi_tab, kj_tab, q_lat, q_rope, k_lat, k_rope, wuv, y_prev)


def _mla_sample_kernel(ql_ref, qr_ref, pl_ref, pr_ref, kl_ref, kr_ref, wuv_ref, y_prev_ref, o_ref,
                       m_ref, l_ref, acc_ref):
    del y_prev_ref
    kj = pl.program_id(1)
    n_past = pl.num_programs(1) - 1

    @pl.when(kj == 0)
    def _():
        _softmax_init(m_ref, l_ref, acc_ref)

    @pl.when(kj < n_past)
    def _():
        _mla_step(ql_ref, qr_ref, pl_ref[0].astype(BF16), pr_ref[0].astype(BF16),
                  m_ref, l_ref, acc_ref, CHUNK)

    @pl.when(kj == n_past)
    def _():
        _mla_step(ql_ref, qr_ref, kl_ref[...], kr_ref[...], m_ref, l_ref, acc_ref, CHUNK)
        _mla_finish(wuv_ref, o_ref, l_ref, acc_ref, CHUNK)


def mla_attention_sample(q_lat, q_rope, past_lat, past_rope, layer, k_lat, k_rope, wuv, y_prev,
                         batch, tok_block0):
    past = past_lat.shape[2]
    tk = _tile(past, 1024, CHUNK)
    n_past = past // tk
    h = MLA_HEADS
    rows = h * CHUNK
    past_map = lambda b, kj: (layer, b, jnp.minimum(kj, n_past - 1), 0)
    return pl.pallas_call(
        _mla_sample_kernel,
        grid=(batch, n_past + 1),
        in_specs=[pl.BlockSpec((h, CHUNK, KV_LORA), lambda b, kj: (0, tok_block0 + b, 0)),
                  pl.BlockSpec((h, CHUNK, QK_ROPE), lambda b, kj: (0, tok_block0 + b, 0)),
                  pl.BlockSpec((None, 1, tk, KV_LORA), past_map),
                  pl.BlockSpec((None, 1, tk, QK_ROPE), past_map),
                  pl.BlockSpec((CHUNK, KV_LORA), lambda b, kj: (tok_block0 + b, 0)),
                  pl.BlockSpec((CHUNK, QK_ROPE), lambda b, kj: (tok_block0 + b, 0)),
                  pl.BlockSpec(wuv.shape, lambda b, kj: (0, 0, 0)),
                  pl.BlockSpec(memory_space=pl.ANY)],
        out_specs=pl.BlockSpec((CHUNK, h * V_HEAD), lambda b, kj: (tok_block0 + b, 0)),
        out_shape=jax.ShapeDtypeStruct(y_prev.shape, y_prev.dtype),
        input_output_aliases={7: 0},
        scratch_shapes=[pltpu.VMEM((rows, 1), F32), pltpu.VMEM((rows, 1), F32),
                        pltpu.VMEM((rows, KV_LORA), F32)],
        compiler_params=_params("parallel", "arbitrary"),
        name="mla_attention_sample",
    )(q_lat, q_rope, past_lat, past_rope, k_lat, k_rope, wuv, y_prev)


def _band_span(qchunks):
    tq, nk = qchunks * CHUNK, (qchunks + BAND_PREV) * CHUNK
    return -(-(tq + nk - 1) // LANE) * LANE


def _band_kernel(q_ref, k_ref, v_ref, g_ref, y_prev_ref, o_ref, bias_ref, *, pad_rows, qchunks):
    del y_prev_ref
    n = pl.program_id(1)
    tq, nk = qchunks * CHUNK, (qchunks + BAND_PREV) * CHUNK
    span = _band_span(qchunks)

    @pl.when(jnp.logical_and(pl.program_id(0) == 0, n == 0))
    def _():
        for h in range(BAND_HEADS):
            rows = jnp.broadcast_to(g_ref[h:h + 1, :], (tq, span))
            rolled = pltpu.roll(rows, span - (tq - 1), 1, stride=1, stride_axis=0)
            bias_ref[h] = rolled[:, :nk]

    start = pl.multiple_of(n * tq, CHUNK)
    row = lax.broadcasted_iota(jnp.int32, (tq, nk), 0)
    col = lax.broadcasted_iota(jnp.int32, (tq, nk), 1)
    visible = start + col >= pad_rows
    if qchunks > 1:
        ahead = (col >> CHUNK_SHIFT) - (row >> CHUNK_SHIFT)
        visible = visible & (ahead >= 0) & (ahead <= BAND_PREV)
    q = q_ref[...]
    for h in range(BAND_HEADS):
        lanes = slice(h * BAND_DIM, (h + 1) * BAND_DIM)
        qh = q[:, lanes].astype(BF16)
        kh = k_ref[0, pl.ds(start, nk), lanes]
        vh = v_ref[0, pl.ds(start, nk), lanes]
        s = _dot_nt(qh, kh) * BAND_SCALE + bias_ref[h]
        if pad_rows or qchunks > 1:
            s = jnp.where(visible, s, NEG)
        p = jnp.exp(s - jnp.max(s, axis=-1, keepdims=True))
        o = _dot(p.astype(BF16), vh) / jnp.sum(p, axis=-1, keepdims=True)
        o_ref[:, lanes] = o.astype(o_ref.dtype)


def band_attention(cols, k_rows, v_rows, rel_bias, y_prev, batch, nchunks, tok_block0, pad_rows, qchunks):
    assert nchunks % qchunks == 0 and tok_block0 % qchunks == 0
    width = BAND_HEADS * BAND_DIM
    rows = k_rows.shape[1]
    tq, nk = qchunks * CHUNK, (qchunks + BAND_PREV) * CHUNK
    nblk = nchunks // qchunks
    bias_row = _band_bias_row(rel_bias, qchunks)
    tok = lambda b, n: tok_block0 // qchunks + b * nblk + n
    return pl.pallas_call(
        functools.partial(_band_kernel, pad_rows=pad_rows, qchunks=qchunks),
        grid=(batch, nblk),
        in_specs=[pl.BlockSpec((tq, width), lambda b, n: (tok(b, n), COL_BQ)),
                  pl.BlockSpec((1, rows, width), lambda b, n: (b, 0, 0)),
                  pl.BlockSpec((1, rows, width), lambda b, n: (b, 0, 0)),
                  pl.BlockSpec(bias_row.shape, lambda b, n: (0, 0)),
                  pl.BlockSpec(memory_space=pl.ANY)],
        out_specs=pl.BlockSpec((tq, width), lambda b, n: (tok(b, n), 0)),
        out_shape=jax.ShapeDtypeStruct(y_prev.shape, y_prev.dtype),
        input_output_aliases={4: 0},
        scratch_shapes=[pltpu.VMEM((BAND_HEADS, tq, nk), F32)],
        compiler_params=_params("arbitrary", "arbitrary"),
        name="band_attention",
    )(cols, k_rows, v_rows, bias_row, y_prev)


def _expand_matrix(rows, first, width, per_shift):
    r = lax.broadcasted_iota(jnp.int32, (rows, width), 0)
    c = lax.broadcasted_iota(jnp.int32, (rows, width), 1)
    return (r == first + (c >> per_shift)).astype(F32)


def _unit_lower_inverse(a, row, col, group):
    gl = a.shape[1]
    r = lax.broadcasted_iota(jnp.int32, (gl, gl), 0)
    c = lax.broadcasted_iota(jnp.int32, (gl, gl), 1)
    diag_blocks = ((r >> CHUNK_SHIFT) == (c >> CHUNK_SHIFT)).astype(BF16)

    def block_diag(w):
        return jnp.concatenate([w] * group, axis=0) * diag_blocks

    t = (row == col).astype(F32) - jnp.where((row >> 1) == (col >> 1), a, 0.0)
    for shift in range(1, CHUNK_SHIFT):
        lower_left = ((row >> (shift + 1)) == (col >> (shift + 1))) & ((row >> shift) != (col >> shift))
        x = _dot_3pass(t, jnp.where(lower_left, a, 0.0), block_diag)
        t = t - _dot_3pass(x, t, block_diag)
    return t


def _l2norm_heads(x, heads, width):
    parts = []
    for h in range(heads):
        xh = x[:, h * width:(h + 1) * width]
        parts.append(xh * lax.rsqrt(jnp.sum(xh * xh, axis=-1, keepdims=True) + EPS))
    return jnp.concatenate(parts, axis=1)


def _gdn_pre_kernel(u_ref, uprev_ref, init_ref, tail_ref, cw_ref, aneg_ref, dtb_ref,
                    uv_ref, wk_ref, attn_ref, qg_ref, kt_ref, gl_ref, ext_ref,
                    *, chunks_per_seq, n_prompt_chunks):
    group = GDN_GROUP
    gw = group * CHUNK
    c = pl.program_id(0)
    is_start = jnp.logical_or(c >= n_prompt_chunks, c % chunks_per_seq == 0)
    ext_ref[0:8, :] = jnp.where(is_start, init_ref[0], uprev_ref[...])
    ext_ref[8:8 + CHUNK, :] = u_ref[...]
    cw = cw_ref[...]
    uc = (cw[3:4] * ext_ref[8:8 + CHUNK, :] + cw[2:3] * ext_ref[7:7 + CHUNK, :]
          + cw[1:2] * ext_ref[6:6 + CHUNK, :] + cw[0:1] * ext_ref[5:5 + CHUNK, :])
    uc = uc * _sigmoid(uc)

    tail = tail_ref[...]
    xa = tail[:, TAIL_A:TAIL_A + LANE] + dtb_ref[...]
    softplus = jnp.maximum(xa, 0.0) + jnp.log(1.0 + jnp.exp(-jnp.abs(xa)))
    g = aneg_ref[...] * softplus
    beta = _sigmoid(tail[:, TAIL_B:TAIL_B + LANE])

    r64 = lax.broadcasted_iota(jnp.int32, (CHUNK, CHUNK), 0)
    c64 = lax.broadcasted_iota(jnp.int32, (CHUNK, CHUNK), 1)
    gc = _dot((r64 >= c64).astype(F32), g, HIGHEST)
    egc = jnp.exp(gc)
    ekt = jnp.exp(gc[CHUNK - 1:CHUNK, :] - gc)
    e128 = _expand_matrix(LANE, 0, GDN_HEADS * GDN_DK, 7)
    beta_x = _dot(beta, e128, HIGHEST)
    egc_x = _dot(egc, e128, HIGHEST)
    ekt_x = _dot(ekt, e128, HIGHEST)
    gl_ref[0] = egc_x[CHUNK - 1:CHUNK, :]

    qn = _l2norm_heads(uc[:, :GDN_QK], GDN_HEADS, GDN_DK) * (GDN_DK ** -0.5)
    kn = _l2norm_heads(uc[:, GDN_QK:2 * GDN_QK], GDN_HEADS, GDN_DK)
    v_all = uc[:, 2 * GDN_QK:]
    qg_ref[...] = (qn * egc_x).astype(BF16)
    kt_ref[...] = (kn * ekt_x).astype(BF16)
    rhs_v = beta_x * v_all
    rhs_k = beta_x * egc_x * kn

    row = lax.broadcasted_iota(jnp.int32, (CHUNK, gw), 0)
    col = lax.broadcasted_iota(jnp.int32, (CHUNK, gw), 1) & (CHUNK - 1)
    incl = row >= col
    head_r = lax.broadcasted_iota(jnp.int32, (gw, group * GDN_DK), 0) >> CHUNK_SHIFT
    head_c = lax.broadcasted_iota(jnp.int32, (gw, group * GDN_DK), 1) >> 7
    for grp in range(GDN_HEADS // group):
        lanes = slice(grp * group * GDN_DK, (grp + 1) * group * GDN_DK)
        e64 = _expand_matrix(LANE, grp * group, gw, CHUNK_SHIFT)
        gcw = _dot(gc, e64, HIGHEST)
        bw = _dot(beta, e64, HIGHEST)
        gc_t = jnp.sum(jnp.where(row == col, gcw, 0.0), axis=0, keepdims=True)
        decay = jnp.where(incl, jnp.exp(jnp.where(incl, gcw - gc_t, 0.0)), 0.0)
        kg = kn[:, lanes]
        k_rows = jnp.where(head_r == head_c, jnp.concatenate([kg] * group, axis=0), 0.0).astype(BF16)
        kk = _dot_nt(kg.astype(BF16), k_rows)
        qk = _dot_nt(qn[:, lanes].astype(BF16), k_rows)
        a_mat = jnp.where(row > col, bw * kk * decay, 0.0)
        t_inv = _unit_lower_inverse(a_mat, row, col, group)
        attn_ref[:, grp * gw:(grp + 1) * gw] = (qk * decay).astype(BF16)
        for hh in range(group):
            h = grp * group + hh
            hl = slice(h * GDN_DK, (h + 1) * GDN_DK)
            rhs = jnp.concatenate([rhs_v[:, hl], rhs_k[:, hl]], axis=1)
            sol = _dot_3pass(t_inv[:, hh * CHUNK:(hh + 1) * CHUNK], rhs)
            uv_ref[:, hl] = sol[:, :GDN_DV]
            wk_ref[:, hl] = sol[:, GDN_DV:].astype(BF16)


def gdn_pre(cols, conv_init, conv_w, aneg, dtb, chunks_per_seq, n_prompt_chunks):
    m = cols.shape[0]
    nch = m // CHUNK
    h = GDN_HEADS

    def init_map(c):
        return (jnp.where(c < n_prompt_chunks, c // chunks_per_seq,
                          n_prompt_chunks // chunks_per_seq + c - n_prompt_chunks), 0, 0)

    tok_spec = lambda d: pl.BlockSpec((CHUNK, d), lambda c: (c, 0))
    return pl.pallas_call(
        functools.partial(_gdn_pre_kernel, chunks_per_seq=chunks_per_seq,
                          n_prompt_chunks=n_prompt_chunks),
        grid=(nch,),
        in_specs=[pl.BlockSpec((CHUNK, C_CONV), lambda c: (c, COL_U)),
                  pl.BlockSpec((8, C_CONV), lambda c: (jnp.maximum(c * (CHUNK // 8) - 1, 0), COL_U)),
                  pl.BlockSpec((1, 8, C_CONV), init_map),
                  pl.BlockSpec((CHUNK, SLOT), lambda c: (c, COL_TAIL)),
                  pl.BlockSpec((CONV_W, C_CONV), lambda c: (0, 0)),
                  pl.BlockSpec((1, LANE), lambda c: (0, 0)),
                  pl.BlockSpec((1, LANE), lambda c: (0, 0))],
        out_specs=[tok_spec(GDN_V), tok_spec(GDN_QK), tok_spec(h * CHUNK), tok_spec(GDN_QK),
                   tok_spec(GDN_QK), pl.BlockSpec((1, 1, GDN_QK), lambda c: (c, 0, 0))],
        out_shape=[jax.ShapeDtypeStruct((m, GDN_V), F32),
                   jax.ShapeDtypeStruct((m, GDN_QK), BF16),
                   jax.ShapeDtypeStruct((m, h * CHUNK), BF16),
                   jax.ShapeDtypeStruct((m, GDN_QK), BF16),
                   jax.ShapeDtypeStruct((m, GDN_QK), BF16),
                   jax.ShapeDtypeStruct((nch, 1, GDN_QK), F32)],
        scratch_shapes=[pltpu.VMEM((8 + CHUNK, C_CONV), F32)],
        compiler_params=_params("parallel"),
        name="gdn_pre",
    )(cols, cols, conv_init, cols, conv_w, aneg, dtb)


def _gdn_rec_kernel(uv_ref, wk_ref, attn_ref, qg_ref, kt_ref, gl_ref, s0_ref, z_ref, gn_ref,
                    y_prev_ref, y_ref, s_ref):
    del y_prev_ref

    @pl.when(pl.program_id(1) == 0)
    def _():
        s_ref[...] = s0_ref[...]

    z = z_ref[...]
    gn = gn_ref[...]
    heads = range(GDN_HEADS)
    lanes = [slice(h * GDN_DK, (h + 1) * GDN_DK) for h in heads]
    s_old = [s_ref[0, h] for h in heads]
    s_bf = [s.astype(BF16) for s in s_old]
    ws = [_dot(wk_ref[:, lanes[h]], s_bf[h]) for h in heads]
    qs = [_dot(qg_ref[:, lanes[h]], s_bf[h]) for h in heads]
    u_bf = [(uv_ref[:, lanes[h]] - ws[h]).astype(BF16) for h in heads]
    outs = [qs[h] + _dot(attn_ref[:, h * CHUNK:(h + 1) * CHUNK], u_bf[h]) for h in heads]
    for h in heads:
        s_ref[0, h] = gl_ref[0, :, lanes[h]] * s_old[h] + _dot_tn(kt_ref[:, lanes[h]], u_bf[h])
    for h in heads:
        zh = z[:, lanes[h]]
        y_ref[:, lanes[h]] = (_rms(outs[h], gn) * (zh * _sigmoid(zh))).astype(y_ref.dtype)


def gdn_recurrence(pre, cols, s0, gdn_norm, y_prev, batch, nchunks, tok_block0):
    uv, wk, attn, qg, kt, gl = pre
    h = GDN_HEADS
    tok = lambda b, n: tok_block0 + b * nchunks + n
    tok_spec = lambda d: pl.BlockSpec((CHUNK, d), lambda b, n: (tok(b, n), 0))
    state_spec = pl.BlockSpec((1, h, GDN_DK, GDN_DV), lambda b, n: (b, 0, 0, 0))
    return pl.pallas_call(
        _gdn_rec_kernel,
        grid=(batch, nchunks),
        in_specs=[tok_spec(GDN_V), tok_spec(GDN_QK), tok_spec(h * CHUNK), tok_spec(GDN_QK),
                  tok_spec(GDN_QK),
                  pl.BlockSpec((1, 1, GDN_QK), lambda b, n: (tok(b, n), 0, 0)),
                  state_spec,
                  pl.BlockSpec((CHUNK, SLOT), lambda b, n: (tok(b, n), COL_Z)),
                  pl.BlockSpec((1, GDN_DV), lambda b, n: (0, 0)),
                  pl.BlockSpec(memory_space=pl.ANY)],
        out_specs=[tok_spec(GDN_V), state_spec],
        out_shape=[jax.ShapeDtypeStruct(y_prev.shape, y_prev.dtype),
                   jax.ShapeDtypeStruct((batch, h, GDN_DK, GDN_DV), F32)],
        input_output_aliases={9: 0},
        compiler_params=_params("parallel", "arbitrary"),
        name="gdn_recurrence",
    )(uv, wk, attn, qg, kt, gl, s0, cols, gdn_norm.reshape(1, -1), y_prev)


def _pad_cols(w, width):
    return jnp.pad(w, ((0, 0), (0, width - w.shape[1])))


def _layout_w_in(w):
    a0, b0 = 0, Q_LORA + KV_LORA + QK_ROPE
    c0 = b0 + 3 * BAND_HEADS * BAND_DIM
    q_a = w[:, a0:a0 + Q_LORA]
    c_kv = w[:, a0 + Q_LORA:a0 + Q_LORA + KV_LORA]
    k_r = w[:, a0 + Q_LORA + KV_LORA:b0]
    band = w[:, b0:c0]
    u = w[:, c0:c0 + C_CONV]
    z = w[:, c0 + C_CONV:c0 + C_CONV + GDN_V]
    a = w[:, c0 + C_CONV + GDN_V:c0 + C_CONV + GDN_V + GDN_HEADS]
    b = w[:, c0 + C_CONV + GDN_V + GDN_HEADS:]
    half = QK_ROPE // 2
    k_sw = jnp.concatenate([k_r[:, half:], k_r[:, :half]], axis=1)
    tail = jnp.concatenate([c_kv, _pad_cols(k_r, LANE), _pad_cols(k_sw, LANE),
                            _pad_cols(a, LANE), _pad_cols(b, LANE)], axis=1)
    return jnp.concatenate([u, q_a, band, z, tail], axis=1).astype(BF16)


def _layout_w_qb(w):
    k = w.shape[0]
    w = w.reshape(k, MLA_HEADS, QK_NOPE + QK_ROPE)
    nope = w[:, :, :QK_NOPE]
    rope = w[:, :, QK_NOPE:]
    half = QK_ROPE // 2
    rope_sw = jnp.concatenate([rope[:, :, half:], rope[:, :, :half]], axis=2)
    pad = lambda t: jnp.pad(t, ((0, 0), (0, 0), (0, LANE - QK_ROPE)))
    out = jnp.concatenate([nope, pad(rope), pad(rope_sw)], axis=1)
    return out.reshape(k, 3 * MLA_HEADS * LANE).astype(BF16)


def _rope_tables(pos):
    inv = 1.0 / (ROPE_THETA ** (jnp.arange(0, QK_ROPE, 2, dtype=F32) / QK_ROPE))
    ang = pos.astype(F32)[:, None] * inv[None, :]
    cos, sin = jnp.cos(ang), jnp.sin(ang)
    zeros = jnp.zeros((pos.shape[0], LANE - QK_ROPE), F32)
    return (jnp.concatenate([cos, cos, zeros], axis=1), jnp.concatenate([-sin, sin, zeros], axis=1))


def _band_bias_row(rel_bias, qchunks):
    heads = rel_bias.shape[0]
    span = _band_span(qchunks)
    left = BAND_ROWS + qchunks * CHUNK - 1 - REL_CLIP
    right = max(span - left - (2 * REL_CLIP + 1), 0)
    row = jnp.concatenate([jnp.broadcast_to(rel_bias[:, -1:], (heads, left)), rel_bias[:, ::-1],
                           jnp.broadcast_to(rel_bias[:, :1], (heads, right))], axis=1)
    return row[:, :span]


def kernel(x_prompt, x_sample, cache_latent, cache_k_rope, cache_band_k, cache_band_v, state_conv, state_delta, norm_ff1, ff1_w1, ff1_w3, ff1_w2, norm_mix, w_in, q_norm, w_qb, kv_norm, w_uk, w_uv, rel_bias, conv_w, a_log, dt_bias, gdn_norm, w_out, norm_ff2, ff2_w1, ff2_w3, ff2_w2, final_norm):
    batch, seq, d = x_prompt.shape
    dbatch, dseq, _ = x_sample.shape
    depth = w_in.shape[0]
    past = cache_latent.shape[2]
    assert d == D_MODEL and dseq == CHUNK and seq % CHUNK == 0 and cache_band_k.shape[2] == BAND_ROWS
    mp, ms = batch * seq, dbatch * dseq
    m = mp + ms
    nchunks = seq // CHUNK
    blk_s = mp // CHUNK
    band_w = BAND_HEADS * BAND_DIM

    xp, xs = x_prompt.reshape(mp, d), x_sample.reshape(ms, d)
    pos = jnp.concatenate([jnp.tile(jnp.arange(seq, dtype=jnp.int32), batch),
                           jnp.tile(past + jnp.arange(dseq, dtype=jnp.int32), dbatch)])
    cos128, sin128 = _rope_tables(pos)
    zero_state = jnp.zeros((batch, GDN_HEADS, GDN_DK, GDN_DV), F32)

    def ffn(x, normed, w13_bf, w2, l, next_w13, next_gain):
        hid, (w2_bf,), _ = matmul_swiglu(normed[0], w13_bf[0], w13_bf[1], 544, 1024, sides=[(w2, l)],
                                         row_scale=normed[1])
        return matmul_residual(hid, w2_bf, x, 0.5, 272, 1024, sides=next_w13, norm_gain=next_gain)

    def mixer_rows(width):
        return jnp.zeros((m, width), BF16)

    outs = [[] for _ in range(12)]
    for l in range(depth):
        if l == 0:
            hid, (w2_bf,), _ = matmul_swiglu(rmsnorm_pair(xp, xs, norm_ff1[0], BF16), (ff1_w1, 0),
                                             (ff1_w3, 0), 544, 512, sides=[(ff1_w2, 0)])
            x, w13_bf, normed = matmul_residual_pair(hid, w2_bf, xp, xs, 0.5, 1024,
                                                     sides=[(ff2_w1, 0), (ff2_w3, 0)],
                                                     norm_gain=norm_mix[0])
        else:
            x, w13_bf, normed = ffn(x, normed, w13_bf, ff1_w2, l, [(ff2_w1, l), (ff2_w3, l)],
                                    norm_mix[l])

        cols, (w_out_bf,), _ = matmul_scaled(normed[0], normed[1], _layout_w_in(w_in[l]), F32,
                                             544, 1024, sides=[(w_out, l)])

        wuk_t = jnp.transpose(w_uk[l], (1, 2, 0)).astype(BF16)
        wuv_t = jnp.transpose(w_uv[l], (1, 0, 2)).astype(BF16)
        q_lat, q_rope, lat, krope, klat_bf, krope_bf = mla_prep(
            cols, q_norm[l], kv_norm[l], _layout_w_qb(w_qb[l]), wuk_t, cos128, sin128)
        ya = mla_attention_prompt(q_lat, q_rope, klat_bf, krope_bf, wuv_t,
                                  mixer_rows(MLA_HEADS * V_HEAD), batch, seq)
        ya = mla_attention_sample(q_lat, q_rope, cache_latent, cache_k_rope, l, klat_bf, krope_bf,
                                  wuv_t, ya, dbatch, blk_s)

        band_k = cols[:, COL_BK * SLOT:(COL_BK + 1) * SLOT]
        band_v = cols[:, COL_BV * SLOT:(COL_BV + 1) * SLOT]

        def prompt_rows(t):
            t = t[:mp].reshape(batch, seq, band_w).astype(BF16)
            return jnp.pad(t, ((0, 0), (BAND_ROWS, 0), (0, 0)))

        def sample_rows(t, cache):
            return jnp.concatenate([cache.reshape(dbatch, BAND_ROWS, band_w).astype(BF16),
                                    t[mp:].reshape(dbatch, dseq, band_w).astype(BF16)], axis=1)

        yb = band_attention(cols, prompt_rows(band_k), prompt_rows(band_v), rel_bias[l],
                            mixer_rows(band_w), batch, nchunks, 0, BAND_ROWS, math.gcd(nchunks, 4))
        yb = band_attention(cols, sample_rows(band_k, cache_band_k[l]),
                            sample_rows(band_v, cache_band_v[l]), rel_bias[l], yb, dbatch, 1, blk_s, 0, 1)

        conv_init = jnp.concatenate(
            [jnp.zeros((batch, 8, C_CONV), F32),
             jnp.pad(state_conv[l], ((0, 0), (8 - (CONV_W - 1), 0), (0, 0)))], axis=0)
        aneg = _pad_cols(-jnp.exp(a_log[l].astype(F32)).reshape(1, -1), LANE)
        dtb = _pad_cols(dt_bias[l].astype(F32).reshape(1, -1), LANE)
        pre = gdn_pre(cols, conv_init, conv_w[l], aneg, dtb, nchunks, mp // CHUNK)
        yc, delta_p = gdn_recurrence(pre, cols, zero_state, gdn_norm[l], mixer_rows(GDN_V),
                                     batch, nchunks, 0)
        yc, delta_s = gdn_recurrence(pre, cols, state_delta[l], gdn_norm[l], yc, dbatch, 1, blk_s)

        x, _, normed = matmul_mix(ya, yb, yc, w_out_bf, x, 544, 1024, norm_gain=norm_ff2[l])
        last = l + 1 == depth
        x, w13_bf, normed = ffn(x, normed, w13_bf, ff2_w2, l,
                                [] if last else [(ff1_w1, l + 1), (ff1_w3, l + 1)],
                                None if last else norm_ff1[l + 1])

        def conv_tail(first_row, rows_per_stream, streams):
            ends = [first_row + (s + 1) * rows_per_stream for s in range(streams)]
            return jnp.stack([cols[e - (CONV_W - 1):e, :C_CONV] for e in ends])

        keep = min(BAND_ROWS, seq)
        layer_out = (
            lat[:mp].reshape(batch, seq, KV_LORA),
            krope[:mp].reshape(batch, seq, QK_ROPE),
            band_k[:mp].reshape(batch, seq, BAND_HEADS, BAND_DIM)[:, seq - keep:],
            band_v[:mp].reshape(batch, seq, BAND_HEADS, BAND_DIM)[:, seq - keep:],
            conv_tail(0, seq, batch),
            delta_p,
            lat[mp:].reshape(dbatch, dseq, KV_LORA),
            krope[mp:].reshape(dbatch, dseq, QK_ROPE),
            band_k[mp:].reshape(dbatch, dseq, BAND_HEADS, BAND_DIM),
            band_v[mp:].reshape(dbatch, dseq, BAND_HEADS, BAND_DIM),
            conv_tail(mp, dseq, dbatch),
            delta_s,
        )
        for acc, t in zip(outs, layer_out):
            acc.append(t)

    y_prompt = rmsnorm(x, final_norm, F32, rows=mp).reshape(batch, seq, d)
    y_sample = rmsnorm(x, final_norm, F32, rows=ms, row_start=mp).reshape(dbatch, dseq, d)
    return (y_prompt, y_sample) + tuple(jnp.stack(t) for t in outs)
```

```python
import functools
import math

import jax
import jax.numpy as jnp
from jax import lax
from jax.experimental import pallas as pl
from jax.experimental.pallas import tpu as pltpu

F32 = jnp.float32
BF16 = jnp.bfloat16
HIGHEST = lax.Precision.HIGHEST

D_MODEL = 4096
CHUNK = 64
CHUNK_SHIFT = 6
EPS = 1e-6
NEG = -1e30
Q_LORA = 1024
KV_LORA = 512
QK_NOPE = 128
QK_ROPE = 64
V_HEAD = 128
MLA_HEADS = 16
ROPE_THETA = 10000.0
MLA_SCALE = (QK_NOPE + QK_ROPE) ** -0.5
MLA_HEAD_GROUPS = 8
BAND_HEADS = 8
BAND_DIM = 128
BAND_PREV = 8
BAND_ROWS = BAND_PREV * CHUNK
BAND_KEYS = BAND_ROWS + CHUNK
REL_CLIP = 128
BAND_SCALE = BAND_DIM ** -0.5
GDN_HEADS = 8
GDN_GROUP = 8
GDN_DK = 128
GDN_DV = 128
CONV_W = 4
GDN_QK = GDN_HEADS * GDN_DK
GDN_V = GDN_HEADS * GDN_DV
C_CONV = 2 * GDN_QK + GDN_V
LANE = 128
SLOT = 1024
COL_U, COL_QA, COL_BQ, COL_BK, COL_BV, COL_Z, COL_TAIL = 0, 3, 4, 5, 6, 7, 8
IN_COLS_PAD = 9 * SLOT
TAIL_KR, TAIL_KSW, TAIL_A, TAIL_B = 512, 640, 768, 896

VMEM_LIMIT_BYTES = 58 * 1024 * 1024


def _params(*sem):
    return pltpu.CompilerParams(dimension_semantics=sem, vmem_limit_bytes=VMEM_LIMIT_BYTES)


def _tile(n, target, mult=16):
    best = None
    for t in range(mult, min(n, target) + 1, mult):
        if n % t == 0:
            best = t
    assert best is not None, (n, target)
    return best


def _dot(a, b, precision=None):
    return jnp.dot(a, b, preferred_element_type=F32, precision=precision)


def _dot_nt(a, b, precision=None):
    return lax.dot_general(a, b, (((1,), (1,)), ((), ())), preferred_element_type=F32,
                           precision=precision)


def _dot_tn(a, b, precision=None):
    return lax.dot_general(a, b, (((0,), (0,)), ((), ())), preferred_element_type=F32,
                           precision=precision)


def _split_bf16(x):
    hi = x.astype(BF16)
    return hi, (x - hi.astype(F32)).astype(BF16)


def _dot_3pass(a, b, expand=lambda t: t):
    a_hi, a_lo = _split_bf16(a)
    b_hi, b_lo = _split_bf16(b)
    m = a.shape[0]
    top = _dot(jnp.concatenate([a_hi, a_lo], axis=0), expand(b_hi))
    return top[:m] + top[m:] + _dot(a_hi, expand(b_lo))


def _sigmoid(x):
    return 1.0 / (1.0 + jnp.exp(-x))


def _rms(x, g):
    return x * lax.rsqrt(jnp.mean(x * x, axis=-1, keepdims=True) + EPS) * g


def _rmsnorm_kernel(x_ref, g_ref, o_ref):
    o_ref[...] = _rms(x_ref[...], g_ref[...]).astype(o_ref.dtype)


def rmsnorm(x, g, out_dtype, rows=None, row_start=0):
    m, d = x.shape
    rows = m if rows is None else rows
    tm = _tile(math.gcd(rows, row_start) if row_start else rows, 256, 8)
    off = row_start // tm
    return pl.pallas_call(
        _rmsnorm_kernel,
        grid=(rows // tm,),
        in_specs=[pl.BlockSpec((tm, d), lambda i: (i + off, 0)),
                  pl.BlockSpec((1, d), lambda i: (0, 0))],
        out_specs=pl.BlockSpec((tm, d), lambda i: (i, 0)),
        out_shape=jax.ShapeDtypeStruct((rows, d), out_dtype),
        compiler_params=_params("parallel"),
        name="rmsnorm",
    )(x, g.reshape(1, d))


def _pair_specs(first, second, tm, width, row_axis, col_fn=None):
    n_first = first.shape[0] // tm
    col = (lambda *g: 0) if col_fn is None else col_fn
    return (n_first,
            pl.BlockSpec((tm, width), lambda *g: (jnp.minimum(g[row_axis], n_first - 1), col(*g))),
            pl.BlockSpec((tm, width), lambda *g: (jnp.maximum(g[row_axis] - n_first, 0), col(*g))))


def _rmsnorm_pair_kernel(xp_ref, xs_ref, g_ref, o_ref, *, n_first):
    @pl.when(pl.program_id(0) < n_first)
    def _():
        o_ref[...] = _rms(xp_ref[...], g_ref[...]).astype(o_ref.dtype)

    @pl.when(pl.program_id(0) >= n_first)
    def _():
        o_ref[...] = _rms(xs_ref[...], g_ref[...]).astype(o_ref.dtype)


def rmsnorm_pair(xp, xs, g, out_dtype):
    d = xp.shape[1]
    tm = _tile(math.gcd(xp.shape[0], xs.shape[0]), 512, 8)
    n_first, spec_p, spec_s = _pair_specs(xp, xs, tm, d, 0)
    m = xp.shape[0] + xs.shape[0]
    return pl.pallas_call(
        functools.partial(_rmsnorm_pair_kernel, n_first=n_first),
        grid=(m // tm,),
        in_specs=[spec_p, spec_s, pl.BlockSpec((1, d), lambda i: (0, 0))],
        out_specs=pl.BlockSpec((tm, d), lambda i: (i, 0)),
        out_shape=jax.ShapeDtypeStruct((m, d), out_dtype),
        compiler_params=_params("parallel"),
        name="rmsnorm_pair",
    )(xp, xs, g.reshape(1, d))


def _cast_weight(w_ref, wb_ref):
    @pl.when(pl.program_id(1) == 0)
    def _():
        wb_ref[...] = w_ref[...].astype(BF16)


def _tiled_kernel(*refs, n_w, n_extra, n_side, is_f32, epilogue, emit_norm):
    a_ref, refs = refs[0], refs[1:]
    w_refs, refs = refs[:n_w], refs[n_w:]
    extra, refs = refs[:n_extra], refs[n_extra:]
    side_in, refs = refs[:n_side], refs[n_side:]
    o_ref, refs = refs[0], refs[1:]
    if emit_norm:
        (hb_ref, ssq_ref), refs = refs[:2], refs[2:]
    side_out, scratch = refs[:n_side], list(refs[n_side:])
    for s_in, s_out in zip(side_in, side_out):
        s_out[...] = s_in[...].astype(BF16)
    ws = []
    for w_ref, f32 in zip(w_refs, is_f32):
        if f32:
            wb_ref = scratch.pop(0)
            _cast_weight(w_ref, wb_ref)
            ws.append(wb_ref)
        else:
            ws.append(w_ref)
    out = epilogue(a_ref[...], ws, extra)
    o_ref[...] = out.astype(o_ref.dtype)
    if emit_norm:
        hb_ref[...] = (out * extra[-1][...]).astype(BF16)
        ssq_ref[...] = jnp.broadcast_to(jnp.sum(out * out, axis=-1, keepdims=True), ssq_ref.shape)


def _scale_rows(t, ssq_ref):
    r = lax.rsqrt(jnp.sum(ssq_ref[...], axis=0) / D_MODEL + EPS)
    return t * jnp.concatenate([r] * (t.shape[1] // LANE), axis=1)


def _swiglu_epilogue(a, ws, extra):
    h1 = _dot(a, ws[0][...])
    h3 = _dot(a, ws[1][...])
    return h1 * _sigmoid(h1) * h3


def _swiglu_scaled_epilogue(a, ws, extra):
    h1 = _scale_rows(_dot(a, ws[0][...]), extra[0])
    h3 = _scale_rows(_dot(a, ws[1][...]), extra[0])
    return h1 * _sigmoid(h1) * h3


def _plain_scaled_epilogue(a, ws, extra):
    return _scale_rows(_dot(a, ws[0][...]), extra[0])


def _mix_epilogue(a, ws, extra):
    yb, yc, res = extra[0][...], extra[1][...], extra[2][...]
    ka, kb = a.shape[1], yb.shape[1]
    w = ws[0]
    return res + _dot(a, w[0:ka, :]) + _dot(yb, w[ka:ka + kb, :]) + _dot(yc, w[ka + kb:, :])


def _residual_epilogue(a, ws, extra, *, scale):
    return extra[0][...] + scale * _dot(a, ws[0][...])


def _residual_pair_epilogue(a, ws, extra, *, scale, n_first):
    res = jnp.where(pl.program_id(1) < n_first, extra[0][...], extra[1][...])
    return res + scale * _dot(a, ws[0][...])


def _tiled_matmul(a, weights, extra, extra_specs, sides, epilogue, out_dtype, tm, tn, name,
                  norm_gain=None):
    m = a.shape[0]
    is_f32 = tuple(isinstance(w, tuple) for w in weights)
    k, n = weights[0][0].shape[1:] if is_f32[0] else weights[0].shape
    nj, ni = n // tn, m // tm
    extra, extra_specs = list(extra), list(extra_specs)
    norm_specs, norm_shapes = [], []
    if norm_gain is not None:
        assert n == D_MODEL
        extra.append(norm_gain.reshape(1, n))
        extra_specs.append(pl.BlockSpec((1, tn), lambda j, i: (0, j)))
        norm_specs = [pl.BlockSpec((tm, tn), lambda j, i: (i, j)),
                      pl.BlockSpec((None, tm, LANE), lambda j, i: (j, i, 0))]
        norm_shapes = [jax.ShapeDtypeStruct((m, n), BF16), jax.ShapeDtypeStruct((nj, m, LANE), F32)]
    w_args, w_specs = [], []
    for w, f32 in zip(weights, is_f32):
        if f32:
            w_args.append(w[0])
            w_specs.append(_layer_weight_spec(w[1], k, tn))
        else:
            w_args.append(w)
            w_specs.append(pl.BlockSpec((k, tn), lambda j, i: (0, j)))
    side_args, side_in_specs, side_out_specs, side_shapes = [], [], [], []
    for w, layer in sides:
        slabs = max(t for t in range(1, nj * ni + 1)
                    if w.shape[1] % t == 0 and (w.shape[1] // t) % 16 == 0)
        rows = w.shape[1] // slabs
        side_args.append(w)
        side_in_specs.append(pl.BlockSpec(
            (None, rows, w.shape[2]), functools.partial(_side_in_map, layer=layer, ni=ni, slabs=slabs)))
        side_out_specs.append(pl.BlockSpec(
            (rows, w.shape[2]), functools.partial(_side_out_map, ni=ni, slabs=slabs)))
        side_shapes.append(jax.ShapeDtypeStruct(w.shape[1:], BF16))
    outs = pl.pallas_call(
        functools.partial(_tiled_kernel, n_w=len(weights), n_extra=len(extra), n_side=len(sides),
                          is_f32=is_f32, epilogue=epilogue, emit_norm=bool(norm_specs)),
        grid=(nj, ni),
        in_specs=[pl.BlockSpec((tm, a.shape[1]), lambda j, i: (i, 0))] + w_specs + extra_specs + side_in_specs,
        out_specs=[pl.BlockSpec((tm, tn), lambda j, i: (i, j))] + norm_specs + side_out_specs,
        out_shape=[jax.ShapeDtypeStruct((m, n), out_dtype)] + norm_shapes + side_shapes,
        scratch_shapes=[pltpu.VMEM((k, tn), BF16) for f32 in is_f32 if f32],
        compiler_params=_params("arbitrary", "arbitrary"),
        name=name,
    )(a, *w_args, *extra, *side_args)
    n_norm = len(norm_specs)
    return outs[0], list(outs[1 + n_norm:]), (tuple(outs[1:1 + n_norm]) if n_norm else None)


def _side_in_map(j, i, *, layer, ni, slabs):
    return (layer, jnp.minimum(j * ni + i, slabs - 1), 0)


def _side_out_map(j, i, *, ni, slabs):
    return (jnp.minimum(j * ni + i, slabs - 1), 0)


def _row_scale_spec(row_scale, tm):
    return pl.BlockSpec((row_scale.shape[0], tm, LANE), lambda j, i: (0, i, 0))


def matmul_scaled(a, row_scale, w, out_dtype, tm_target, tn_target, sides=()):
    tm, tn = _tile(a.shape[0], tm_target), _tile(w.shape[1], tn_target, LANE)
    return _tiled_matmul(a, [w], [row_scale], [_row_scale_spec(row_scale, tm)], sides, _plain_scaled_epilogue,
                         out_dtype, tm, tn, "matmul_scaled")


def _layer_weight_spec(layer, k, tn):
    return pl.BlockSpec((None, k, tn), lambda j, i: (layer, 0, j))


def matmul_residual(a, w, res, scale, tm_target, tn_target, sides=(), norm_gain=None):
    n = res.shape[1]
    tm, tn = _tile(a.shape[0], tm_target), _tile(n, tn_target, LANE)
    return _tiled_matmul(a, [w], [res], [pl.BlockSpec((tm, tn), lambda j, i: (i, j))], sides,
                         functools.partial(_residual_epilogue, scale=scale), F32, tm, tn,
                         "matmul_residual", norm_gain)


def matmul_residual_pair(a, w, res_p, res_s, scale, tn_target, sides=(), norm_gain=None):
    tm = _tile(math.gcd(res_p.shape[0], res_s.shape[0]), 256)
    tn = _tile(res_p.shape[1], tn_target, LANE)
    n_first, spec_p, spec_s = _pair_specs(res_p, res_s, tm, tn, 1, lambda j, i: j)
    return _tiled_matmul(a, [w], [res_p, res_s], [spec_p, spec_s], sides,
                         functools.partial(_residual_pair_epilogue, scale=scale, n_first=n_first),
                         F32, tm, tn, "matmul_residual_pair", norm_gain)


def matmul_mix(ya, yb, yc, w, res, tm_target, tn_target, norm_gain=None):
    tm, tn = _tile(ya.shape[0], tm_target), _tile(w.shape[1], tn_target, LANE)
    row = lambda y: pl.BlockSpec((tm, y.shape[1]), lambda j, i: (i, 0))
    return _tiled_matmul(ya, [w], [yb, yc, res],
                         [row(yb), row(yc), pl.BlockSpec((tm, tn), lambda j, i: (i, j))], (),
                         _mix_epilogue, F32, tm, tn, "matmul_mix", norm_gain)


def matmul_swiglu(a, w1, w3, tm_target, tn_target, sides=(), row_scale=None):
    n = w1[0].shape[2] if isinstance(w1, tuple) else w1.shape[1]
    tm, tn = _tile(a.shape[0], tm_target), _tile(n, tn_target, LANE)
    if row_scale is None:
        return _tiled_matmul(a, [w1, w3], [], [], sides, _swiglu_epilogue, BF16, tm, tn,
                             "matmul_swiglu")
    return _tiled_matmul(a, [w1, w3], [row_scale], [_row_scale_spec(row_scale, tm)], sides,
                         _swiglu_scaled_epilogue, BF16, tm, tn, "matmul_swiglu")


def _mla_prep_kernel(qa_ref, tail_ref, qn_ref, kvn_ref, wq_ref, wuk_ref, cos_ref, sin_ref,
                     qlat_ref, qrope_ref, lat_ref, krope_ref, klat_bf_ref, krope_bf_ref):
    hq = _rms(qa_ref[...], qn_ref[...]).astype(BF16)
    q_all = _dot(hq, wq_ref[...])
    cos = cos_ref[...]
    sin = sin_ref[...]
    hw = MLA_HEADS * LANE
    for h in range(MLA_HEADS):
        qn = q_all[:, h * LANE:(h + 1) * LANE].astype(BF16)
        qlat_ref[h] = (_dot(qn, wuk_ref[h]) * MLA_SCALE).astype(BF16)
        qr = (q_all[:, hw + h * LANE:hw + (h + 1) * LANE] * cos
              + q_all[:, 2 * hw + h * LANE:2 * hw + (h + 1) * LANE] * sin)
        qrope_ref[h] = (qr[:, :QK_ROPE] * MLA_SCALE).astype(BF16)
    tail = tail_ref[...]
    lat = _rms(tail[:, :KV_LORA], kvn_ref[...])
    lat_ref[...] = lat
    klat_bf_ref[...] = lat.astype(BF16)
    kr = tail[:, TAIL_KR:TAIL_KR + LANE] * cos + tail[:, TAIL_KSW:TAIL_KSW + LANE] * sin
    krope_ref[...] = kr[:, :QK_ROPE]
    krope_bf_ref[...] = kr[:, :QK_ROPE].astype(BF16)


def mla_prep(cols, q_norm, kv_norm, wq_all, wuk_t, cos128, sin128):
    m = cols.shape[0]
    tm = _tile(m, 256)
    h = MLA_HEADS
    const2 = lambda i: (0, 0)
    return pl.pallas_call(
        _mla_prep_kernel,
        grid=(m // tm,),
        in_specs=[pl.BlockSpec((tm, SLOT), lambda i: (i, COL_QA)),
                  pl.BlockSpec((tm, SLOT), lambda i: (i, COL_TAIL)),
                  pl.BlockSpec((1, Q_LORA), const2),
                  pl.BlockSpec((1, KV_LORA), const2),
                  pl.BlockSpec(wq_all.shape, const2, pipeline_mode=pl.Buffered(1)),
                  pl.BlockSpec(wuk_t.shape, lambda i: (0, 0, 0), pipeline_mode=pl.Buffered(1)),
                  pl.BlockSpec((tm, LANE), lambda i: (i, 0)),
                  pl.BlockSpec((tm, LANE), lambda i: (i, 0))],
        out_specs=[pl.BlockSpec((h, tm, KV_LORA), lambda i: (0, i, 0)),
                   pl.BlockSpec((h, tm, QK_ROPE), lambda i: (0, i, 0)),
                   pl.BlockSpec((tm, KV_LORA), lambda i: (i, 0)),
                   pl.BlockSpec((tm, QK_ROPE), lambda i: (i, 0)),
                   pl.BlockSpec((tm, KV_LORA), lambda i: (i, 0)),
                   pl.BlockSpec((tm, QK_ROPE), lambda i: (i, 0))],
        out_shape=[jax.ShapeDtypeStruct((h, m, KV_LORA), BF16),
                   jax.ShapeDtypeStruct((h, m, QK_ROPE), BF16),
                   jax.ShapeDtypeStruct((m, KV_LORA), F32),
                   jax.ShapeDtypeStruct((m, QK_ROPE), F32),
                   jax.ShapeDtypeStruct((m, KV_LORA), BF16),
                   jax.ShapeDtypeStruct((m, QK_ROPE), BF16)],
        compiler_params=_params("parallel"),
        name="mla_prep",
    )(cols, cols, q_norm.reshape(1, -1), kv_norm.reshape(1, -1), wq_all, wuk_t, cos128, sin128)


def _softmax_init(m_ref, l_ref, acc_ref):
    m_ref[...] = jnp.full(m_ref.shape, NEG, F32)
    l_ref[...] = jnp.zeros(l_ref.shape, F32)
    acc_ref[...] = jnp.zeros(acc_ref.shape, F32)


def _mla_step(ql_ref, qr_ref, k_l, k_r, m_ref, l_ref, acc_ref, tq, visible=None):
    hg = MLA_HEADS // MLA_HEAD_GROUPS
    rows = hg * tq

    def scores(g):
        q_l = ql_ref[g * hg:(g + 1) * hg].reshape(rows, KV_LORA)
        q_r = qr_ref[g * hg:(g + 1) * hg].reshape(rows, QK_ROPE)
        s = _dot_nt(q_l, k_l) + _dot_nt(q_r, k_r)
        return s if visible is None else jnp.where(visible, s, NEG)

    def update(g, s):
        rs = slice(g * rows, (g + 1) * rows)
        m_old = m_ref[rs]
        m_new = jnp.maximum(m_old, jnp.max(s, axis=-1, keepdims=True))
        alpha = jnp.exp(m_old - m_new)
        p = jnp.exp(s - m_new)
        l_ref[rs] = alpha * l_ref[rs] + jnp.sum(p, axis=-1, keepdims=True)
        acc_ref[rs] = alpha * acc_ref[rs] + _dot(p.astype(BF16), k_l)
        m_ref[rs] = m_new

    s = scores(0)
    for g in range(1, MLA_HEAD_GROUPS):
        s_next = scores(g)
        update(g - 1, s)
        s = s_next
    update(MLA_HEAD_GROUPS - 1, s)


def _mla_finish(wuv_ref, o_ref, l_ref, acc_ref, tq):
    ctx = acc_ref[...] / l_ref[...]
    for h in range(MLA_HEADS):
        ch = ctx[h * tq:(h + 1) * tq].astype(BF16)
        o_ref[:, h * V_HEAD:(h + 1) * V_HEAD] = _dot(ch, wuv_ref[h]).astype(o_ref.dtype)


def _mla_prompt_kernel(qi_ref, kj_ref, ql_ref, qr_ref, kl_ref, kr_ref, wuv_ref, y_prev_ref, o_ref,
                       m_ref, l_ref, acc_ref, *, tq, tk):
    del y_prev_ref
    qi = qi_ref[pl.program_id(1)]
    kj = kj_ref[pl.program_id(1)]
    last = (qi * tq + tq - 1) // tk

    @pl.when(kj == 0)
    def _():
        _softmax_init(m_ref, l_ref, acc_ref)

    @pl.when(kj < last)
    def _():
        _mla_step(ql_ref, qr_ref, kl_ref[...], kr_ref[...], m_ref, l_ref, acc_ref, tq)

    @pl.when(kj == last)
    def _():
        shape = (MLA_HEADS // MLA_HEAD_GROUPS * tq, tk)
        row = lax.broadcasted_iota(jnp.int32, shape, 0)
        col = lax.broadcasted_iota(jnp.int32, shape, 1)
        q_chunk = (qi * tq + row % tq) // CHUNK
        k_chunk = (kj * tk + col) // CHUNK
        _mla_step(ql_ref, qr_ref, kl_ref[...], kr_ref[...], m_ref, l_ref, acc_ref, tq,
                  visible=k_chunk <= q_chunk)
        _mla_finish(wuv_ref, o_ref, l_ref, acc_ref, tq)


def mla_attention_prompt(q_lat, q_rope, k_lat, k_rope, wuv, y_prev, batch, seq):
    tq = _tile(seq, 128, CHUNK)
    tk = _tile(seq, 512, CHUNK)
    nq, nk = seq // tq, seq // tk
    h = MLA_HEADS
    rows = h * tq
    pairs = [(qi, kj) for qi in range(nq) for kj in range((qi * tq + tq - 1) // tk + 1)]
    qi_tab = jnp.asarray([p[0] for p in pairs], jnp.int32)
    kj_tab = jnp.asarray([p[1] for p in pairs], jnp.int32)
    q_map = lambda b, t, qi_ref, kj_ref: (0, b * nq + qi_ref[t], 0)
    k_map = lambda b, t, qi_ref, kj_ref: (b * nk + kj_ref[t], 0)
    return pl.pallas_call(
        functools.partial(_mla_prompt_kernel, tq=tq, tk=tk),
        grid_spec=pltpu.PrefetchScalarGridSpec(
            num_scalar_prefetch=2,
            grid=(batch, len(pairs)),
            in_specs=[pl.BlockSpec((h, tq, KV_LORA), q_map),
                      pl.BlockSpec((h, tq, QK_ROPE), q_map),
                      pl.BlockSpec((tk, KV_LORA), k_map),
                      pl.BlockSpec((tk, QK_ROPE), k_map),
                      pl.BlockSpec(wuv.shape, lambda b, t, qi_ref, kj_ref: (0, 0, 0)),
                      pl.BlockSpec(memory_space=pl.ANY)],
            out_specs=pl.BlockSpec((tq, h * V_HEAD),
                                   lambda b, t, qi_ref, kj_ref: (b * nq + qi_ref[t], 0)),
            scratch_shapes=[pltpu.VMEM((rows, 1), F32), pltpu.VMEM((rows, 1), F32),
                            pltpu.VMEM((rows, KV_LORA), F32)]),
        out_shape=jax.ShapeDtypeStruct(y_prev.shape, y_prev.dtype),
        input_output_aliases={7: 0},
        compiler_params=_params("parallel", "arbitrary"),
        name="mla_attention_prompt",
    )(qi_tab, kj_tab, q_lat, q_rope, k_lat, k_rope, wuv, y_prev)


def _mla_sample_kernel(ql_ref, qr_ref, pl_ref, pr_ref, kl_ref, kr_ref, wuv_ref, y_prev_ref, o_ref,
                       m_ref, l_ref, acc_ref):
    del y_prev_ref
    kj = pl.program_id(1)
    n_past = pl.num_programs(1) - 1

    @pl.when(kj == 0)
    def _():
        _softmax_init(m_ref, l_ref, acc_ref)

    @pl.when(kj < n_past)
    def _():
        _mla_step(ql_ref, qr_ref, pl_ref[0].astype(BF16), pr_ref[0].astype(BF16),
                  m_ref, l_ref, acc_ref, CHUNK)

    @pl.when(kj == n_past)
    def _():
        _mla_step(ql_ref, qr_ref, kl_ref[...], kr_ref[...], m_ref, l_ref, acc_ref, CHUNK)
        _mla_finish(wuv_ref, o_ref, l_ref, acc_ref, CHUNK)


def mla_attention_sample(q_lat, q_rope, past_lat, past_rope, layer, k_lat, k_rope, wuv, y_prev,
                         batch, tok_block0):
    past = past_lat.shape[2]
    tk = _tile(past, 2048, CHUNK)
    n_past = past // tk
    h = MLA_HEADS
    rows = h * CHUNK
    past_map = lambda b, kj: (layer, b, jnp.minimum(kj, n_past - 1), 0)
    return pl.pallas_call(
        _mla_sample_kernel,
        grid=(batch, n_past + 1),
        in_specs=[pl.BlockSpec((h, CHUNK, KV_LORA), lambda b, kj: (0, tok_block0 + b, 0)),
                  pl.BlockSpec((h, CHUNK, QK_ROPE), lambda b, kj: (0, tok_block0 + b, 0)),
                  pl.BlockSpec((None, 1, tk, KV_LORA), past_map),
                  pl.BlockSpec((None, 1, tk, QK_ROPE), past_map),
                  pl.BlockSpec((CHUNK, KV_LORA), lambda b, kj: (tok_block0 + b, 0)),
                  pl.BlockSpec((CHUNK, QK_ROPE), lambda b, kj: (tok_block0 + b, 0)),
                  pl.BlockSpec(wuv.shape, lambda b, kj: (0, 0, 0)),
                  pl.BlockSpec(memory_space=pl.ANY)],
        out_specs=pl.BlockSpec((CHUNK, h * V_HEAD), lambda b, kj: (tok_block0 + b, 0)),
        out_shape=jax.ShapeDtypeStruct(y_prev.shape, y_prev.dtype),
        input_output_aliases={7: 0},
        scratch_shapes=[pltpu.VMEM((rows, 1), F32), pltpu.VMEM((rows, 1), F32),
                        pltpu.VMEM((rows, KV_LORA), F32)],
        compiler_params=_params("parallel", "arbitrary"),
        name="mla_attention_sample",
    )(q_lat, q_rope, past_lat, past_rope, k_lat, k_rope, wuv, y_prev)


def _band_span(qchunks):
    tq, nk = qchunks * CHUNK, (qchunks + BAND_PREV) * CHUNK
    return -(-(tq + nk - 1) // LANE) * LANE


def _band_kernel(q_ref, k_ref, v_ref, g_ref, y_prev_ref, o_ref, bias_ref, *, pad_rows, qchunks):
    del y_prev_ref
    n = pl.program_id(1)
    tq, nk = qchunks * CHUNK, (qchunks + BAND_PREV) * CHUNK
    span = _band_span(qchunks)

    @pl.when(jnp.logical_and(pl.program_id(0) == 0, n == 0))
    def _():
        for h in range(BAND_HEADS):
            rows = jnp.broadcast_to(g_ref[h:h + 1, :], (tq, span))
            rolled = pltpu.roll(rows, span - (tq - 1), 1, stride=1, stride_axis=0)
            bias_ref[h] = rolled[:, :nk]

    start = pl.multiple_of(n * tq, CHUNK)
    row = lax.broadcasted_iota(jnp.int32, (tq, nk), 0)
    col = lax.broadcasted_iota(jnp.int32, (tq, nk), 1)
    visible = start + col >= pad_rows
    if qchunks > 1:
        ahead = (col >> CHUNK_SHIFT) - (row >> CHUNK_SHIFT)
        visible = visible & (ahead >= 0) & (ahead <= BAND_PREV)
    q = q_ref[...]
    for h in range(BAND_HEADS):
        lanes = slice(h * BAND_DIM, (h + 1) * BAND_DIM)
        qh = q[:, lanes].astype(BF16)
        kh = k_ref[0, pl.ds(start, nk), lanes]
        vh = v_ref[0, pl.ds(start, nk), lanes]
        s = _dot_nt(qh, kh) * BAND_SCALE + bias_ref[h]
        if pad_rows or qchunks > 1:
            s = jnp.where(visible, s, NEG)
        p = jnp.exp(s - jnp.max(s, axis=-1, keepdims=True))
        o = _dot(p.astype(BF16), vh) / jnp.sum(p, axis=-1, keepdims=True)
        o_ref[:, lanes] = o.astype(o_ref.dtype)


def band_attention(cols, k_rows, v_rows, rel_bias, y_prev, batch, nchunks, tok_block0, pad_rows, qchunks):
    assert nchunks % qchunks == 0 and tok_block0 % qchunks == 0
    width = BAND_HEADS * BAND_DIM
    rows = k_rows.shape[1]
    tq, nk = qchunks * CHUNK, (qchunks + BAND_PREV) * CHUNK
    nblk = nchunks // qchunks
    bias_row = _band_bias_row(rel_bias, qchunks)
    tok = lambda b, n: tok_block0 // qchunks + b * nblk + n
    return pl.pallas_call(
        functools.partial(_band_kernel, pad_rows=pad_rows, qchunks=qchunks),
        grid=(batch, nblk),
        in_specs=[pl.BlockSpec((tq, width), lambda b, n: (tok(b, n), COL_BQ)),
                  pl.BlockSpec((1, rows, width), lambda b, n: (b, 0, 0)),
                  pl.BlockSpec((1, rows, width), lambda b, n: (b, 0, 0)),
                  pl.BlockSpec(bias_row.shape, lambda b, n: (0, 0)),
                  pl.BlockSpec(memory_space=pl.ANY)],
        out_specs=pl.BlockSpec((tq, width), lambda b, n: (tok(b, n), 0)),
        out_shape=jax.ShapeDtypeStruct(y_prev.shape, y_prev.dtype),
        input_output_aliases={4: 0},
        scratch_shapes=[pltpu.VMEM((BAND_HEADS, tq, nk), F32)],
        compiler_params=_params("arbitrary", "arbitrary"),
        name="band_attention",
    )(cols, k_rows, v_rows, bias_row, y_prev)


def _expand_matrix(rows, first, width, per_shift):
    r = lax.broadcasted_iota(jnp.int32, (rows, width), 0)
    c = lax.broadcasted_iota(jnp.int32, (rows, width), 1)
    return (r == first + (c >> per_shift)).astype(F32)


def _unit_lower_inverse(a, row, col, group):
    gl = a.shape[1]
    r = lax.broadcasted_iota(jnp.int32, (gl, gl), 0)
    c = lax.broadcasted_iota(jnp.int32, (gl, gl), 1)
    diag_blocks = ((r >> CHUNK_SHIFT) == (c >> CHUNK_SHIFT)).astype(BF16)

    def block_diag(w):
        return jnp.concatenate([w] * group, axis=0) * diag_blocks

    t = (row == col).astype(F32) - jnp.where((row >> 1) == (col >> 1), a, 0.0)
    for shift in range(1, CHUNK_SHIFT):
        lower_left = ((row >> (shift + 1)) == (col >> (shift + 1))) & ((row >> shift) != (col >> shift))
        x = _dot_3pass(t, jnp.where(lower_left, a, 0.0), block_diag)
        t = t - _dot_3pass(x, t, block_diag)
    return t


def _l2norm_heads(x, heads, width):
    parts = []
    for h in range(heads):
        xh = x[:, h * width:(h + 1) * width]
        parts.append(xh * lax.rsqrt(jnp.sum(xh * xh, axis=-1, keepdims=True) + EPS))
    return jnp.concatenate(parts, axis=1)


def _gdn_pre_kernel(u_ref, uprev_ref, init_ref, tail_ref, cw_ref, aneg_ref, dtb_ref,
                    uv_ref, wk_ref, attn_ref, qg_ref, kt_ref, gl_ref, ext_ref,
                    *, chunks_per_seq, n_prompt_chunks):
    group = GDN_GROUP
    gw = group * CHUNK
    c = pl.program_id(0)
    is_start = jnp.logical_or(c >= n_prompt_chunks, c % chunks_per_seq == 0)
    ext_ref[0:8, :] = jnp.where(is_start, init_ref[0], uprev_ref[...])
    ext_ref[8:8 + CHUNK, :] = u_ref[...]
    cw = cw_ref[...]
    uc = (cw[3:4] * ext_ref[8:8 + CHUNK, :] + cw[2:3] * ext_ref[7:7 + CHUNK, :]
          + cw[1:2] * ext_ref[6:6 + CHUNK, :] + cw[0:1] * ext_ref[5:5 + CHUNK, :])
    uc = uc * _sigmoid(uc)

    tail = tail_ref[...]
    xa = tail[:, TAIL_A:TAIL_A + LANE] + dtb_ref[...]
    softplus = jnp.maximum(xa, 0.0) + jnp.log(1.0 + jnp.exp(-jnp.abs(xa)))
    g = aneg_ref[...] * softplus
    beta = _sigmoid(tail[:, TAIL_B:TAIL_B + LANE])

    r64 = lax.broadcasted_iota(jnp.int32, (CHUNK, CHUNK), 0)
    c64 = lax.broadcasted_iota(jnp.int32, (CHUNK, CHUNK), 1)
    gc = _dot((r64 >= c64).astype(F32), g, HIGHEST)
    egc = jnp.exp(gc)
    ekt = jnp.exp(gc[CHUNK - 1:CHUNK, :] - gc)
    e128 = _expand_matrix(LANE, 0, GDN_HEADS * GDN_DK, 7)
    beta_x = _dot(beta, e128, HIGHEST)
    egc_x = _dot(egc, e128, HIGHEST)
    ekt_x = _dot(ekt, e128, HIGHEST)
    gl_ref[0] = egc_x[CHUNK - 1:CHUNK, :]

    qn = _l2norm_heads(uc[:, :GDN_QK], GDN_HEADS, GDN_DK) * (GDN_DK ** -0.5)
    kn = _l2norm_heads(uc[:, GDN_QK:2 * GDN_QK], GDN_HEADS, GDN_DK)
    v_all = uc[:, 2 * GDN_QK:]
    qg_ref[...] = (qn * egc_x).astype(BF16)
    kt_ref[...] = (kn * ekt_x).astype(BF16)
    rhs_v = beta_x * v_all
    rhs_k = beta_x * egc_x * kn

    row = lax.broadcasted_iota(jnp.int32, (CHUNK, gw), 0)
    col = lax.broadcasted_iota(jnp.int32, (CHUNK, gw), 1) & (CHUNK - 1)
    incl = row >= col
    head_r = lax.broadcasted_iota(jnp.int32, (gw, group * GDN_DK), 0) >> CHUNK_SHIFT
    head_c = lax.broadcasted_iota(jnp.int32, (gw, group * GDN_DK), 1) >> 7
    for grp in range(GDN_HEADS // group):
        lanes = slice(grp * group * GDN_DK, (grp + 1) * group * GDN_DK)
        e64 = _expand_matrix(LANE, grp * group, gw, CHUNK_SHIFT)
        gcw = _dot(gc, e64, HIGHEST)
        bw = _dot(beta, e64, HIGHEST)
        gc_t = jnp.sum(jnp.where(row == col, gcw, 0.0), axis=0, keepdims=True)
        decay = jnp.where(incl, jnp.exp(jnp.where(incl, gcw - gc_t, 0.0)), 0.0)
        kg = kn[:, lanes]
        k_rows = jnp.where(head_r == head_c, jnp.concatenate([kg] * group, axis=0), 0.0).astype(BF16)
        kk = _dot_nt(kg.astype(BF16), k_rows)
        qk = _dot_nt(qn[:, lanes].astype(BF16), k_rows)
        a_mat = jnp.where(row > col, bw * kk * decay, 0.0)
        t_inv = _unit_lower_inverse(a_mat, row, col, group)
        attn_ref[:, grp * gw:(grp + 1) * gw] = (qk * decay).astype(BF16)
        for hh in range(group):
            h = grp * group + hh
            hl = slice(h * GDN_DK, (h + 1) * GDN_DK)
            rhs = jnp.concatenate([rhs_v[:, hl], rhs_k[:, hl]], axis=1)
            sol = _dot_3pass(t_inv[:, hh * CHUNK:(hh + 1) * CHUNK], rhs)
            uv_ref[:, hl] = sol[:, :GDN_DV]
            wk_ref[:, hl] = sol[:, GDN_DV:].astype(BF16)


def gdn_pre(cols, conv_init, conv_w, aneg, dtb, chunks_per_seq, n_prompt_chunks):
    m = cols.shape[0]
    nch = m // CHUNK
    h = GDN_HEADS

    def init_map(c):
        return (jnp.where(c < n_prompt_chunks, c // chunks_per_seq,
                          n_prompt_chunks // chunks_per_seq + c - n_prompt_chunks), 0, 0)

    tok_spec = lambda d: pl.BlockSpec((CHUNK, d), lambda c: (c, 0))
    return pl.pallas_call(
        functools.partial(_gdn_pre_kernel, chunks_per_seq=chunks_per_seq,
                          n_prompt_chunks=n_prompt_chunks),
        grid=(nch,),
        in_specs=[pl.BlockSpec((CHUNK, C_CONV), lambda c: (c, COL_U)),
                  pl.BlockSpec((8, C_CONV), lambda c: (jnp.maximum(c * (CHUNK // 8) - 1, 0), COL_U)),
                  pl.BlockSpec((1, 8, C_CONV), init_map),
                  pl.BlockSpec((CHUNK, SLOT), lambda c: (c, COL_TAIL)),
                  pl.BlockSpec((CONV_W, C_CONV), lambda c: (0, 0)),
                  pl.BlockSpec((1, LANE), lambda c: (0, 0)),
                  pl.BlockSpec((1, LANE), lambda c: (0, 0))],
        out_specs=[tok_spec(GDN_V), tok_spec(GDN_QK), tok_spec(h * CHUNK), tok_spec(GDN_QK),
                   tok_spec(GDN_QK), pl.BlockSpec((1, 1, GDN_QK), lambda c: (c, 0, 0))],
        out_shape=[jax.ShapeDtypeStruct((m, GDN_V), F32),
                   jax.ShapeDtypeStruct((m, GDN_QK), BF16),
                   jax.ShapeDtypeStruct((m, h * CHUNK), BF16),
                   jax.ShapeDtypeStruct((m, GDN_QK), BF16),
                   jax.ShapeDtypeStruct((m, GDN_QK), BF16),
                   jax.ShapeDtypeStruct((nch, 1, GDN_QK), F32)],
        scratch_shapes=[pltpu.VMEM((8 + CHUNK, C_CONV), F32)],
        compiler_params=_params("parallel"),
        name="gdn_pre",
    )(cols, cols, conv_init, cols, conv_w, aneg, dtb)


def _gdn_rec_kernel(uv_ref, wk_ref, attn_ref, qg_ref, kt_ref, gl_ref, s0_ref, z_ref, gn_ref,
                    y_prev_ref, y_ref, s_ref):
    del y_prev_ref

    @pl.when(pl.program_id(1) == 0)
    def _():
        s_ref[...] = s0_ref[...]

    z = z_ref[...]
    gn = gn_ref[...]
    heads = range(GDN_HEADS)
    lanes = [slice(h * GDN_DK, (h + 1) * GDN_DK) for h in heads]
    s_old = [s_ref[0, h] for h in heads]
    s_bf = [s.astype(BF16) for s in s_old]
    ws = [_dot(wk_ref[:, lanes[h]], s_bf[h]) for h in heads]
    qs = [_dot(qg_ref[:, lanes[h]], s_bf[h]) for h in heads]
    u_bf = [(uv_ref[:, lanes[h]] - ws[h]).astype(BF16) for h in heads]
    outs = [qs[h] + _dot(attn_ref[:, h * CHUNK:(h + 1) * CHUNK], u_bf[h]) for h in heads]
    for h in heads:
        s_ref[0, h] = gl_ref[0, :, lanes[h]] * s_old[h] + _dot_tn(kt_ref[:, lanes[h]], u_bf[h])
    for h in heads:
        zh = z[:, lanes[h]]
        y_ref[:, lanes[h]] = (_rms(outs[h], gn) * (zh * _sigmoid(zh))).astype(y_ref.dtype)


def gdn_recurrence(pre, cols, s0, gdn_norm, y_prev, batch, nchunks, tok_block0):
    uv, wk, attn, qg, kt, gl = pre
    h = GDN_HEADS
    tok = lambda b, n: tok_block0 + b * nchunks + n
    tok_spec = lambda d: pl.BlockSpec((CHUNK, d), lambda b, n: (tok(b, n), 0))
    state_spec = pl.BlockSpec((1, h, GDN_DK, GDN_DV), lambda b, n: (b, 0, 0, 0))
    return pl.pallas_call(
        _gdn_rec_kernel,
        grid=(batch, nchunks),
        in_specs=[tok_spec(GDN_V), tok_spec(GDN_QK), tok_spec(h * CHUNK), tok_spec(GDN_QK),
                  tok_spec(GDN_QK),
                  pl.BlockSpec((1, 1, GDN_QK), lambda b, n: (tok(b, n), 0, 0)),
                  state_spec,
                  pl.BlockSpec((CHUNK, SLOT), lambda b, n: (tok(b, n), COL_Z)),
                  pl.BlockSpec((1, GDN_DV), lambda b, n: (0, 0)),
                  pl.BlockSpec(memory_space=pl.ANY)],
        out_specs=[tok_spec(GDN_V), state_spec],
        out_shape=[jax.ShapeDtypeStruct(y_prev.shape, y_prev.dtype),
                   jax.ShapeDtypeStruct((batch, h, GDN_DK, GDN_DV), F32)],
        input_output_aliases={9: 0},
        compiler_params=_params("parallel", "arbitrary"),
        name="gdn_recurrence",
    )(uv, wk, attn, qg, kt, gl, s0, cols, gdn_norm.reshape(1, -1), y_prev)


def _pad_cols(w, width):
    return jnp.pad(w, ((0, 0), (0, width - w.shape[1])))


def _layout_w_in(w):
    a0, b0 = 0, Q_LORA + KV_LORA + QK_ROPE
    c0 = b0 + 3 * BAND_HEADS * BAND_DIM
    q_a = w[:, a0:a0 + Q_LORA]
    c_kv = w[:, a0 + Q_LORA:a0 + Q_LORA + KV_LORA]
    k_r = w[:, a0 + Q_LORA + KV_LORA:b0]
    band = w[:, b0:c0]
    u = w[:, c0:c0 + C_CONV]
    z = w[:, c0 + C_CONV:c0 + C_CONV + GDN_V]
    a = w[:, c0 + C_CONV + GDN_V:c0 + C_CONV + GDN_V + GDN_HEADS]
    b = w[:, c0 + C_CONV + GDN_V + GDN_HEADS:]
    half = QK_ROPE // 2
    k_sw = jnp.concatenate([k_r[:, half:], k_r[:, :half]], axis=1)
    tail = jnp.concatenate([c_kv, _pad_cols(k_r, LANE), _pad_cols(k_sw, LANE),
                            _pad_cols(a, LANE), _pad_cols(b, LANE)], axis=1)
    return jnp.concatenate([u, q_a, band, z, tail], axis=1).astype(BF16)


def _layout_w_qb(w):
    k = w.shape[0]
    w = w.reshape(k, MLA_HEADS, QK_NOPE + QK_ROPE)
    nope = w[:, :, :QK_NOPE]
    rope = w[:, :, QK_NOPE:]
    half = QK_ROPE // 2
    rope_sw = jnp.concatenate([rope[:, :, half:], rope[:, :, :half]], axis=2)
    pad = lambda t: jnp.pad(t, ((0, 0), (0, 0), (0, LANE - QK_ROPE)))
    out = jnp.concatenate([nope, pad(rope), pad(rope_sw)], axis=1)
    return out.reshape(k, 3 * MLA_HEADS * LANE).astype(BF16)


def _rope_tables(pos):
    inv = 1.0 / (ROPE_THETA ** (jnp.arange(0, QK_ROPE, 2, dtype=F32) / QK_ROPE))
    ang = pos.astype(F32)[:, None] * inv[None, :]
    cos, sin = jnp.cos(ang), jnp.sin(ang)
    zeros = jnp.zeros((pos.shape[0], LANE - QK_ROPE), F32)
    return (jnp.concatenate([cos, cos, zeros], axis=1), jnp.concatenate([-sin, sin, zeros], axis=1))


def _band_bias_row(rel_bias, qchunks):
    heads = rel_bias.shape[0]
    span = _band_span(qchunks)
    left = BAND_ROWS + qchunks * CHUNK - 1 - REL_CLIP
    right = max(span - left - (2 * REL_CLIP + 1), 0)
    row = jnp.concatenate([jnp.broadcast_to(rel_bias[:, -1:], (heads, left)), rel_bias[:, ::-1],
                           jnp.broadcast_to(rel_bias[:, :1], (heads, right))], axis=1)
    return row[:, :span]


def kernel(x_prompt, x_sample, cache_latent, cache_k_rope, cache_band_k, cache_band_v, state_conv, state_delta, norm_ff1, ff1_w1, ff1_w3, ff1_w2, norm_mix, w_in, q_norm, w_qb, kv_norm, w_uk, w_uv, rel_bias, conv_w, a_log, dt_bias, gdn_norm, w_out, norm_ff2, ff2_w1, ff2_w3, ff2_w2, final_norm):
    batch, seq, d = x_prompt.shape
    dbatch, dseq, _ = x_sample.shape
    depth = w_in.shape[0]
    past = cache_latent.shape[2]
    assert d == D_MODEL and dseq == CHUNK and seq % CHUNK == 0 and cache_band_k.shape[2] == BAND_ROWS
    mp, ms = batch * seq, dbatch * dseq
    m = mp + ms
    nchunks = seq // CHUNK
    blk_s = mp // CHUNK
    band_w = BAND_HEADS * BAND_DIM

    xp, xs = x_prompt.reshape(mp, d), x_sample.reshape(ms, d)
    pos = jnp.concatenate([jnp.tile(jnp.arange(seq, dtype=jnp.int32), batch),
                           jnp.tile(past + jnp.arange(dseq, dtype=jnp.int32), dbatch)])
    cos128, sin128 = _rope_tables(pos)
    zero_state = jnp.zeros((batch, GDN_HEADS, GDN_DK, GDN_DV), F32)

    def ffn(x, normed, w13_bf, w2, l, next_w13, next_gain):
        hid, (w2_bf,), _ = matmul_swiglu(normed[0], w13_bf[0], w13_bf[1], 544, 1024, sides=[(w2, l)],
                                         row_scale=normed[1])
        return matmul_residual(hid, w2_bf, x, 0.5, 272, 1024, sides=next_w13, norm_gain=next_gain)

    def mixer_rows(width):
        return jnp.zeros((m, width), BF16)

    outs = [[] for _ in range(12)]
    for l in range(depth):
        if l == 0:
            hid, (w2_bf,), _ = matmul_swiglu(rmsnorm_pair(xp, xs, norm_ff1[0], BF16), (ff1_w1, 0),
                                             (ff1_w3, 0), 544, 512, sides=[(ff1_w2, 0)])
            x, w13_bf, normed = matmul_residual_pair(hid, w2_bf, xp, xs, 0.5, 1024,
                                                     sides=[(ff2_w1, 0), (ff2_w3, 0)],
                                                     norm_gain=norm_mix[0])
        else:
            x, w13_bf, normed = ffn(x, normed, w13_bf, ff1_w2, l, [(ff2_w1, l), (ff2_w3, l)],
                                    norm_mix[l])

        cols, (w_out_bf,), _ = matmul_scaled(normed[0], normed[1], _layout_w_in(w_in[l]), F32,
                                             544, 1536, sides=[(w_out, l)])

        wuk_t = jnp.transpose(w_uk[l], (1, 2, 0)).astype(BF16)
        wuv_t = jnp.transpose(w_uv[l], (1, 0, 2)).astype(BF16)
        q_lat, q_rope, lat, krope, klat_bf, krope_bf = mla_prep(
            cols, q_norm[l], kv_norm[l], _layout_w_qb(w_qb[l]), wuk_t, cos128, sin128)
        ya = mla_attention_prompt(q_lat, q_rope, klat_bf, krope_bf, wuv_t,
                                  mixer_rows(MLA_HEADS * V_HEAD), batch, seq)
        ya = mla_attention_sample(q_lat, q_rope, cache_latent, cache_k_rope, l, klat_bf, krope_bf,
                                  wuv_t, ya, dbatch, blk_s)

        band_k = cols[:, COL_BK * SLOT:(COL_BK + 1) * SLOT]
        band_v = cols[:, COL_BV * SLOT:(COL_BV + 1) * SLOT]

        def prompt_rows(t):
            t = t[:mp].reshape(batch, seq, band_w).astype(BF16)
            return jnp.pad(t, ((0, 0), (BAND_ROWS, 0), (0, 0)))

        def sample_rows(t, cache):
            return jnp.concatenate([cache.reshape(dbatch, BAND_ROWS, band_w).astype(BF16),
                                    t[mp:].reshape(dbatch, dseq, band_w).astype(BF16)], axis=1)

        yb = band_attention(cols, prompt_rows(band_k), prompt_rows(band_v), rel_bias[l],
                            mixer_rows(band_w), batch, nchunks, 0, BAND_ROWS, math.gcd(nchunks, 4))
        yb = band_attention(cols, sample_rows(band_k, cache_band_k[l]),
                            sample_rows(band_v, cache_band_v[l]), rel_bias[l], yb, dbatch, 1, blk_s, 0, 1)

        conv_init = jnp.concatenate(
            [jnp.zeros((batch, 8, C_CONV), F32),
             jnp.pad(state_conv[l], ((0, 0), (8 - (CONV_W - 1), 0), (0, 0)))], axis=0)
        aneg = _pad_cols(-jnp.exp(a_log[l].astype(F32)).reshape(1, -1), LANE)
        dtb = _pad_cols(dt_bias[l].astype(F32).reshape(1, -1), LANE)
        pre = gdn_pre(cols, conv_init, conv_w[l], aneg, dtb, nchunks, mp // CHUNK)
        yc, delta_p = gdn_recurrence(pre, cols, zero_state, gdn_norm[l], mixer_rows(GDN_V),
                                     batch, nchunks, 0)
        yc, delta_s = gdn_recurrence(pre, cols, state_delta[l], gdn_norm[l], yc, dbatch, 1, blk_s)

        x, _, normed = matmul_mix(ya, yb, yc, w_out_bf, x, 544, 1024, norm_gain=norm_ff2[l])
        last = l + 1 == depth
        x, w13_bf, normed = ffn(x, normed, w13_bf, ff2_w2, l,
                                [] if last else [(ff1_w1, l + 1), (ff1_w3, l + 1)],
                                None if last else norm_ff1[l + 1])

        def conv_tail(first_row, rows_per_stream, streams):
            ends = [first_row + (s + 1) * rows_per_stream for s in range(streams)]
            return jnp.stack([cols[e - (CONV_W - 1):e, :C_CONV] for e in ends])

        keep = min(BAND_ROWS, seq)
        layer_out = (
            lat[:mp].reshape(batch, seq, KV_LORA),
            krope[:mp].reshape(batch, seq, QK_ROPE),
            band_k[:mp].reshape(batch, seq, BAND_HEADS, BAND_DIM)[:, seq - keep:],
            band_v[:mp].reshape(batch, seq, BAND_HEADS, BAND_DIM)[:, seq - keep:],
            conv_tail(0, seq, batch),
            delta_p,
            lat[mp:].reshape(dbatch, dseq, KV_LORA),
            krope[mp:].reshape(dbatch, dseq, QK_ROPE),
            band_k[mp:].reshape(dbatch, dseq, BAND_HEADS, BAND_DIM),
            band_v[mp:].reshape(dbatch, dseq, BAND_HEADS, BAND_DIM),
            conv_tail(mp, dseq, dbatch),
            delta_s,
        )
        for acc, t in zip(outs, layer_out):
            acc.append(t)

    y_prompt = rmsnorm(x, final_norm, F32, rows=mp).reshape(batch, seq, d)
    y_sample = rmsnorm(x, final_norm, F32, rows=ms, row_start=mp).reshape(dbatch, dseq, d)
    return (y_prompt, y_sample) + tuple(jnp.stack(t) for t in outs)
```
